```python
import math
import jax
import jax.numpy as jnp
from jax import lax
import numpy as np

D_MODEL = 4096
BATCH = 1
SEQ = 8192
DEPTH = 2
DEC_BATCH = 8
DEC_SEQ = 32
PAST_LEN = 2048

CHUNK = 64
N_A_LAYERS = DEPTH // 2
N_B_LAYERS = DEPTH - N_A_LAYERS
RMS_EPS = 1e-6

SSM_HEAD_DIM = 64
SSM_INNER = 3 * D_MODEL // 2
SSM_HEADS = SSM_INNER // SSM_HEAD_DIM
SSM_GROUPS = 8
SSM_HPG = SSM_HEADS // SSM_GROUPS
SSM_STATE = 128
CONV_WIDTH = 4
GROUP_STATE = SSM_GROUPS * SSM_STATE
CONV_DIM = SSM_INNER + 2 * GROUP_STATE
A_SSM_IN = SSM_INNER + CONV_DIM + SSM_HEADS

B_HEAD_DIM = 128
B_Q_WIDTH = 3 * D_MODEL // 2
B_HEADS = B_Q_WIDTH // B_HEAD_DIM
B_KV_HEADS = 8
B_GQ = B_HEADS // B_KV_HEADS
BAND_CHUNKS = 8
BAND_PAST = BAND_CHUNKS * CHUNK
BAND_LEN = BAND_PAST + CHUNK
REL_CLIP = 64

MEM_TOKENS = 256
MEM_WIDTH = D_MODEL // 2
MEM_HEADS = 4
MEM_HEAD_DIM = MEM_WIDTH // MEM_HEADS

A_IN_WIDTH = A_SSM_IN + MEM_WIDTH
B_IN_WIDTH = B_Q_WIDTH + MEM_WIDTH
MIX_WIDTH = SSM_INNER + MEM_WIDTH

D_FF = ((-(-8 * D_MODEL // 3) + 255) // 256) * 256

kernel_name = 'yoco_ssd_chunkband_memory_encoder_step'


def rms_norm(x, gain, eps=RMS_EPS):
    xf = x.astype(jnp.float32)
    inv = lax.rsqrt(jnp.mean(xf * xf, axis=-1, keepdims=True) + eps)
    return (xf * inv * gain.astype(jnp.float32)).astype(x.dtype)


def swiglu(x, w_gate, w_up, w_down):
    return (jax.nn.silu(x @ w_gate) * (x @ w_up)) @ w_down


def causal_depthwise_conv(u, prev, w, b):
    L = u.shape[1]
    up = jnp.concatenate([prev.astype(u.dtype), u], axis=1)
    out = b + up[:, 0:L] * w[:, 0]
    for tap in range(1, CONV_WIDTH):
        out = out + up[:, tap:tap + L] * w[:, tap]
    return jax.nn.silu(out), up[:, L:]


def ssd_chunked(x, dt, a, bmat, cmat, h0, block):
    f32 = jnp.float32
    bsz, L = x.shape[:2]
    nc = L // block
    xc = x.astype(f32).reshape(bsz, nc, block, SSM_GROUPS, SSM_HPG, SSM_HEAD_DIM)
    dtc = dt.reshape(bsz, nc, block, SSM_GROUPS, SSM_HPG)
    bc = bmat.astype(f32).reshape(bsz, nc, block, SSM_GROUPS, SSM_STATE)
    cc = cmat.astype(f32).reshape(bsz, nc, block, SSM_GROUPS, SSM_STATE)
    cum = jnp.cumsum(dtc * a.reshape(SSM_GROUPS, SSM_HPG), axis=2)
    diff = cum[:, :, :, None] - cum[:, :, None, :]
    tri = jnp.tril(jnp.ones((block, block), dtype=bool))[:, :, None, None]
    decay = jnp.exp(jnp.where(tri, diff, -jnp.inf))
    xdt = xc * dtc[..., None]
    cb = jnp.einsum('bctgn,bcsgn->bctsg', cc, bc)
    y_diag = jnp.einsum('bctsgk,bcsgkp->bctgkp', cb[..., None] * decay, xdt)
    to_end = jnp.exp(cum[:, :, -1:] - cum)
    states = jnp.einsum('bcsgn,bcsgkp->bcgkpn', bc, xdt * to_end[..., None])
    blk_decay = jnp.exp(cum[:, :, -1])

    def step(h, inp):
        dec, st = inp
        return h * dec[..., None, None] + st, h

    h_init = h0.astype(f32).reshape(bsz, SSM_GROUPS, SSM_HPG, SSM_HEAD_DIM, SSM_STATE)
    h_last, h_in = lax.scan(step, h_init, (jnp.moveaxis(blk_decay, 1, 0), jnp.moveaxis(states, 1, 0)))
    h_in = jnp.moveaxis(h_in, 0, 1)
    y_off = jnp.einsum('bctgn,bcgkpn->bctgkp', cc, h_in) * jnp.exp(cum)[..., None]
    y = (y_diag + y_off).reshape(bsz, L, SSM_HEADS, SSM_HEAD_DIM)
    return y, h_last.reshape(bsz, SSM_HEADS, SSM_HEAD_DIM, SSM_STATE).astype(h0.dtype)


def mamba_mixer(u, conv_prev, ssm_prev, conv_w, conv_b, dt_bias, a_log, d_skip, g_ssm, block):
    bsz, L, _ = u.shape
    z = u[..., :SSM_INNER]
    xbc = u[..., SSM_INNER:SSM_INNER + CONV_DIM]
    dt_raw = u[..., SSM_INNER + CONV_DIM:]
    xbc, conv_new = causal_depthwise_conv(xbc, conv_prev, conv_w, conv_b)
    xs = xbc[..., :SSM_INNER].reshape(bsz, L, SSM_HEADS, SSM_HEAD_DIM)
    bm = xbc[..., SSM_INNER:SSM_INNER + GROUP_STATE].reshape(bsz, L, SSM_GROUPS, SSM_STATE)
    cm = xbc[..., SSM_INNER + GROUP_STATE:].reshape(bsz, L, SSM_GROUPS, SSM_STATE)
    dt = jax.nn.softplus(dt_raw.astype(jnp.float32) + dt_bias.astype(jnp.float32))
    a = -jnp.exp(a_log.astype(jnp.float32))
    y, h_new = ssd_chunked(xs, dt, a, bm, cm, ssm_prev, block)
    y = y + xs.astype(jnp.float32) * d_skip.astype(jnp.float32)[:, None]
    y = y.reshape(bsz, L, SSM_INNER).astype(u.dtype)
    return rms_norm(y * jax.nn.silu(z), g_ssm), conv_new, h_new


def rel_position_bias(table, q_pos, k_pos):
    rel = jnp.clip(q_pos[:, None] - k_pos[None, :], -REL_CLIP, REL_CLIP) + REL_CLIP
    bias = table[:, rel].astype(jnp.float32)
    return bias.reshape(B_KV_HEADS, B_GQ, q_pos.shape[0], k_pos.shape[0])


def band_scores_to_out(qb, kb, vb, bias, valid):
    s = jnp.einsum('bqhgd,bkhd->bhgqk', qb, kb).astype(jnp.float32) / math.sqrt(B_HEAD_DIM) + bias
    s = jnp.where(valid, s, -jnp.inf)
    p = jax.nn.softmax(s, axis=-1).astype(vb.dtype)
    return jnp.einsum('bhgqk,bkhd->bqhgd', p, vb)


def band_attention_prompt(q, k, v, table):
    bsz, S = q.shape[:2]
    nc = S // CHUNK
    pad = ((0, 0), (BAND_PAST, 0), (0, 0), (0, 0))
    kp = jnp.pad(k, pad)
    vp = jnp.pad(v, pad)
    qc = q.reshape(bsz, nc, CHUNK, B_KV_HEADS, B_GQ, B_HEAD_DIM).transpose(1, 0, 2, 3, 4, 5)
    k_off = jnp.arange(BAND_LEN)
    bias = rel_position_bias(table, BAND_PAST + jnp.arange(CHUNK), k_off)

    def one_chunk(args):
        qb, c = args
        kb = lax.dynamic_slice_in_dim(kp, c * CHUNK, BAND_LEN, axis=1)
        vb = lax.dynamic_slice_in_dim(vp, c * CHUNK, BAND_LEN, axis=1)
        valid = (c * CHUNK - BAND_PAST + k_off) >= 0
        return band_scores_to_out(qb, kb, vb, bias, valid)

    o = lax.map(one_chunk, (qc, jnp.arange(nc)))
    return o.transpose(1, 0, 2, 3, 4, 5).reshape(bsz, S, B_Q_WIDTH)


def band_attention_sample(q, k_cache, v_cache, k_new, v_new, table):
    bsz, L = q.shape[:2]
    W = k_cache.shape[1]
    kk = jnp.concatenate([k_cache.astype(k_new.dtype), k_new], axis=1)
    vv = jnp.concatenate([v_cache.astype(v_new.dtype), v_new], axis=1)
    q_pos = PAST_LEN + jnp.arange(L)
    k_pos = jnp.concatenate([PAST_LEN - W + jnp.arange(W), PAST_LEN + jnp.arange(L)])
    bias = rel_position_bias(table, q_pos, k_pos)
    valid = jnp.ones((W + L,), dtype=bool)
    qb = q.reshape(bsz, L, B_KV_HEADS, B_GQ, B_HEAD_DIM)
    return band_scores_to_out(qb, kk, vv, bias, valid).reshape(bsz, L, B_Q_WIDTH)


def shared_kv(h, g_kv, w_kv, k_norm_kv):
    bsz, L, _ = h.shape
    kv = rms_norm(h, g_kv) @ w_kv
    k = kv[..., :B_KV_HEADS * B_HEAD_DIM].reshape(bsz, L, B_KV_HEADS, B_HEAD_DIM)
    v = kv[..., B_KV_HEADS * B_HEAD_DIM:].reshape(bsz, L, B_KV_HEADS, B_HEAD_DIM)
    return rms_norm(k, k_norm_kv), v


def memory_kv(mem, g_mem, w_mem_kv, k_norm_mem):
    bsz, M, _ = mem.shape
    kv = rms_norm(mem, g_mem) @ w_mem_kv
    k = kv[..., :MEM_WIDTH].reshape(bsz, M, MEM_HEADS, MEM_HEAD_DIM)
    v = kv[..., MEM_WIDTH:].reshape(bsz, M, MEM_HEADS, MEM_HEAD_DIM)
    return rms_norm(k, k_norm_mem), v


def memory_attention(q, mem_k, mem_v, q_norm):
    bsz, L, _ = q.shape
    q = rms_norm(q.reshape(bsz, L, MEM_HEADS, MEM_HEAD_DIM), q_norm)
    s = jnp.einsum('bqhd,bkhd->bhqk', q, mem_k.astype(q.dtype)).astype(jnp.float32) / math.sqrt(MEM_HEAD_DIM)
    p = jax.nn.softmax(s, axis=-1).astype(q.dtype)
    o = jnp.einsum('bhqk,bkhd->bqhd', p, mem_v.astype(q.dtype))
    return o.reshape(bsz, L, MEM_WIDTH)


def setup_inputs(seed: int = 0) -> dict:
    key = jax.random.key(seed)
    ks = jax.random.split(key, 40)
    f32 = jnp.float32

    def nrm(k, shape, scale=1.0):
        return scale * jax.random.normal(k, shape, f32)

    def gain(k, shape):
        return 1.0 + 0.01 * jax.random.normal(k, shape, f32)

    w_cache = min(BAND_PAST, PAST_LEN)
    dt0 = jnp.exp(jax.random.uniform(ks[9], (N_A_LAYERS, SSM_HEADS), f32, math.log(1e-3), math.log(1e-1)))
    return {
        'x_prompt': nrm(ks[0], (BATCH, SEQ, D_MODEL)),
        'x_sample': nrm(ks[1], (DEC_BATCH, DEC_SEQ, D_MODEL)),
        'mem_prompt': nrm(ks[2], (BATCH, MEM_TOKENS, D_MODEL)),
        'state_ssm': nrm(ks[3], (N_A_LAYERS, DEC_BATCH, SSM_HEADS, SSM_HEAD_DIM, SSM_STATE), 0.1),
        'state_conv': nrm(ks[4], (N_A_LAYERS, DEC_BATCH, CONV_WIDTH - 1, CONV_DIM)),
        'cache_kv_k': nrm(ks[5], (DEC_BATCH, w_cache, B_KV_HEADS, B_HEAD_DIM)),
        'cache_kv_v': nrm(ks[6], (DEC_BATCH, w_cache, B_KV_HEADS, B_HEAD_DIM)),
        'cache_mem_k': nrm(ks[7], (DEPTH, DEC_BATCH, MEM_TOKENS, MEM_HEADS, MEM_HEAD_DIM)),
        'cache_mem_v': nrm(ks[8], (DEPTH, DEC_BATCH, MEM_TOKENS, MEM_HEADS, MEM_HEAD_DIM)),
        'g_mix': gain(ks[10], (DEPTH, D_MODEL)),
        'w_in_a': nrm(ks[11], (N_A_LAYERS, D_MODEL, A_IN_WIDTH), D_MODEL ** -0.5),
        'conv_w': nrm(ks[12], (N_A_LAYERS, CONV_DIM, CONV_WIDTH), CONV_WIDTH ** -0.5),
        'conv_b': nrm(ks[13], (N_A_LAYERS, CONV_DIM), 0.01),
        'dt_bias': dt0 + jnp.log(-jnp.expm1(-dt0)),
        'a_log': jnp.log(jax.random.uniform(ks[14], (N_A_LAYERS, SSM_HEADS), f32, 1.0, 16.0)),
        'd_skip': gain(ks[15], (N_A_LAYERS, SSM_HEADS)),
        'g_ssm': gain(ks[16], (N_A_LAYERS, SSM_INNER)),
        'w_in_b': nrm(ks[17], (N_B_LAYERS, D_MODEL, B_IN_WIDTH), D_MODEL ** -0.5),
        'rel_bias': nrm(ks[18], (N_B_LAYERS, B_HEADS, 2 * REL_CLIP + 1), 0.1),
        'q_norm_b': gain(ks[19], (N_B_LAYERS, B_HEAD_DIM)),
        'g_kv': gain(ks[20], (D_MODEL,)),
        'w_kv': nrm(ks[21], (D_MODEL, 2 * B_KV_HEADS * B_HEAD_DIM), D_MODEL ** -0.5),
        'k_norm_kv': gain(ks[22], (B_HEAD_DIM,)),
        'g_mem': gain(ks[23], (DEPTH, D_MODEL)),
        'w_mem_kv': nrm(ks[24], (DEPTH, D_MODEL, 2 * MEM_WIDTH), D_MODEL ** -0.5),
        'q_norm_mem': gain(ks[25], (DEPTH, MEM_HEAD_DIM)),
        'k_norm_mem': gain(ks[26], (DEPTH, MEM_HEAD_DIM)),
        'w_out': nrm(ks[27], (DEPTH, MIX_WIDTH, D_MODEL), MIX_WIDTH ** -0.5),
        'g_ffn': gain(ks[28], (DEPTH, D_MODEL)),
        'w_ffn_gate': nrm(ks[29], (DEPTH, D_MODEL, D_FF), D_MODEL ** -0.5),
        'w_ffn_up': nrm(ks[30], (DEPTH, D_MODEL, D_FF), D_MODEL ** -0.5),
        'w_ffn_down': nrm(ks[31], (DEPTH, D_FF, D_MODEL), D_FF ** -0.5),
    }


def reference(x_prompt, x_sample, mem_prompt, state_ssm, state_conv, cache_kv_k, cache_kv_v,
              cache_mem_k, cache_mem_v, g_mix, w_in_a, conv_w, conv_b, dt_bias, a_log, d_skip,
              g_ssm, w_in_b, rel_bias, q_norm_b, g_kv, w_kv, k_norm_kv, g_mem, w_mem_kv,
              q_norm_mem, k_norm_mem, w_out, g_ffn, w_ffn_gate, w_ffn_up, w_ffn_down):

    def run_trunk(x, mem_k, mem_v, conv_prev, ssm_prev, kv_cache):
        bsz, L, _ = x.shape
        block = CHUNK if kv_cache is None else L
        h = x
        conv_new, ssm_new = [], []
        k_sh = v_sh = None
        for layer in range(DEPTH):
            if layer == N_A_LAYERS:
                k_sh, v_sh = shared_kv(h, g_kv, w_kv, k_norm_kv)
            hn = rms_norm(h, g_mix[layer])
            if layer < N_A_LAYERS:
                u = hn @ w_in_a[layer]
                y_mix, c_new, s_new = mamba_mixer(
                    u[..., :A_SSM_IN], conv_prev[layer], ssm_prev[layer], conv_w[layer], conv_b[layer],
                    dt_bias[layer], a_log[layer], d_skip[layer], g_ssm[layer], block)
                conv_new.append(c_new)
                ssm_new.append(s_new)
                q_mem = u[..., A_SSM_IN:]
            else:
                j = layer - N_A_LAYERS
                u = hn @ w_in_b[j]
                q = rms_norm(u[..., :B_Q_WIDTH].reshape(bsz, L, B_HEADS, B_HEAD_DIM), q_norm_b[j])
                if kv_cache is None:
                    y_mix = band_attention_prompt(q, k_sh, v_sh, rel_bias[j])
                else:
                    y_mix = band_attention_sample(q, kv_cache[0], kv_cache[1], k_sh, v_sh, rel_bias[j])
                q_mem = u[..., B_Q_WIDTH:]
            y_mem = memory_attention(q_mem, mem_k[layer], mem_v[layer], q_norm_mem[layer])
            h = h + jnp.concatenate([y_mix, y_mem], axis=-1) @ w_out[layer]
            h = h + swiglu(rms_norm(h, g_ffn[layer]), w_ffn_gate[layer], w_ffn_up[layer], w_ffn_down[layer])
        if kv_cache is None:
            keep = min(BAND_PAST, L)
            k_rows, v_rows = k_sh[:, L - keep:], v_sh[:, L - keep:]
        else:
            k_rows, v_rows = k_sh, v_sh
        return h, jnp.stack(conv_new), jnp.stack(ssm_new), k_rows, v_rows

    mem_kv_p = [memory_kv(mem_prompt, g_mem[l], w_mem_kv[l], k_norm_mem[l]) for l in range(DEPTH)]
    mem_k_prompt = jnp.stack([kv[0] for kv in mem_kv_p])
    mem_v_prompt = jnp.stack([kv[1] for kv in mem_kv_p])
    bp = x_prompt.shape[0]
    conv0 = jnp.zeros((N_A_LAYERS, bp, CONV_WIDTH - 1, CONV_DIM), x_prompt.dtype)
    ssm0 = jnp.zeros((N_A_LAYERS, bp, SSM_HEADS, SSM_HEAD_DIM, SSM_STATE), x_prompt.dtype)
    y_prompt, conv_prompt, ssm_prompt, kv_k_prompt, kv_v_prompt = run_trunk(
        x_prompt, mem_k_prompt, mem_v_prompt, conv0, ssm0, None)

    y_sample, conv_sample, ssm_sample, kv_k_sample, kv_v_sample = run_trunk(
        x_sample, cache_mem_k, cache_mem_v, state_conv, state_ssm, (cache_kv_k, cache_kv_v))

    return (y_prompt, y_sample, ssm_prompt, conv_prompt, kv_k_prompt, kv_v_prompt,
            mem_k_prompt, mem_v_prompt, ssm_sample, conv_sample, kv_k_sample, kv_v_sample)
```

```python
import functools
import math

import jax
import jax.numpy as jnp
from jax import lax
from jax.experimental import pallas as pl
from jax.experimental.pallas import tpu as pltpu

F32 = jnp.float32
BF16 = jnp.bfloat16

D_MODEL = 4096
SEQ = 8192
DEC_BATCH = 8
DEC_SEQ = 32
PAST_LEN = 2048
CHUNK = 64
RMS_EPS = 1e-6

SSM_HEAD_DIM = 64
SSM_INNER = 6144
SSM_HEADS = 96
SSM_GROUPS = 8
SSM_HPG = 12
SSM_STATE = 128
CONV_WIDTH = 4
CONV_DIM = 8192

B_HEAD_DIM = 128
B_Q_WIDTH = 6144
B_HEADS = 48
B_KV_HEADS = 8
B_GQ = 6
BAND_PAST = 512
REL_CLIP = 64

MEM_TOKENS = 256
MEM_WIDTH = 2048
MEM_HEADS = 4
MEM_HEAD_DIM = 512

D_FF = 11008
D_FF_PAD = 11264
DT_PAD = 128

N_SAMPLE_ROWS = DEC_BATCH * DEC_SEQ
N_ROWS = SEQ + N_SAMPLE_ROWS
ROW_TILE = 768

V7X_VMEM_LIMIT_BYTES = 56 * 1024 * 1024


def _params(*sem):
    return pltpu.CompilerParams(dimension_semantics=sem, vmem_limit_bytes=V7X_VMEM_LIMIT_BYTES)


def _sigmoid(x):
    return 1.0 / (1.0 + jnp.exp(-x))


def _softplus(x):
    return jnp.maximum(x, 0.0) + jnp.log1p(jnp.exp(-jnp.abs(x)))


def _rmsnorm_body(x_ref, g_ref, *o_refs):
    x = x_ref[...]
    inv = lax.rsqrt(jnp.mean(x * x, axis=-1, keepdims=True) + RMS_EPS)
    xn = x * inv
    for j, o_ref in enumerate(o_refs):
        o_ref[...] = (xn * g_ref[j:j + 1, :]).astype(o_ref.dtype)


def rmsnorm(x, gains, tm=256):
    m, d = x.shape
    n = gains.shape[0]
    return pl.pallas_call(
        _rmsnorm_body,
        grid=(m // tm,),
        in_specs=[pl.BlockSpec((tm, d), lambda i: (i, 0)),
                  pl.BlockSpec((n, d), lambda i: (0, 0))],
        out_specs=[pl.BlockSpec((tm, d), lambda i: (i, 0))] * n,
        out_shape=[jax.ShapeDtypeStruct((m, d), BF16)] * n,
        compiler_params=_params("parallel"),
        name="rmsnorm",
    )(x, gains)


def _mm_body(x_ref, w_ref, o_ref):
    o_ref[...] = jnp.dot(x_ref[...], w_ref[...], preferred_element_type=F32).astype(o_ref.dtype)


def matmul(x, w, out_dtype, tm, tn):
    m, k = x.shape
    n = w.shape[1]
    return pl.pallas_call(
        _mm_body,
        grid=(m // tm, n // tn),
        in_specs=[pl.BlockSpec((tm, k), lambda i, j: (i, 0)),
                  pl.BlockSpec((k, tn), lambda i, j: (0, j))],
        out_specs=pl.BlockSpec((tm, tn), lambda i, j: (i, j)),
        out_shape=jax.ShapeDtypeStruct((m, n), out_dtype),
        compiler_params=_params("parallel", "arbitrary"),
        name="matmul",
    )(x, w)


def _mm_out_body(xa_ref, xb_ref, w_ref, r_ref, o_ref, acc_ref, *, n_a):
    k = pl.program_id(2)
    nk = pl.num_programs(2)

    @pl.when(k == 0)
    def _():
        acc_ref[...] = r_ref[...]

    @pl.when(k < n_a)
    def _():
        acc_ref[...] += jnp.dot(xa_ref[...], w_ref[...], preferred_element_type=F32)

    @pl.when(k >= n_a)
    def _():
        acc_ref[...] += jnp.dot(xb_ref[...], w_ref[...], preferred_element_type=F32)

    @pl.when(k == nk - 1)
    def _():
        o_ref[...] = acc_ref[...]


def matmul_out(xa, xb, w, res, tm, tn, tk):
    m, ka = xa.shape
    kb = xb.shape[1]
    n = w.shape[1]
    n_a, n_b = ka // tk, kb // tk
    return pl.pallas_call(
        functools.partial(_mm_out_body, n_a=n_a),
        grid=(m // tm, n // tn, n_a + n_b),
        in_specs=[pl.BlockSpec((tm, tk), lambda i, j, k: (i, jnp.minimum(k, n_a - 1))),
                  pl.BlockSpec((tm, tk), lambda i, j, k: (i, jnp.maximum(k - n_a, 0))),
                  pl.BlockSpec((tk, tn), lambda i, j, k: (k, j)),
                  pl.BlockSpec((tm, tn), lambda i, j, k: (i, j))],
        out_specs=pl.BlockSpec((tm, tn), lambda i, j, k: (i, j)),
        out_shape=jax.ShapeDtypeStruct((m, n), F32),
        scratch_shapes=[pltpu.VMEM((tm, tn), F32)],
        compiler_params=_params("parallel", "arbitrary", "arbitrary"),
        name="matmul_out",
    )(xa, xb, w, res)


def _ffn_up_body(x_ref, wg_ref, wu_ref, o_ref):
    x = x_ref[...]
    g = jnp.dot(x, wg_ref[...], preferred_element_type=F32)
    u = jnp.dot(x, wu_ref[...], preferred_element_type=F32)
    o_ref[...] = (g * _sigmoid(g) * u).astype(o_ref.dtype)


def ffn_up(x, wg, wu, tm, tn):
    m, k = x.shape
    n = wg.shape[1]
    return pl.pallas_call(
        _ffn_up_body,
        grid=(m // tm, n // tn),
        in_specs=[pl.BlockSpec((tm, k), lambda i, j: (i, 0)),
                  pl.BlockSpec((k, tn), lambda i, j: (0, j)),
                  pl.BlockSpec((k, tn), lambda i, j: (0, j))],
        out_specs=pl.BlockSpec((tm, tn), lambda i, j: (i, j)),
        out_shape=jax.ShapeDtypeStruct((m, n), BF16),
        compiler_params=_params("parallel", "arbitrary"),
        name="ffn_up",
    )(x, wg, wu)


def _ffn_down_body(x_ref, w_ref, r_ref, o_ref, acc_ref):
    k = pl.program_id(2)

    @pl.when(k == 0)
    def _():
        acc_ref[...] = r_ref[...]

    acc_ref[...] += jnp.dot(x_ref[...], w_ref[...], preferred_element_type=F32)

    @pl.when(k == pl.num_programs(2) - 1)
    def _():
        o_ref[...] = acc_ref[...]


def ffn_down(x, w, res, tm, tn, tk):
    m, kk = x.shape
    n = w.shape[1]
    return pl.pallas_call(
        _ffn_down_body,
        grid=(m // tm, n // tn, kk // tk),
        in_specs=[pl.BlockSpec((tm, tk), lambda i, j, k: (i, k)),
                  pl.BlockSpec((tk, tn), lambda i, j, k: (k, j)),
                  pl.BlockSpec((tm, tn), lambda i, j, k: (i, j))],
        out_specs=pl.BlockSpec((tm, tn), lambda i, j, k: (i, j)),
        out_shape=jax.ShapeDtypeStruct((m, n), F32),
        scratch_shapes=[pltpu.VMEM((tm, tn), F32)],
        compiler_params=_params("parallel", "arbitrary", "arbitrary"),
        name="ffn_down",
    )(x, w, res)


def _kv_post_body(kv_ref, g_ref, k32_ref, k16_ref, v16_ref, *, width, head_dim):
    for h in range(width // head_dim):
        sl = slice(h * head_dim, (h + 1) * head_dim)
        kh = kv_ref[:, sl]
        inv = lax.rsqrt(jnp.mean(kh * kh, axis=-1, keepdims=True) + RMS_EPS)
        kn = kh * inv * g_ref[...]
        k32_ref[:, sl] = kn
        k16_ref[:, sl] = kn.astype(BF16)
    v16_ref[...] = kv_ref[:, width:].astype(BF16)


def kv_post(kv, gain, head_dim, tm=256):
    m, w2 = kv.shape
    width = w2 // 2
    return pl.pallas_call(
        functools.partial(_kv_post_body, width=width, head_dim=head_dim),
        grid=(m // tm,),
        in_specs=[pl.BlockSpec((tm, w2), lambda i: (i, 0)),
                  pl.BlockSpec((1, head_dim), lambda i: (0, 0))],
        out_specs=[pl.BlockSpec((tm, width), lambda i: (i, 0))] * 3,
        out_shape=[jax.ShapeDtypeStruct((m, width), F32),
                   jax.ShapeDtypeStruct((m, width), BF16),
                   jax.ShapeDtypeStruct((m, width), BF16)],
        compiler_params=_params("parallel"),
        name="kv_post",
    )(kv, gain.reshape(1, head_dim))


def _ssd_body(z_ref, xbc_ref, dt_ref, dtT_ref, conv0_ref, st0_ref, cw_ref, cb_ref,
              dtb_ref, dtbT_ref, alog_ref, alogT_ref, dskip_ref, gssm_ref,
              y_ref, convn_ref, stn_ref,
              xbuf, xc, st, yacc, *, T):
    c = pl.program_id(1)
    nc = pl.num_programs(1)
    P, N, HPG = SSM_HEAD_DIM, SSM_STATE, SSM_HPG
    tail = CONV_WIDTH - 1

    @pl.when(c == 0)
    def _():
        xbuf[0:8, :] = conv0_ref[0]
        st[...] = st0_ref[0]

    xbuf[8:8 + T, :] = xbc_ref[...].astype(F32)
    acc = cb_ref[...] + cw_ref[0:1, :] * xbuf[8 - tail:8 - tail + T, :]
    for tap in range(1, CONV_WIDTH):
        acc = acc + cw_ref[tap:tap + 1, :] * xbuf[8 - tail + tap:8 - tail + tap + T, :]
    xc[...] = acc * _sigmoid(acc)
    last_rows = xbuf[T:T + 8, :]
    convn_ref[0] = last_rows
    xbuf[0:8, :] = last_rows

    ti = lax.broadcasted_iota(jnp.int32, (T, T), 0)
    si = lax.broadcasted_iota(jnp.int32, (T, T), 1)
    tri = si <= ti
    dt = _softplus(dt_ref[:, 0:SSM_HEADS] + dtb_ref[...])
    dta = dt * (-jnp.exp(alog_ref[...]))
    cum = jnp.dot(tri.astype(F32), dta, precision=lax.Precision.HIGHEST,
                  preferred_element_type=F32)
    dtT = _softplus(dtT_ref[0] + dtbT_ref[...])
    dtaT = dtT * (-jnp.exp(alogT_ref[...]))
    cumT = jnp.dot(dtaT, (ti <= si).astype(F32), precision=lax.Precision.HIGHEST,
                   preferred_element_type=F32)
    lastT = cumT[:, T - 1:T]
    wT = dtT * jnp.exp(lastT - cumT)
    edec = jnp.exp(jnp.broadcast_to(lastT, (SSM_HEADS, P)))

    for g in range(SSM_GROUPS):
        bg = xc[:, SSM_INNER + g * N:SSM_INNER + (g + 1) * N]
        cg = xc[:, SSM_INNER + SSM_GROUPS * N + g * N:SSM_INNER + SSM_GROUPS * N + (g + 1) * N]
        cbg = lax.dot_general(cg.astype(BF16), bg.astype(BF16), (((1,), (1,)), ((), ())),
                              preferred_element_type=F32)
        bgT = bg.T
        for k in range(HPG):
            h = g * HPG + k
            hs = slice(h * P, (h + 1) * P)
            colb = jnp.broadcast_to(cum[:, h:h + 1], (T, N))
            dec = jnp.where(tri, jnp.exp(colb[:, 0:T] - cumT[h:h + 1, :]), 0.0)
            mh = (cbg * dec * dtT[h:h + 1, :]).astype(BF16)
            cs = (cg * jnp.exp(colb)).astype(BF16)
            xh = xc[:, hs].astype(BF16)
            sth = st[:, hs]
            yacc[:, hs] = (jnp.dot(mh, xh, preferred_element_type=F32)
                           + jnp.dot(cs, sth.astype(BF16), preferred_element_type=F32))
            btw = (bgT * wT[h:h + 1, :]).astype(BF16)
            st[:, hs] = sth * edec[h:h + 1, :] + jnp.dot(btw, xh, preferred_element_type=F32)

    y = yacc[...] + xc[:, 0:SSM_INNER] * dskip_ref[...]
    z = z_ref[...].astype(F32)
    yg = y * (z * _sigmoid(z))
    inv = lax.rsqrt(jnp.mean(yg * yg, axis=-1, keepdims=True) + RMS_EPS)
    y_ref[...] = (yg * inv * gssm_ref[...]).astype(y_ref.dtype)

    @pl.when(c == nc - 1)
    def _():
        stn_ref[0] = st[...]


def ssd_mixer(u, dt_raw, conv0, st0, conv_w, conv_b, dt_bias, a_log, d_skip, g_ssm,
              *, row0, batch, nc, T):
    rows = batch * nc * T
    r0 = row0 // T
    dt_seg = dt_raw[row0:row0 + rows]
    dtT = dt_seg[:, :SSM_HEADS].reshape(batch * nc, T, SSM_HEADS).transpose(0, 2, 1)
    row = lambda v: v.reshape(1, -1).astype(F32)
    col = lambda v: v.reshape(-1, 1).astype(F32)
    const = lambda shape: pl.BlockSpec(shape, lambda b, c: (0,) * len(shape))
    xbc_blk = SSM_INNER * 0 + 1
    return pl.pallas_call(
        functools.partial(_ssd_body, T=T),
        grid=(batch, nc),
        in_specs=[
            pl.BlockSpec((T, SSM_INNER), lambda b, c: (r0 + b * nc + c, 0)),
            pl.BlockSpec((T, CONV_DIM), lambda b, c: (r0 + b * nc + c, xbc_blk)),
            pl.BlockSpec((T, DT_PAD), lambda b, c: (r0 + b * nc + c, 0)),
            pl.BlockSpec((1, SSM_HEADS, T), lambda b, c: (b * nc + c, 0, 0)),
            pl.BlockSpec((1, 8, CONV_DIM), lambda b, c: (b, 0, 0)),
            pl.BlockSpec((1, SSM_STATE, SSM_INNER), lambda b, c: (b, 0, 0)),
            const((CONV_WIDTH, CONV_DIM)), const((1, CONV_DIM)),
            const((1, SSM_HEADS)), const((SSM_HEADS, 1)),
            const((1, SSM_HEADS)), const((SSM_HEADS, 1)),
            const((1, SSM_INNER)), const((1, SSM_INNER)),
        ],
        out_specs=[
            pl.BlockSpec((T, SSM_INNER), lambda b, c: (b * nc + c, 0)),
            pl.BlockSpec((1, 8, CONV_DIM), lambda b, c: (b, 0, 0)),
            pl.BlockSpec((1, SSM_STATE, SSM_INNER), lambda b, c: (b, 0, 0)),
        ],
        out_shape=[
            jax.ShapeDtypeStruct((rows, SSM_INNER), BF16),
            jax.ShapeDtypeStruct((batch, 8, CONV_DIM), F32),
            jax.ShapeDtypeStruct((batch, SSM_STATE, SSM_INNER), F32),
        ],
        scratch_shapes=[
            pltpu.VMEM((8 + T, CONV_DIM), F32),
            pltpu.VMEM((T, CONV_DIM), F32),
            pltpu.VMEM((SSM_STATE, SSM_INNER), F32),
            pltpu.VMEM((T, SSM_INNER), F32),
        ],
        compiler_params=_params("parallel", "arbitrary"),
        name="ssd_mixer",
    )(u, u, dt_raw, dtT, conv0, st0, conv_w.T.astype(F32), row(conv_b),
      row(dt_bias), col(dt_bias), row(a_log), col(a_log),
      row(jnp.repeat(d_skip, SSM_HEAD_DIM)), row(g_ssm))


def _band_body(q_ref, k_ref, v_ref, bias_ref, qg_ref, o_ref, *, T, KB, pad):
    c = pl.program_id(2)
    start = pl.multiple_of(c * T, T)
    k = k_ref[0, 0, pl.ds(start, KB), :]
    v = v_ref[0, 0, pl.ds(start, KB), :]
    scale = 1.0 / math.sqrt(B_HEAD_DIM)
    qs = []
    for j in range(B_GQ):
        qj = q_ref[:, j * B_HEAD_DIM:(j + 1) * B_HEAD_DIM].astype(F32)
        inv = lax.rsqrt(jnp.mean(qj * qj, axis=-1, keepdims=True) + RMS_EPS)
        qs.append((qj * inv * qg_ref[...] * scale).astype(BF16))
    q = jnp.concatenate(qs, axis=0)
    s = lax.dot_general(q, k, (((1,), (1,)), ((), ())), preferred_element_type=F32)
    s = s + bias_ref[0]
    if pad:
        kidx = lax.broadcasted_iota(jnp.int32, (1, KB), 1)
        s = jnp.where(start + kidx >= pad, s, -jnp.inf)
    m = jnp.max(s, axis=-1, keepdims=True)
    p = jnp.exp(s - m)
    l = jnp.sum(p, axis=-1, keepdims=True)
    o = jnp.dot(p.astype(BF16), v, preferred_element_type=F32) / l
    for j in range(B_GQ):
        o_ref[:, j * B_HEAD_DIM:(j + 1) * B_HEAD_DIM] = o[j * T:(j + 1) * T].astype(o_ref.dtype)


def band_attention(u, k, v, bias, q_gain, *, row0, batch, nc, T, KB, pad):
    rows = batch * nc * T
    r0 = row0 // T
    ktot = k.shape[2]
    gw = B_GQ * B_HEAD_DIM
    return pl.pallas_call(
        functools.partial(_band_body, T=T, KB=KB, pad=pad),
        grid=(batch, B_KV_HEADS, nc),
        in_specs=[
            pl.BlockSpec((T, gw), lambda b, g, c: (r0 + b * nc + c, g)),
            pl.BlockSpec((1, 1, ktot, B_HEAD_DIM), lambda b, g, c: (b, g, 0, 0)),
            pl.BlockSpec((1, 1, ktot, B_HEAD_DIM), lambda b, g, c: (b, g, 0, 0)),
            pl.BlockSpec((1, B_GQ * T, KB), lambda b, g, c: (g, 0, 0)),
            pl.BlockSpec((1, B_HEAD_DIM), lambda b, g, c: (0, 0)),
        ],
        out_specs=pl.BlockSpec((T, gw), lambda b, g, c: (b * nc + c, g)),
        out_shape=jax.ShapeDtypeStruct((rows, B_Q_WIDTH), BF16),
        compiler_params=_params("parallel", "parallel", "arbitrary"),
        name="band_attention",
    )(u, k, v, bias, q_gain.reshape(1, B_HEAD_DIM).astype(F32))


def _mem_body(q_ref, k_ref, v_ref, qg_ref, o_ref):
    scale = 1.0 / math.sqrt(MEM_HEAD_DIM)
    for h in range(MEM_HEADS):
        sl = slice(h * MEM_HEAD_DIM, (h + 1) * MEM_HEAD_DIM)
        qh = q_ref[:, sl].astype(F32)
        inv = lax.rsqrt(jnp.mean(qh * qh, axis=-1, keepdims=True) + RMS_EPS)
        qn = (qh * inv * qg_ref[...] * scale).astype(BF16)
        s = lax.dot_general(qn, k_ref[0, :, sl], (((1,), (1,)), ((), ())),
                            preferred_element_type=F32)
        m = jnp.max(s, axis=-1, keepdims=True)
        p = jnp.exp(s - m)
        l = jnp.sum(p, axis=-1, keepdims=True)
        o = jnp.dot(p.astype(BF16), v_ref[0, :, sl], preferred_element_type=F32) / l
        o_ref[:, sl] = o.astype(o_ref.dtype)


def memory_attention(u, k, v, q_gain, *, row0, batch, nt, tq):
    rows = batch * nt * tq
    r0 = row0 // tq
    return pl.pallas_call(
        _mem_body,
        grid=(batch, nt),
        in_specs=[
            pl.BlockSpec((tq, MEM_WIDTH), lambda b, i: (r0 + b * nt + i, 3)),
            pl.BlockSpec((1, MEM_TOKENS, MEM_WIDTH), lambda b, i: (b, 0, 0)),
            pl.BlockSpec((1, MEM_TOKENS, MEM_WIDTH), lambda b, i: (b, 0, 0)),
            pl.BlockSpec((1, MEM_HEAD_DIM), lambda b, i: (0, 0)),
        ],
        out_specs=pl.BlockSpec((tq, MEM_WIDTH), lambda b, i: (b * nt + i, 0)),
        out_shape=jax.ShapeDtypeStruct((rows, MEM_WIDTH), BF16),
        compiler_params=_params("parallel", "arbitrary"),
        name="memory_attention",
    )(u, k, v, q_gain.reshape(1, MEM_HEAD_DIM).astype(F32))


def _rel_bias(table, q_pos, k_pos):
    rel = jnp.clip(q_pos[:, None] - k_pos[None, :], -REL_CLIP, REL_CLIP) + REL_CLIP
    bias = table[:, rel].astype(F32)
    return bias.reshape(B_KV_HEADS, B_GQ * q_pos.shape[0], k_pos.shape[0])


def _heads_major(x, n_heads):
    b, l, w = x.shape
    return x.reshape(b, l, n_heads, w // n_heads).transpose(0, 2, 1, 3)


def _ffn(h, g_ffn, wg, wu, wd):
    (hn,) = rmsnorm(h, g_ffn.reshape(1, -1))
    act = ffn_up(hn, wg, wu, ROW_TILE, 512)
    return ffn_down(act, wd, h, ROW_TILE, 1024, 2816)


def kernel(x_prompt, x_sample, mem_prompt, state_ssm, state_conv, cache_kv_k, cache_kv_v, cache_mem_k, cache_mem_v, g_mix, w_in_a, conv_w, conv_b, dt_bias, a_log, d_skip, g_ssm, w_in_b, rel_bias, q_norm_b, g_kv, w_kv, k_norm_kv, g_mem, w_mem_kv, q_norm_mem, k_norm_mem, w_out, g_ffn, w_ffn_gate, w_ffn_up, w_ffn_down):
    S, NS = SEQ, N_SAMPLE_ROWS
    ff_pad = D_FF_PAD - D_FF

    wa = w_in_a[0]
    z_end, xbc_end, dt_end = SSM_INNER, SSM_INNER + CONV_DIM, SSM_INNER + CONV_DIM + SSM_HEADS
    wa_main = jnp.concatenate([wa[:, :z_end], wa[:, dt_end:], wa[:, z_end:xbc_end]], axis=1).astype(BF16)
    wa_dt = jnp.pad(wa[:, xbc_end:dt_end], ((0, 0), (0, DT_PAD - SSM_HEADS))).astype(BF16)
    wb = w_in_b[0].astype(BF16)
    wkv = w_kv.astype(BF16)
    wmem = w_mem_kv.astype(BF16)
    wout = w_out.astype(BF16)
    wg = jnp.pad(w_ffn_gate, ((0, 0), (0, 0), (0, ff_pad))).astype(BF16)
    wu = jnp.pad(w_ffn_up, ((0, 0), (0, 0), (0, ff_pad))).astype(BF16)
    wd = jnp.pad(w_ffn_down, ((0, 0), (0, ff_pad), (0, 0))).astype(BF16)

    x = jnp.concatenate([x_prompt.reshape(S, D_MODEL), x_sample.reshape(NS, D_MODEL)], axis=0)

    mem_k32, mem_v32, mem_k16, mem_v16 = [], [], [], []
    mem_x = mem_prompt.reshape(MEM_TOKENS, D_MODEL)
    for l in range(2):
        (mn,) = rmsnorm(mem_x, g_mem[l].reshape(1, -1))
        kv = matmul(mn, wmem[l], F32, 256, 1024)
        k32, k16, v16 = kv_post(kv, k_norm_mem[l], MEM_HEAD_DIM)
        mem_k32.append(k32.reshape(1, MEM_TOKENS, MEM_HEADS, MEM_HEAD_DIM))
        mem_v32.append(kv[:, MEM_WIDTH:].reshape(1, MEM_TOKENS, MEM_HEADS, MEM_HEAD_DIM))
        mem_k16.append(k16.reshape(1, MEM_TOKENS, MEM_WIDTH))
        mem_v16.append(v16.reshape(1, MEM_TOKENS, MEM_WIDTH))
    cmk = cache_mem_k.reshape(2, DEC_BATCH, MEM_TOKENS, MEM_WIDTH).astype(BF16)
    cmv = cache_mem_v.reshape(2, DEC_BATCH, MEM_TOKENS, MEM_WIDTH).astype(BF16)

    def mem_attn(u, l):
        yp = memory_attention(u, mem_k16[l], mem_v16[l], q_norm_mem[l], row0=0, batch=1, nt=S // 512, tq=512)
        ys = memory_attention(u, cmk[l], cmv[l], q_norm_mem[l], row0=S, batch=DEC_BATCH, nt=1, tq=DEC_SEQ)
        return jnp.concatenate([yp, ys], axis=0)

    (hn,) = rmsnorm(x, g_mix[0].reshape(1, -1))
    u = matmul(hn, wa_main, BF16, ROW_TILE, 1024)
    dt_raw = matmul(hn, wa_dt, F32, ROW_TILE, DT_PAD)

    def to_state_t(s):
        b = s.shape[0]
        return s.reshape(b, SSM_INNER, SSM_STATE).transpose(0, 2, 1)

    def from_state_t(s):
        b = s.shape[0]
        return s.transpose(0, 2, 1).reshape(b, SSM_HEADS, SSM_HEAD_DIM, SSM_STATE)

    ssd = functools.partial(ssd_mixer, u, dt_raw, conv_w=conv_w[0], conv_b=conv_b[0], dt_bias=dt_bias[0],
                            a_log=a_log[0], d_skip=d_skip[0], g_ssm=g_ssm[0])
    tail = CONV_WIDTH - 1
    conv0_p = jnp.zeros((1, 8, CONV_DIM), F32)
    st0_p = jnp.zeros((1, SSM_STATE, SSM_INNER), F32)
    y_p, conv_p, st_p = ssd(conv0=conv0_p, st0=st0_p, row0=0, batch=1, nc=S // CHUNK, T=CHUNK)
    conv0_s = jnp.pad(state_conv[0], ((0, 0), (8 - tail, 0), (0, 0)))
    y_s, conv_s, st_s = ssd(conv0=conv0_s, st0=to_state_t(state_ssm[0]), row0=S, batch=DEC_BATCH, nc=1, T=DEC_SEQ)
    y_mix = jnp.concatenate([y_p, y_s], axis=0)
    h = matmul_out(y_mix, mem_attn(u, 0), wout[0], x, ROW_TILE, 1024, 2048)
    h = _ffn(h, g_ffn[0], wg[0], wu[0], wd[0])

    hkv, hn = rmsnorm(h, jnp.stack([g_kv, g_mix[1]]))
    kv = matmul(hkv, wkv, F32, ROW_TILE, 1024)
    kvw = B_KV_HEADS * B_HEAD_DIM
    k32, k16, v16 = kv_post(kv, k_norm_kv, B_HEAD_DIM)
    u = matmul(hn, wb, BF16, ROW_TILE, 1024)

    kp = jnp.pad(_heads_major(k16[:S].reshape(1, S, kvw), B_KV_HEADS), ((0, 0), (0, 0), (BAND_PAST, 0), (0, 0)))
    vp = jnp.pad(_heads_major(v16[:S].reshape(1, S, kvw), B_KV_HEADS), ((0, 0), (0, 0), (BAND_PAST, 0), (0, 0)))
    band_len = BAND_PAST + CHUNK
    bias_p = _rel_bias(rel_bias[0], BAND_PAST + jnp.arange(CHUNK), jnp.arange(band_len))
    yb_p = band_attention(u, kp, vp, bias_p, q_norm_b[0], row0=0, batch=1, nc=S // CHUNK, T=CHUNK,
                          KB=band_len, pad=BAND_PAST)

    wc = cache_kv_k.shape[1]
    ks = jnp.concatenate([cache_kv_k.reshape(DEC_BATCH, wc, kvw).astype(BF16),
                          k16[S:].reshape(DEC_BATCH, DEC_SEQ, kvw)], axis=1)
    vs = jnp.concatenate([cache_kv_v.reshape(DEC_BATCH, wc, kvw).astype(BF16),
                          v16[S:].reshape(DEC_BATCH, DEC_SEQ, kvw)], axis=1)
    k_pos = jnp.concatenate([PAST_LEN - wc + jnp.arange(wc), PAST_LEN + jnp.arange(DEC_SEQ)])
    bias_s = _rel_bias(rel_bias[0], PAST_LEN + jnp.arange(DEC_SEQ), k_pos)
    yb_s = band_attention(u, _heads_major(ks, B_KV_HEADS), _heads_major(vs, B_KV_HEADS), bias_s, q_norm_b[0],
                          row0=S, batch=DEC_BATCH, nc=1, T=DEC_SEQ, KB=wc + DEC_SEQ, pad=0)
    y_mix = jnp.concatenate([yb_p, yb_s], axis=0)
    h = matmul_out(y_mix, mem_attn(u, 1), wout[1], h, ROW_TILE, 1024, 2048)
    h = _ffn(h, g_ffn[1], wg[1], wu[1], wd[1])

    keep = min(BAND_PAST, S)
    kv_shape = (B_KV_HEADS, B_HEAD_DIM)
    return (
        h[:S].reshape(1, S, D_MODEL),
        h[S:].reshape(DEC_BATCH, DEC_SEQ, D_MODEL),
        from_state_t(st_p)[None],
        conv_p[:, 8 - tail:][None],
        k32[S - keep:S].reshape(1, keep, *kv_shape),
        kv[S - keep:S, kvw:].reshape(1, keep, *kv_shape),
        jnp.stack(mem_k32),
        jnp.stack(mem_v32),
        from_state_t(st_s)[None],
        conv_s[:, 8 - tail:][None],
        k32[S:].reshape(DEC_BATCH, DEC_SEQ, *kv_shape),
        kv[S:, kvw:].reshape(DEC_BATCH, DEC_SEQ, *kv_shape),
    )
```

```python
import functools
import math

import jax
import jax.numpy as jnp
from jax import lax
from jax.experimental import pallas as pl
from jax.experimental.pallas import tpu as pltpu

F32 = jnp.float32
BF16 = jnp.bfloat16

D_MODEL = 4096
SEQ = 8192
DEC_BATCH = 8
DEC_SEQ = 32
PAST_LEN = 2048
CHUNK = 64
RMS_EPS = 1e-6

SSM_HEAD_DIM = 64
SSM_INNER = 6144
SSM_HEADS = 96
SSM_PAIRS = SSM_HEADS // 2
SSM_GROUPS = 8
SSM_PPG = SSM_PAIRS // SSM_GROUPS
SSM_STATE = 128
CONV_WIDTH = 4
CONV_DIM = 8192

B_HEAD_DIM = 128
B_Q_WIDTH = 6144
B_HEADS = 48
B_KV_HEADS = 8
B_GQ = 6
BAND_PAST = 512
REL_CLIP = 64
BAND_CONST_KEYS = 384
BAND_CHUNKS_PER_STEP = 4

MEM_TOKENS = 256
MEM_WIDTH = 2048
MEM_HEADS = 4
MEM_HEAD_DIM = 512

D_FF = 11008
D_FF_PAD = 11264
DT_PAD = 128

N_SAMPLE_ROWS = DEC_BATCH * DEC_SEQ
N_ROWS = SEQ + N_SAMPLE_ROWS
ROW_TILE = 768
SUB_ROWS = 256
SUBS_PER_TILE = ROW_TILE // SUB_ROWS

V7X_VMEM_LIMIT_BYTES = 56 * 1024 * 1024


def _params(*sem):
    return pltpu.CompilerParams(dimension_semantics=sem, vmem_limit_bytes=V7X_VMEM_LIMIT_BYTES)


def _sigmoid(x):
    return 1.0 / (1.0 + jnp.exp(-x))


def _softplus(x):
    return jnp.maximum(x, 0.0) + jnp.log1p(jnp.exp(-jnp.abs(x)))


def _dot(a, b):
    return jnp.dot(a, b, preferred_element_type=F32)


def _dot_nt(a, b):
    return lax.dot_general(a, b, (((1,), (1,)), ((), ())), preferred_element_type=F32)


def _dot_exact(a, b):
    return jnp.dot(a, b, precision=lax.Precision.HIGHEST, preferred_element_type=F32)


def _rms_scale(x):
    return x * lax.rsqrt(jnp.mean(x * x, axis=-1, keepdims=True) + RMS_EPS)


def _rmsnorm_body(x_ref, g_ref, *o_refs):
    xn = _rms_scale(x_ref[...])
    for j, o_ref in enumerate(o_refs):
        o_ref[...] = (xn * g_ref[j:j + 1, :]).astype(o_ref.dtype)


def rmsnorm(x, gains, tm=256):
    m, d = x.shape
    n = gains.shape[0]
    return pl.pallas_call(
        _rmsnorm_body,
        grid=(m // tm,),
        in_specs=[pl.BlockSpec((tm, d), lambda i: (i, 0)),
                  pl.BlockSpec((n, d), lambda i: (0, 0))],
        out_specs=[pl.BlockSpec((tm, d), lambda i: (i, 0))] * n,
        out_shape=[jax.ShapeDtypeStruct((m, d), BF16)] * n,
        compiler_params=_params("parallel"),
        name="rmsnorm",
    )(x, gains)


def _rmsnorm2_body(xp_ref, xs_ref, g_ref, o_ref, *, n_p):
    i = pl.program_id(0)

    @pl.when(i < n_p)
    def _():
        o_ref[...] = (_rms_scale(xp_ref[...]) * g_ref[...]).astype(o_ref.dtype)

    @pl.when(i >= n_p)
    def _():
        o_ref[...] = (_rms_scale(xs_ref[...]) * g_ref[...]).astype(o_ref.dtype)


def rmsnorm_rows2(xp, xs, gain):
    tm = SUB_ROWS
    d = xp.shape[1]
    n_p, n_s = xp.shape[0] // tm, xs.shape[0] // tm
    return pl.pallas_call(
        functools.partial(_rmsnorm2_body, n_p=n_p),
        grid=(n_p + n_s,),
        in_specs=[pl.BlockSpec((tm, d), lambda i: (jnp.minimum(i, n_p - 1), 0)),
                  pl.BlockSpec((tm, d), lambda i: (jnp.maximum(i - n_p, 0), 0)),
                  pl.BlockSpec((1, d), lambda i: (0, 0))],
        out_specs=pl.BlockSpec((tm, d), lambda i: (i, 0)),
        out_shape=jax.ShapeDtypeStruct((xp.shape[0] + xs.shape[0], d), BF16),
        compiler_params=_params("parallel"),
        name="rmsnorm_rows2",
    )(xp, xs, gain.reshape(1, d))


def _mm_body(x_ref, w_ref, o_ref):
    o_ref[...] = _dot(x_ref[...], w_ref[...]).astype(o_ref.dtype)


def matmul(x, w, out_dtype, tm, tn):
    m, k = x.shape
    n = w.shape[1]
    return pl.pallas_call(
        _mm_body,
        grid=(m // tm, n // tn),
        in_specs=[pl.BlockSpec((tm, k), lambda i, j: (i, 0)),
                  pl.BlockSpec((k, tn), lambda i, j: (0, j))],
        out_specs=pl.BlockSpec((tm, tn), lambda i, j: (i, j)),
        out_shape=jax.ShapeDtypeStruct((m, n), out_dtype),
        compiler_params=_params("parallel", "arbitrary"),
        name="matmul",
    )(x, w)


def _mm_out_body(xap_ref, xas_ref, xbp_ref, xbs_ref, w_ref, rp_ref, rs_ref, o_ref, acc_ref, *, n_a):
    i, k = pl.program_id(0), pl.program_id(2)
    last_tile = i == pl.num_programs(0) - 1
    head = (SUBS_PER_TILE - 1) * SUB_ROWS

    @pl.when(k == 0)
    def _():
        acc_ref[0:head, :] = rp_ref[0:SUBS_PER_TILE - 1].reshape(head, -1)

        @pl.when(jnp.logical_not(last_tile))
        def _():
            acc_ref[head:, :] = rp_ref[SUBS_PER_TILE - 1]

        @pl.when(last_tile)
        def _():
            acc_ref[head:, :] = rs_ref[0]

    def accumulate(p_ref, s_ref):
        @pl.when(jnp.logical_not(last_tile))
        def _():
            acc_ref[...] += _dot(p_ref[...].reshape(ROW_TILE, -1), w_ref[...])

        @pl.when(last_tile)
        def _():
            acc_ref[0:head, :] += _dot(p_ref[0:SUBS_PER_TILE - 1].reshape(head, -1), w_ref[...])
            acc_ref[head:, :] += _dot(s_ref[0], w_ref[...])

    @pl.when(k < n_a)
    def _():
        accumulate(xap_ref, xas_ref)

    @pl.when(k >= n_a)
    def _():
        accumulate(xbp_ref, xbs_ref)

    @pl.when(k == pl.num_programs(2) - 1)
    def _():
        o_ref[...] = acc_ref[...]


def matmul_out(xa_p, xa_s, xb_p, xb_s, w, r_p, r_s, r_s_block, tn, tk):
    sub3 = lambda a: a.reshape(a.shape[0] // SUB_ROWS, SUB_ROWS, a.shape[1])
    ka, kb, n = xa_p.shape[1], xb_p.shape[1], w.shape[1]
    n_a, n_b = ka // tk, kb // tk
    ka_idx = lambda k: jnp.minimum(k, n_a - 1)
    kb_idx = lambda k: jnp.maximum(k - n_a, 0)
    nt = N_ROWS // ROW_TILE
    spt = SUBS_PER_TILE
    return pl.pallas_call(
        functools.partial(_mm_out_body, n_a=n_a),
        grid=(nt, n // tn, n_a + n_b),
        in_specs=[pl.BlockSpec((spt, SUB_ROWS, tk), lambda i, j, k: (i, 0, ka_idx(k))),
                  pl.BlockSpec((1, SUB_ROWS, tk), lambda i, j, k: (0, 0, ka_idx(k))),
                  pl.BlockSpec((spt, SUB_ROWS, tk), lambda i, j, k: (i, 0, kb_idx(k))),
                  pl.BlockSpec((1, SUB_ROWS, tk), lambda i, j, k: (0, 0, kb_idx(k))),
                  pl.BlockSpec((tk, tn), lambda i, j, k: (k, j)),
                  pl.BlockSpec((spt, SUB_ROWS, tn), lambda i, j, k: (i, 0, j)),
                  pl.BlockSpec((1, SUB_ROWS, tn), lambda i, j, k: (r_s_block, 0, j))],
        out_specs=pl.BlockSpec((ROW_TILE, tn), lambda i, j, k: (i, j)),
        out_shape=jax.ShapeDtypeStruct((N_ROWS, n), F32),
        scratch_shapes=[pltpu.VMEM((ROW_TILE, tn), F32)],
        compiler_params=_params("parallel", "arbitrary", "arbitrary"),
        name="matmul_out",
    )(sub3(xa_p), sub3(xa_s), sub3(xb_p), sub3(xb_s), w, sub3(r_p), sub3(r_s))


def _ffn_up_body(x_ref, wg_ref, wu_ref, o_ref):
    x = x_ref[...]
    g = _dot(x, wg_ref[...])
    u = _dot(x, wu_ref[...])
    o_ref[...] = (g * _sigmoid(g) * u).astype(o_ref.dtype)


def ffn_up(x, wg, wu, tm, tn):
    m, k = x.shape
    n = wg.shape[1]
    return pl.pallas_call(
        _ffn_up_body,
        grid=(m // tm, n // tn),
        in_specs=[pl.BlockSpec((tm, k), lambda i, j: (i, 0)),
                  pl.BlockSpec((k, tn), lambda i, j: (0, j)),
                  pl.BlockSpec((k, tn), lambda i, j: (0, j))],
        out_specs=pl.BlockSpec((tm, tn), lambda i, j: (i, j)),
        out_shape=jax.ShapeDtypeStruct((m, n), BF16),
        compiler_params=_params("parallel", "arbitrary"),
        name="ffn_up",
    )(x, wg, wu)


def _ffn_down_body(x_ref, w_ref, r_ref, o_ref, acc_ref):
    k = pl.program_id(2)

    @pl.when(k == 0)
    def _():
        acc_ref[...] = r_ref[...]

    acc_ref[...] += _dot(x_ref[...], w_ref[...])

    @pl.when(k == pl.num_programs(2) - 1)
    def _():
        o_ref[...] = acc_ref[...]


def ffn_down(x, w, res, *, row0, rows, tm, tn, tk):
    kk = x.shape[1]
    n = w.shape[1]
    r0 = row0 // tm
    return pl.pallas_call(
        _ffn_down_body,
        grid=(rows // tm, n // tn, kk // tk),
        in_specs=[pl.BlockSpec((tm, tk), lambda i, j, k: (r0 + i, k)),
                  pl.BlockSpec((tk, tn), lambda i, j, k: (k, j)),
                  pl.BlockSpec((tm, tn), lambda i, j, k: (r0 + i, j))],
        out_specs=pl.BlockSpec((tm, tn), lambda i, j, k: (i, j)),
        out_shape=jax.ShapeDtypeStruct((rows, n), F32),
        scratch_shapes=[pltpu.VMEM((tm, tn), F32)],
        compiler_params=_params("parallel", "arbitrary", "arbitrary"),
        name="ffn_down",
    )(x, w, res)


def _kv_post_body(kv_ref, g_ref, k32_ref, k16_ref, v16_ref, *, width, head_dim):
    for h in range(width // head_dim):
        sl = slice(h * head_dim, (h + 1) * head_dim)
        kn = _rms_scale(kv_ref[:, sl]) * g_ref[...]
        k32_ref[:, sl] = kn
        k16_ref[:, sl] = kn.astype(BF16)
    v16_ref[...] = kv_ref[:, width:].astype(BF16)


def kv_post(kv, gain, head_dim, tm=256):
    m, w2 = kv.shape
    width = w2 // 2
    return pl.pallas_call(
        functools.partial(_kv_post_body, width=width, head_dim=head_dim),
        grid=(m // tm,),
        in_specs=[pl.BlockSpec((tm, w2), lambda i: (i, 0)),
                  pl.BlockSpec((1, head_dim), lambda i: (0, 0))],
        out_specs=[pl.BlockSpec((tm, width), lambda i: (i, 0))] * 3,
        out_shape=[jax.ShapeDtypeStruct((m, width), F32),
                   jax.ShapeDtypeStruct((m, width), BF16),
                   jax.ShapeDtypeStruct((m, width), BF16)],
        compiler_params=_params("parallel"),
        name="kv_post",
    )(kv, gain.reshape(1, head_dim))


def _ssd_body(z_ref, xbc_ref, dt_ref, dtT_ref, conv0_ref, st0_ref, cw_ref, cb_ref,
              dtb_ref, dtbT_ref, alog_ref, alogT_ref, dskip_ref, gssm_ref,
              y_ref, convn_ref, stn_ref,
              xbuf, xc, st, yacc, *, T):
    c = pl.program_id(1)
    nc = pl.num_programs(1)
    P, N = SSM_HEAD_DIM, SSM_STATE
    T2, PW = 2 * T, 2 * P
    tail = CONV_WIDTH - 1

    @pl.when(c == 0)
    def _():
        xbuf[0:8, :] = conv0_ref[0]
        st[...] = st0_ref[0]

    xbuf[8:8 + T, :] = xbc_ref[...].astype(F32)
    acc = cb_ref[...] + cw_ref[0:1, :] * xbuf[8 - tail:8 - tail + T, :]
    for tap in range(1, CONV_WIDTH):
        acc = acc + cw_ref[tap:tap + 1, :] * xbuf[8 - tail + tap:8 - tail + tap + T, :]
    xc[...] = acc * _sigmoid(acc)
    last_rows = xbuf[T:T + 8, :]
    convn_ref[0] = last_rows
    xbuf[0:8, :] = last_rows

    iota = lambda shape, d: lax.broadcasted_iota(jnp.int32, shape, d)
    tri = iota((T, T), 1) <= iota((T, T), 0)
    dt = _softplus(dt_ref[:, 0:SSM_HEADS] + dtb_ref[...])
    cum = _dot_exact(tri.astype(F32), dt * (-jnp.exp(alog_ref[...])))

    half = lambda x, size: jnp.where(x >= size, 1, 0)
    ra, rb = iota((T2, T2), 0), iota((T2, T2), 1)
    same = half(ra, T) == half(rb, T)
    dtT = _softplus(dtT_ref[0] + dtbT_ref[...])
    dtaT = dtT * (-jnp.exp(alogT_ref[...]))
    cumT = _dot_exact(dtaT, jnp.where(same & (ra <= rb), 1.0, 0.0))
    lastT = _dot_exact(dtaT, jnp.where(same, 1.0, 0.0))
    same_p = half(iota((T2, PW), 0), T) == half(iota((T2, PW), 1), P)
    edec = jnp.exp(_dot_exact(dtaT, jnp.where(same_p, 1.0, 0.0)))
    wT = dtT * jnp.exp(lastT - cumT)

    left = iota((T, PW), 1) < P
    left2 = iota((T, T2), 1) < T
    tri2 = (iota((T, T2), 1) - T * half(iota((T, T2), 1), T)) <= iota((T, T2), 0)
    bc_off = SSM_INNER
    cc_off = SSM_INNER + SSM_GROUPS * N

    for g in range(SSM_GROUPS):
        bg = xc[:, bc_off + g * N:bc_off + (g + 1) * N]
        cg16 = xc[:, cc_off + g * N:cc_off + (g + 1) * N].astype(BF16)
        bg2 = jnp.concatenate([bg, bg], axis=0)
        cb2 = _dot_nt(cg16, bg2.astype(BF16))
        bgT2 = bg2.T
        gw = SSM_PPG * PW
        yoff = _dot(cg16, st[:, g * gw:(g + 1) * gw].astype(BF16))
        for k in range(SSM_PPG):
            q = g * SSM_PPG + k
            ps = slice(q * PW, (q + 1) * PW)
            b0 = jnp.broadcast_to(cum[:, 2 * q:2 * q + 1], (T, PW))
            b1 = jnp.broadcast_to(cum[:, 2 * q + 1:2 * q + 2], (T, PW))
            col_p = jnp.where(left, b0, b1)
            col_2 = col_p if T2 == PW else jnp.where(left2, b0[:, 0:T2], b1[:, 0:T2])
            dec = jnp.where(tri2, jnp.exp(col_2 - cumT[q:q + 1, :]), 0.0)
            m2 = (cb2 * dec * dtT[q:q + 1, :]).astype(BF16)
            xp = xc[:, ps]
            bd = jnp.concatenate([jnp.where(left, xp, 0.0), jnp.where(left, 0.0, xp)],
                                 axis=0).astype(BF16)
            yacc[:, ps] = _dot(m2, bd) + yoff[:, k * PW:(k + 1) * PW] * jnp.exp(col_p)
            btw = (bgT2 * wT[q:q + 1, :]).astype(BF16)
            st[:, ps] = st[:, ps] * edec[q:q + 1, :] + _dot(btw, bd)

    y = yacc[...] + xc[:, 0:SSM_INNER] * dskip_ref[...]
    z = z_ref[...].astype(F32)
    yg = y * (z * _sigmoid(z))
    y_ref[...] = (_rms_scale(yg) * gssm_ref[...]).astype(y_ref.dtype)

    @pl.when(c == nc - 1)
    def _():
        stn_ref[0] = st[...]


def ssd_mixer(u, dt_raw, conv0, st0, conv_w, conv_b, dt_bias, a_log, d_skip, g_ssm,
              *, row0, batch, nc, T):
    rows = batch * nc * T
    r0 = row0 // T
    nblk = batch * nc
    dtT = (dt_raw[row0:row0 + rows, :SSM_HEADS].reshape(nblk, T, SSM_PAIRS, 2)
           .transpose(0, 2, 3, 1).reshape(nblk, SSM_PAIRS, 2 * T))
    pair_rows = lambda v: jnp.repeat(v.astype(F32).reshape(SSM_PAIRS, 2), T, axis=1)
    row = lambda v: v.reshape(1, -1).astype(F32)
    const = lambda shape: pl.BlockSpec(shape, lambda b, c: (0,) * len(shape))
    xbc_col_block = (SSM_INNER + MEM_WIDTH) // CONV_DIM
    return pl.pallas_call(
        functools.partial(_ssd_body, T=T),
        grid=(batch, nc),
        in_specs=[
            pl.BlockSpec((T, SSM_INNER), lambda b, c: (r0 + b * nc + c, 0)),
            pl.BlockSpec((T, CONV_DIM), lambda b, c: (r0 + b * nc + c, xbc_col_block)),
            pl.BlockSpec((T, DT_PAD), lambda b, c: (r0 + b * nc + c, 0)),
            pl.BlockSpec((1, SSM_PAIRS, 2 * T), lambda b, c: (b * nc + c, 0, 0)),
            pl.BlockSpec((1, 8, CONV_DIM), lambda b, c: (b, 0, 0)),
            pl.BlockSpec((1, SSM_STATE, SSM_INNER), lambda b, c: (b, 0, 0)),
            const((CONV_WIDTH, CONV_DIM)), const((1, CONV_DIM)),
            const((1, SSM_HEADS)), const((SSM_PAIRS, 2 * T)),
            const((1, SSM_HEADS)), const((SSM_PAIRS, 2 * T)),
            const((1, SSM_INNER)), const((1, SSM_INNER)),
        ],
        out_specs=[
            pl.BlockSpec((T, SSM_INNER), lambda b, c: (b * nc + c, 0)),
            pl.BlockSpec((1, 8, CONV_DIM), lambda b, c: (b, 0, 0)),
            pl.BlockSpec((1, SSM_STATE, SSM_INNER), lambda b, c: (b, 0, 0)),
        ],
        out_shape=[
            jax.ShapeDtypeStruct((rows, SSM_INNER), BF16),
            jax.ShapeDtypeStruct((batch, 8, CONV_DIM), F32),
            jax.ShapeDtypeStruct((batch, SSM_STATE, SSM_INNER), F32),
        ],
        scratch_shapes=[
            pltpu.VMEM((8 + T, CONV_DIM), F32),
            pltpu.VMEM((T, CONV_DIM), F32),
            pltpu.VMEM((SSM_STATE, SSM_INNER), F32),
            pltpu.VMEM((T, SSM_INNER), F32),
        ],
        compiler_params=_params("parallel", "arbitrary"),
        name="ssd_mixer",
    )(u, u, dt_raw, dtT, conv0, st0, conv_w.T.astype(F32), row(conv_b),
      row(dt_bias), pair_rows(dt_bias), row(a_log), pair_rows(a_log),
      row(jnp.repeat(d_skip, SSM_HEAD_DIM)), row(g_ssm))


def _band_body(q_ref, k_ref, v_ref, tab_ref, qg_ref, o_ref, bias_var, bias_const, *, T, KB, pad, cps):
    c = pl.program_id(2)
    KA = BAND_CONST_KEYS
    d = B_HEAD_DIM

    @pl.when(c == 0)
    def _():
        for j in range(B_GQ):
            tj = tab_ref[0, j:j + 1, :]
            band = pltpu.roll(jnp.broadcast_to(tj, (T, 2 * d)), 2 * d - REL_CLIP, 1,
                              stride=1, stride_axis=0)
            bias_var[j * T:(j + 1) * T, :] = band[:, 0:KB - KA]
            bias_const[j * T:(j + 1) * T, :] = jnp.broadcast_to(tj[:, 0:1], (T, d))

    scale = 1.0 / math.sqrt(d)

    def chunk(cl, masked):
        start = pl.multiple_of((c * cps + cl) * T, T)
        k = k_ref[0, 0, pl.ds(start, KB), :]
        v = v_ref[0, 0, pl.ds(start, KB), :]
        q = jnp.concatenate(
            [(_rms_scale(q_ref[cl * T:(cl + 1) * T, j * d:(j + 1) * d].astype(F32)) * qg_ref[...] * scale
              ).astype(BF16) for j in range(B_GQ)], axis=0)
        s = _dot_nt(q, k)
        bc = bias_const[...]
        sa = s[:, 0:KA] + jnp.concatenate([bc] * (KA // d), axis=1)
        sb = s[:, KA:KB] + bias_var[...]
        if masked:
            sa = jnp.where(start + lax.broadcasted_iota(jnp.int32, (1, KA), 1) >= pad, sa, -jnp.inf)
            sb = jnp.where(start + KA + lax.broadcasted_iota(jnp.int32, (1, KB - KA), 1) >= pad, sb, -jnp.inf)
        m = jnp.maximum(jnp.max(sa, axis=-1, keepdims=True), jnp.max(sb, axis=-1, keepdims=True))
        pa = jnp.exp(sa - m)
        pb = jnp.exp(sb - m)
        l = jnp.sum(pa, axis=-1, keepdims=True) + jnp.sum(pb, axis=-1, keepdims=True)
        o = (_dot(pa.astype(BF16), v[0:KA]) + _dot(pb.astype(BF16), v[KA:KB])) / l
        for j in range(B_GQ):
            o_ref[cl * T:(cl + 1) * T, j * d:(j + 1) * d] = o[j * T:(j + 1) * T].astype(o_ref.dtype)

    if pad:
        @pl.when(c * (cps * T) < pad)
        def _():
            for cl in range(cps):
                chunk(cl, True)

        @pl.when(c * (cps * T) >= pad)
        def _():
            for cl in range(cps):
                chunk(cl, False)
    else:
        for cl in range(cps):
            chunk(cl, False)


def band_attention(u, k, v, table, q_gain, *, row0, batch, nc, T, KB, pad, cps):
    rows = batch * nc * T
    r0 = row0 // (cps * T)
    ns = nc // cps
    ktot = k.shape[2]
    gw = B_GQ * B_HEAD_DIM
    tab = table[:, jnp.clip(2 * B_HEAD_DIM - jnp.arange(2 * B_HEAD_DIM), 0, 2 * REL_CLIP)].astype(F32)
    tab = tab.reshape(B_KV_HEADS, B_GQ, 2 * B_HEAD_DIM)
    return pl.pallas_call(
        functools.partial(_band_body, T=T, KB=KB, pad=pad, cps=cps),
        grid=(batch, B_KV_HEADS, ns),
        in_specs=[
            pl.BlockSpec((cps * T, gw), lambda b, g, c: (r0 + b * ns + c, g)),
            pl.BlockSpec((1, 1, ktot, B_HEAD_DIM), lambda b, g, c: (b, g, 0, 0)),
            pl.BlockSpec((1, 1, ktot, B_HEAD_DIM), lambda b, g, c: (b, g, 0, 0)),
            pl.BlockSpec((1, B_GQ, 2 * B_HEAD_DIM), lambda b, g, c: (g, 0, 0)),
            pl.BlockSpec((1, B_HEAD_DIM), lambda b, g, c: (0, 0)),
        ],
        out_specs=pl.BlockSpec((cps * T, gw), lambda b, g, c: (b * ns + c, g)),
        out_shape=jax.ShapeDtypeStruct((rows, B_Q_WIDTH), BF16),
        scratch_shapes=[pltpu.VMEM((B_GQ * T, KB - BAND_CONST_KEYS), F32),
                        pltpu.VMEM((B_GQ * T, B_HEAD_DIM), F32)],
        compiler_params=_params("parallel", "parallel", "arbitrary"),
        name="band_attention",
    )(u, k, v, tab, q_gain.reshape(1, B_HEAD_DIM).astype(F32))


def _mem_body(q_ref, k_ref, v_ref, qg_ref, o_ref):
    scale = 1.0 / math.sqrt(MEM_HEAD_DIM)
    for h in range(MEM_HEADS):
        sl = slice(h * MEM_HEAD_DIM, (h + 1) * MEM_HEAD_DIM)
        qn = (_rms_scale(q_ref[:, sl].astype(F32)) * qg_ref[...] * scale).astype(BF16)
        s = _dot_nt(qn, k_ref[0, :, sl])
        m = jnp.max(s, axis=-1, keepdims=True)
        p = jnp.exp(s - m)
        l = jnp.sum(p, axis=-1, keepdims=True)
        o_ref[:, sl] = (_dot(p.astype(BF16), v_ref[0, :, sl]) / l).astype(o_ref.dtype)


def memory_attention(u, k, v, q_gain, *, row0, batch, nt, tq):
    rows = batch * nt * tq
    r0 = row0 // tq
    q_col_block = B_Q_WIDTH // MEM_WIDTH
    return pl.pallas_call(
        _mem_body,
        grid=(batch, nt),
        in_specs=[
            pl.BlockSpec((tq, MEM_WIDTH), lambda b, i: (r0 + b * nt + i, q_col_block)),
            pl.BlockSpec((1, MEM_TOKENS, MEM_WIDTH), lambda b, i: (b, 0, 0)),
            pl.BlockSpec((1, MEM_TOKENS, MEM_WIDTH), lambda b, i: (b, 0, 0)),
            pl.BlockSpec((1, MEM_HEAD_DIM), lambda b, i: (0, 0)),
        ],
        out_specs=pl.BlockSpec((tq, MEM_WIDTH), lambda b, i: (b * nt + i, 0)),
        out_shape=jax.ShapeDtypeStruct((rows, MEM_WIDTH), BF16),
        compiler_params=_params("parallel", "arbitrary"),
        name="memory_attention",
    )(u, k, v, q_gain.reshape(1, MEM_HEAD_DIM).astype(F32))


def _heads_major(x, n_heads):
    b, l, w = x.shape
    return x.reshape(b, l, n_heads, w // n_heads).transpose(0, 2, 1, 3)


def _ffn_act(h, g_ffn, wg, wu):
    (hn,) = rmsnorm(h, g_ffn.reshape(1, -1))
    return ffn_up(hn, wg, wu, ROW_TILE, 512)


def kernel(x_prompt, x_sample, mem_prompt, state_ssm, state_conv, cache_kv_k, cache_kv_v, cache_mem_k, cache_mem_v, g_mix, w_in_a, conv_w, conv_b, dt_bias, a_log, d_skip, g_ssm, w_in_b, rel_bias, q_norm_b, g_kv, w_kv, k_norm_kv, g_mem, w_mem_kv, q_norm_mem, k_norm_mem, w_out, g_ffn, w_ffn_gate, w_ffn_up, w_ffn_down):
    S, NS = SEQ, N_SAMPLE_ROWS
    ff_pad = D_FF_PAD - D_FF
    bf = lambda w: w.astype(BF16)

    wa = w_in_a[0]
    z_end, xbc_end, dt_end = SSM_INNER, SSM_INNER + CONV_DIM, SSM_INNER + CONV_DIM + SSM_HEADS
    wa_main = jnp.concatenate([bf(wa[:, :z_end]), bf(wa[:, dt_end:]), bf(wa[:, z_end:xbc_end])], axis=1)
    wa_dt = jnp.pad(bf(wa[:, xbc_end:dt_end]), ((0, 0), (0, DT_PAD - SSM_HEADS)))
    wb = bf(w_in_b[0])
    wkv = bf(w_kv)
    wmem = [bf(w_mem_kv[l]) for l in range(2)]
    wout = [bf(w_out[l]) for l in range(2)]
    wg = [jnp.pad(bf(w_ffn_gate[l]), ((0, 0), (0, ff_pad))) for l in range(2)]
    wu = [jnp.pad(bf(w_ffn_up[l]), ((0, 0), (0, ff_pad))) for l in range(2)]
    wd = [jnp.pad(bf(w_ffn_down[l]), ((0, ff_pad), (0, 0))) for l in range(2)]

    xp = x_prompt.reshape(S, D_MODEL)
    xs = x_sample.reshape(NS, D_MODEL)

    mem_k32, mem_v32, mem_k16, mem_v16 = [], [], [], []
    mem_x = mem_prompt.reshape(MEM_TOKENS, D_MODEL)
    for l in range(2):
        (mn,) = rmsnorm(mem_x, g_mem[l].reshape(1, -1))
        kv = matmul(mn, wmem[l], F32, 256, 1024)
        k32, k16, v16 = kv_post(kv, k_norm_mem[l], MEM_HEAD_DIM)
        mem_k32.append(k32.reshape(1, MEM_TOKENS, MEM_HEADS, MEM_HEAD_DIM))
        mem_v32.append(kv[:, MEM_WIDTH:].reshape(1, MEM_TOKENS, MEM_HEADS, MEM_HEAD_DIM))
        mem_k16.append(k16.reshape(1, MEM_TOKENS, MEM_WIDTH))
        mem_v16.append(v16.reshape(1, MEM_TOKENS, MEM_WIDTH))
    cmk = [bf(cache_mem_k[l]).reshape(DEC_BATCH, MEM_TOKENS, MEM_WIDTH) for l in range(2)]
    cmv = [bf(cache_mem_v[l]).reshape(DEC_BATCH, MEM_TOKENS, MEM_WIDTH) for l in range(2)]

    def mem_attn(u, l):
        yp = memory_attention(u, mem_k16[l], mem_v16[l], q_norm_mem[l], row0=0, batch=1, nt=S // 512, tq=512)
        ys = memory_attention(u, cmk[l], cmv[l], q_norm_mem[l], row0=S, batch=DEC_BATCH, nt=1, tq=DEC_SEQ)
        return yp, ys

    hn = rmsnorm_rows2(xp, xs, g_mix[0])
    u = matmul(hn, wa_main, BF16, ROW_TILE, 1024)
    dt_raw = matmul(hn, wa_dt, F32, ROW_TILE, DT_PAD)

    def to_state_t(s):
        b = s.shape[0]
        return s.reshape(b, SSM_INNER, SSM_STATE).transpose(0, 2, 1)

    def from_state_t(s):
        b = s.shape[0]
        return s.transpose(0, 2, 1).reshape(b, SSM_HEADS, SSM_HEAD_DIM, SSM_STATE)

    ssd = functools.partial(ssd_mixer, u, dt_raw, conv_w=conv_w[0], conv_b=conv_b[0], dt_bias=dt_bias[0],
                            a_log=a_log[0], d_skip=d_skip[0], g_ssm=g_ssm[0])
    tail = CONV_WIDTH - 1
    conv0_p = jnp.zeros((1, 8, CONV_DIM), F32)
    st0_p = jnp.zeros((1, SSM_STATE, SSM_INNER), F32)
    y_p, conv_p, st_p = ssd(conv0=conv0_p, st0=st0_p, row0=0, batch=1, nc=S // CHUNK, T=CHUNK)
    conv0_s = jnp.pad(state_conv[0], ((0, 0), (8 - tail, 0), (0, 0)))
    y_s, conv_s, st_s = ssd(conv0=conv0_s, st0=to_state_t(state_ssm[0]), row0=S, batch=DEC_BATCH, nc=1, T=DEC_SEQ)
    ym_p, ym_s = mem_attn(u, 0)
    h = matmul_out(y_p, y_s, ym_p, ym_s, wout[0], xp, xs, 0, 1024, 2048)
    h = ffn_down(_ffn_act(h, g_ffn[0], wg[0], wu[0]), wd[0], h, row0=0, rows=N_ROWS,
                 tm=ROW_TILE, tn=1024, tk=2816)

    hkv, hn = rmsnorm(h, jnp.stack([g_kv, g_mix[1]]))
    kv = matmul(hkv, wkv, F32, ROW_TILE, 1024)
    kvw = B_KV_HEADS * B_HEAD_DIM
    k32, k16, v16 = kv_post(kv, k_norm_kv, B_HEAD_DIM)
    u = matmul(hn, wb, BF16, ROW_TILE, 1024)

    kp = jnp.pad(_heads_major(k16[:S].reshape(1, S, kvw), B_KV_HEADS), ((0, 0), (0, 0), (BAND_PAST, 0), (0, 0)))
    vp = jnp.pad(_heads_major(v16[:S].reshape(1, S, kvw), B_KV_HEADS), ((0, 0), (0, 0), (BAND_PAST, 0), (0, 0)))
    yb_p = band_attention(u, kp, vp, rel_bias[0], q_norm_b[0], row0=0, batch=1, nc=S // CHUNK, T=CHUNK,
                          KB=BAND_PAST + CHUNK, pad=BAND_PAST, cps=BAND_CHUNKS_PER_STEP)

    wc = cache_kv_k.shape[1]
    ks = jnp.concatenate([bf(cache_kv_k).reshape(DEC_BATCH, wc, kvw), k16[S:].reshape(DEC_BATCH, DEC_SEQ, kvw)], axis=1)
    vs = jnp.concatenate([bf(cache_kv_v).reshape(DEC_BATCH, wc, kvw), v16[S:].reshape(DEC_BATCH, DEC_SEQ, kvw)], axis=1)
    yb_s = band_attention(u, _heads_major(ks, B_KV_HEADS), _heads_major(vs, B_KV_HEADS), rel_bias[0], q_norm_b[0],
                          row0=S, batch=DEC_BATCH, nc=1, T=DEC_SEQ, KB=wc + DEC_SEQ, pad=0, cps=1)
    ym_p, ym_s = mem_attn(u, 1)
    h = matmul_out(yb_p, yb_s, ym_p, ym_s, wout[1], h, h, N_ROWS // SUB_ROWS - 1, 1024, 2048)
    act = _ffn_act(h, g_ffn[1], wg[1], wu[1])
    y_prompt = ffn_down(act, wd[1], h, row0=0, rows=S, tm=1024, tn=1024, tk=2816)
    y_sample = ffn_down(act, wd[1], h, row0=S, rows=NS, tm=NS, tn=1024, tk=2816)

    keep = min(BAND_PAST, S)
    kv_shape = (B_KV_HEADS, B_HEAD_DIM)
    return (
        y_prompt.reshape(1, S, D_MODEL),
        y_sample.reshape(DEC_BATCH, DEC_SEQ, D_MODEL),
        from_state_t(st_p)[None],
        conv_p[:, 8 - tail:][None],
        k32[S - keep:S].reshape(1, keep, *kv_shape),
        kv[S - keep:S, kvw:].reshape(1, keep, *kv_shape),
        jnp.stack(mem_k32),
        jnp.stack(mem_v32),
        from_state_t(st_s)[None],
        conv_s[:, 8 - tail:][None],
        k32[S:].reshape(DEC_BATCH, DEC_SEQ, *kv_shape),
        kv[S:, kvw:].reshape(DEC_BATCH, DEC_SEQ, *kv_shape),
    )
```

```python
import functools
import math

import jax
import jax.numpy as jnp
from jax import lax
from jax.experimental import pallas as pl
from jax.experimental.pallas import tpu as pltpu

F32 = jnp.float32
BF16 = jnp.bfloat16

D_MODEL = 4096
SEQ = 8192
DEC_BATCH = 8
DEC_SEQ = 32
PAST_LEN = 2048
CHUNK = 64
RMS_EPS = 1e-6

SSM_HEAD_DIM = 64
SSM_INNER = 6144
SSM_HEADS = 96
SSM_PAIRS = SSM_HEADS // 2
SSM_GROUPS = 8
SSM_PPG = SSM_PAIRS // SSM_GROUPS
SSM_STATE = 128
CONV_WIDTH = 4
CONV_DIM = 8192
XBC_BLOCK = 2048

B_HEAD_DIM = 128
B_Q_WIDTH = 6144
B_HEADS = 48
B_KV_HEADS = 8
B_GQ = 6
BAND_PAST = 512
REL_CLIP = 64
BAND_CONST_KEYS = 384
BAND_CHUNKS_PER_STEP = 4

MEM_TOKENS = 256
MEM_WIDTH = 2048
MEM_HEADS = 4
MEM_HEAD_DIM = 512

D_FF = 11008
D_FF_PAD = 11264
DT_PAD = 128

N_SAMPLE_ROWS = DEC_BATCH * DEC_SEQ
N_ROWS = SEQ + N_SAMPLE_ROWS
ROW_TILE = 768
WRES_ROW_TILE = 384
SUB_ROWS = 256
SUBS_PER_TILE = ROW_TILE // SUB_ROWS

V7X_VMEM_LIMIT_BYTES = 56 * 1024 * 1024


def _params(*sem):
    return pltpu.CompilerParams(dimension_semantics=sem, vmem_limit_bytes=V7X_VMEM_LIMIT_BYTES)


def _sigmoid(x):
    return 1.0 / (1.0 + jnp.exp(-x))


def _softplus(x):
    return jnp.maximum(x, 0.0) + jnp.log1p(jnp.exp(-jnp.abs(x)))


def _dot(a, b):
    return jnp.dot(a, b, preferred_element_type=F32)


def _dot_nt(a, b):
    return lax.dot_general(a, b, (((1,), (1,)), ((), ())), preferred_element_type=F32)


def _dot_exact(a, b):
    return jnp.dot(a, b, precision=lax.Precision.HIGHEST, preferred_element_type=F32)


def _rms_scale(x):
    return x * lax.rsqrt(jnp.mean(x * x, axis=-1, keepdims=True) + RMS_EPS)


def _rmsnorm_body(x_ref, g_ref, *o_refs):
    xn = _rms_scale(x_ref[...])
    for j, o_ref in enumerate(o_refs):
        o_ref[...] = (xn * g_ref[j:j + 1, :]).astype(o_ref.dtype)


def rmsnorm(x, gains, tm=256):
    m, d = x.shape
    n = gains.shape[0]
    return pl.pallas_call(
        _rmsnorm_body,
        grid=(m // tm,),
        in_specs=[pl.BlockSpec((tm, d), lambda i: (i, 0)),
                  pl.BlockSpec((n, d), lambda i: (0, 0))],
        out_specs=[pl.BlockSpec((tm, d), lambda i: (i, 0))] * n,
        out_shape=[jax.ShapeDtypeStruct((m, d), BF16)] * n,
        compiler_params=_params("parallel"),
        name="rmsnorm",
    )(x, gains)


def _rmsnorm2_body(xp_ref, xs_ref, g_ref, o_ref, *, n_p):
    i = pl.program_id(0)

    @pl.when(i < n_p)
    def _():
        o_ref[...] = (_rms_scale(xp_ref[...]) * g_ref[...]).astype(o_ref.dtype)

    @pl.when(i >= n_p)
    def _():
        o_ref[...] = (_rms_scale(xs_ref[...]) * g_ref[...]).astype(o_ref.dtype)


def rmsnorm_rows2(xp, xs, gain):
    tm = SUB_ROWS
    d = xp.shape[1]
    n_p, n_s = xp.shape[0] // tm, xs.shape[0] // tm
    return pl.pallas_call(
        functools.partial(_rmsnorm2_body, n_p=n_p),
        grid=(n_p + n_s,),
        in_specs=[pl.BlockSpec((tm, d), lambda i: (jnp.minimum(i, n_p - 1), 0)),
                  pl.BlockSpec((tm, d), lambda i: (jnp.maximum(i - n_p, 0), 0)),
                  pl.BlockSpec((1, d), lambda i: (0, 0))],
        out_specs=pl.BlockSpec((tm, d), lambda i: (i, 0)),
        out_shape=jax.ShapeDtypeStruct((xp.shape[0] + xs.shape[0], d), BF16),
        compiler_params=_params("parallel"),
        name="rmsnorm_rows2",
    )(xp, xs, gain.reshape(1, d))


def _mm_body(x_ref, w_ref, o_ref):
    o_ref[...] = _dot(x_ref[...], w_ref[...]).astype(o_ref.dtype)


def matmul(x, w, out_dtype, tm, tn):
    m, k = x.shape
    n = w.shape[1]
    return pl.pallas_call(
        _mm_body,
        grid=(m // tm, n // tn),
        in_specs=[pl.BlockSpec((tm, k), lambda i, j: (i, 0)),
                  pl.BlockSpec((k, tn), lambda i, j: (0, j))],
        out_specs=pl.BlockSpec((tm, tn), lambda i, j: (i, j)),
        out_shape=jax.ShapeDtypeStruct((m, n), out_dtype),
        compiler_params=_params("parallel", "arbitrary"),
        name="matmul",
    )(x, w)


def _wres_body(x_ref, *refs, n_w, n_valid, tn, epilogue):
    w_refs, o_ref, wbf_refs = refs[:n_w], refs[n_w], refs[n_w + 1:]
    j, i = pl.program_id(0), pl.program_id(1)

    @pl.when(i == 0)
    def _():
        for w_ref, wbf in zip(w_refs, wbf_refs):
            w = w_ref[...]
            if n_valid % tn:
                col = j * tn + lax.broadcasted_iota(jnp.int32, w.shape, 1)
                w = jnp.where(col < n_valid, w, 0.0)
            wbf[...] = w.astype(BF16)

    x = x_ref[...]
    o_ref[...] = epilogue(*[_dot(x, wbf[...]) for wbf in wbf_refs]).astype(o_ref.dtype)


def matmul_wres(x, ws, layer, n_out, out_dtype, tm, tn, epilogue=lambda acc: acc):
    m, k = x.shape
    n_valid = min(n_out, ws[0].shape[2])
    return pl.pallas_call(
        functools.partial(_wres_body, n_w=len(ws), n_valid=n_valid, tn=tn, epilogue=epilogue),
        grid=(n_out // tn, m // tm),
        in_specs=[pl.BlockSpec((tm, k), lambda j, i: (i, 0))]
                 + [pl.BlockSpec((None, k, tn), lambda j, i: (layer, 0, j))] * len(ws),
        out_specs=pl.BlockSpec((tm, tn), lambda j, i: (i, j)),
        out_shape=jax.ShapeDtypeStruct((m, n_out), out_dtype),
        scratch_shapes=[pltpu.VMEM((k, tn), BF16)] * len(ws),
        compiler_params=_params("arbitrary", "arbitrary"),
        name="matmul_wres",
    )(x, *ws)


def _cast_rows_body(w_ref, o_ref, *, rows_valid, tr):
    w = w_ref[...]
    if rows_valid % tr:
        row = pl.program_id(1) * tr + lax.broadcasted_iota(jnp.int32, w.shape, 0)
        w = jnp.where(row < rows_valid, w, 0.0)
    o_ref[...] = w.astype(o_ref.dtype)


def cast_rows(w, rows_out, tr=512):
    nl, r, n = w.shape
    return pl.pallas_call(
        functools.partial(_cast_rows_body, rows_valid=r, tr=tr),
        grid=(nl, rows_out // tr),
        in_specs=[pl.BlockSpec((None, tr, n), lambda l, i: (l, i, 0))],
        out_specs=pl.BlockSpec((None, tr, n), lambda l, i: (l, i, 0)),
        out_shape=jax.ShapeDtypeStruct((nl, rows_out, n), BF16),
        compiler_params=_params("parallel", "parallel"),
        name="cast_rows",
    )(w)


def _mm_out_body(xap_ref, xas_ref, xbp_ref, xbs_ref, w_ref, rp_ref, rs_ref, o_ref, acc_ref, *, n_a):
    i, k = pl.program_id(0), pl.program_id(2)
    last_tile = i == pl.num_programs(0) - 1
    head = (SUBS_PER_TILE - 1) * SUB_ROWS

    @pl.when(k == 0)
    def _():
        acc_ref[0:head, :] = rp_ref[0:SUBS_PER_TILE - 1].reshape(head, -1)

        @pl.when(jnp.logical_not(last_tile))
        def _():
            acc_ref[head:, :] = rp_ref[SUBS_PER_TILE - 1]

        @pl.when(last_tile)
        def _():
            acc_ref[head:, :] = rs_ref[0]

    def accumulate(p_ref, s_ref):
        @pl.when(jnp.logical_not(last_tile))
        def _():
            acc_ref[...] += _dot(p_ref[...].reshape(ROW_TILE, -1), w_ref[...])

        @pl.when(last_tile)
        def _():
            acc_ref[0:head, :] += _dot(p_ref[0:SUBS_PER_TILE - 1].reshape(head, -1), w_ref[...])
            acc_ref[head:, :] += _dot(s_ref[0], w_ref[...])

    @pl.when(k < n_a)
    def _():
        accumulate(xap_ref, xas_ref)

    @pl.when(k >= n_a)
    def _():
        accumulate(xbp_ref, xbs_ref)

    @pl.when(k == pl.num_programs(2) - 1)
    def _():
        o_ref[...] = acc_ref[...]


def matmul_out(xa_p, xa_s, xb_p, xb_s, w, layer, r_p, r_s, r_s_block, tn, tk):
    sub3 = lambda a: a.reshape(a.shape[0] // SUB_ROWS, SUB_ROWS, a.shape[1])
    ka, kb, n = xa_p.shape[1], xb_p.shape[1], w.shape[2]
    n_a, n_b = ka // tk, kb // tk
    ka_idx = lambda k: jnp.minimum(k, n_a - 1)
    kb_idx = lambda k: jnp.maximum(k - n_a, 0)
    nt = N_ROWS // ROW_TILE
    spt = SUBS_PER_TILE
    return pl.pallas_call(
        functools.partial(_mm_out_body, n_a=n_a),
        grid=(nt, n // tn, n_a + n_b),
        in_specs=[pl.BlockSpec((spt, SUB_ROWS, tk), lambda i, j, k: (i, 0, ka_idx(k))),
                  pl.BlockSpec((1, SUB_ROWS, tk), lambda i, j, k: (0, 0, ka_idx(k))),
                  pl.BlockSpec((spt, SUB_ROWS, tk), lambda i, j, k: (i, 0, kb_idx(k))),
                  pl.BlockSpec((1, SUB_ROWS, tk), lambda i, j, k: (0, 0, kb_idx(k))),
                  pl.BlockSpec((None, tk, tn), lambda i, j, k: (layer, k, j)),
                  pl.BlockSpec((spt, SUB_ROWS, tn), lambda i, j, k: (i, 0, j)),
                  pl.BlockSpec((1, SUB_ROWS, tn), lambda i, j, k: (r_s_block, 0, j))],
        out_specs=pl.BlockSpec((ROW_TILE, tn), lambda i, j, k: (i, j)),
        out_shape=jax.ShapeDtypeStruct((N_ROWS, n), F32),
        scratch_shapes=[pltpu.VMEM((ROW_TILE, tn), F32)],
        compiler_params=_params("parallel", "arbitrary", "arbitrary"),
        name="matmul_out",
    )(sub3(xa_p), sub3(xa_s), sub3(xb_p), sub3(xb_s), w, sub3(r_p), sub3(r_s))


def _silu_gate(g, u):
    return g * _sigmoid(g) * u


def _ffn_down_body(x_ref, w_ref, r_ref, o_ref, acc_ref):
    k = pl.program_id(2)

    @pl.when(k == 0)
    def _():
        acc_ref[...] = r_ref[...]

    acc_ref[...] += _dot(x_ref[...], w_ref[...])

    @pl.when(k == pl.num_programs(2) - 1)
    def _():
        o_ref[...] = acc_ref[...]


def ffn_down(x, w, layer, res, *, row0, rows, tm, tn, tk):
    kk = x.shape[1]
    n = w.shape[2]
    r0 = row0 // tm
    return pl.pallas_call(
        _ffn_down_body,
        grid=(rows // tm, n // tn, kk // tk),
        in_specs=[pl.BlockSpec((tm, tk), lambda i, j, k: (r0 + i, k)),
                  pl.BlockSpec((None, tk, tn), lambda i, j, k: (layer, k, j)),
                  pl.BlockSpec((tm, tn), lambda i, j, k: (r0 + i, j))],
        out_specs=pl.BlockSpec((tm, tn), lambda i, j, k: (i, j)),
        out_shape=jax.ShapeDtypeStruct((rows, n), F32),
        scratch_shapes=[pltpu.VMEM((tm, tn), F32)],
        compiler_params=_params("parallel", "arbitrary", "arbitrary"),
        name="ffn_down",
    )(x, w, res)


def _kv_post_body(kv_ref, g_ref, k32_ref, k16_ref, v16_ref, *, width, head_dim):
    for h in range(width // head_dim):
        sl = slice(h * head_dim, (h + 1) * head_dim)
        kn = _rms_scale(kv_ref[:, sl]) * g_ref[...]
        k32_ref[:, sl] = kn
        k16_ref[:, sl] = kn.astype(BF16)
    v16_ref[...] = kv_ref[:, width:].astype(BF16)


def kv_post(kv, gain, head_dim, tm=256):
    m, w2 = kv.shape
    width = w2 // 2
    return pl.pallas_call(
        functools.partial(_kv_post_body, width=width, head_dim=head_dim),
        grid=(m // tm,),
        in_specs=[pl.BlockSpec((tm, w2), lambda i: (i, 0)),
                  pl.BlockSpec((1, head_dim), lambda i: (0, 0))],
        out_specs=[pl.BlockSpec((tm, width), lambda i: (i, 0))] * 3,
        out_shape=[jax.ShapeDtypeStruct((m, width), F32),
                   jax.ShapeDtypeStruct((m, width), BF16),
                   jax.ShapeDtypeStruct((m, width), BF16)],
        compiler_params=_params("parallel"),
        name="kv_post",
    )(kv, gain.reshape(1, head_dim))


def _ssd_body(z_ref, xbc0_ref, xbc1_ref, xbc2_ref, xbc3_ref, dt_ref, dtT_ref, conv0_ref, st0_ref, cw_ref, cb_ref,
              dtb_ref, dtbT_ref, alog_ref, alogT_ref, dskip_ref, gssm_ref,
              y_ref, convn_ref, stn_ref,
              xbuf, xc, st, yacc, *, T):
    c = pl.program_id(1)
    nc = pl.num_programs(1)
    P, N = SSM_HEAD_DIM, SSM_STATE
    T2, PW = 2 * T, 2 * P
    tail = CONV_WIDTH - 1

    @pl.when(c == 0)
    def _():
        xbuf[0:8, :] = conv0_ref[0]
        st[...] = st0_ref[0]

    for q, xbc_ref in enumerate((xbc0_ref, xbc1_ref, xbc2_ref, xbc3_ref)):
        xbuf[8:8 + T, q * XBC_BLOCK:(q + 1) * XBC_BLOCK] = xbc_ref[...].astype(F32)
    acc = cb_ref[...] + cw_ref[0:1, :] * xbuf[8 - tail:8 - tail + T, :]
    for tap in range(1, CONV_WIDTH):
        acc = acc + cw_ref[tap:tap + 1, :] * xbuf[8 - tail + tap:8 - tail + tap + T, :]
    xc[...] = acc * _sigmoid(acc)
    last_rows = xbuf[T:T + 8, :]
    convn_ref[0] = last_rows
    xbuf[0:8, :] = last_rows

    iota = lambda shape, d: lax.broadcasted_iota(jnp.int32, shape, d)
    tri = iota((T, T), 1) <= iota((T, T), 0)
    dt = _softplus(dt_ref[:, 0:SSM_HEADS] + dtb_ref[...])
    cum = _dot_exact(tri.astype(F32), dt * (-jnp.exp(alog_ref[...])))

    half = lambda x, size: jnp.where(x >= size, 1, 0)
    ra, rb = iota((T2, T2), 0), iota((T2, T2), 1)
    same = half(ra, T) == half(rb, T)
    dtT = _softplus(dtT_ref[0] + dtbT_ref[...])
    dtaT = dtT * (-jnp.exp(alogT_ref[...]))
    cumT = _dot_exact(dtaT, jnp.where(same & (ra <= rb), 1.0, 0.0))
    lastT = _dot_exact(dtaT, jnp.where(same, 1.0, 0.0))
    same_p = half(iota((T2, PW), 0), T) == half(iota((T2, PW), 1), P)
    edec = jnp.exp(_dot_exact(dtaT, jnp.where(same_p, 1.0, 0.0)))
    wT = dtT * jnp.exp(lastT - cumT)

    left = iota((T, PW), 1) < P
    left2 = iota((T, T2), 1) < T
    tri2 = (iota((T, T2), 1) - T * half(iota((T, T2), 1), T)) <= iota((T, T2), 0)
    bc_off = SSM_INNER
    cc_off = SSM_INNER + SSM_GROUPS * N

    for g in range(SSM_GROUPS):
        bg = xc[:, bc_off + g * N:bc_off + (g + 1) * N]
        cg16 = xc[:, cc_off + g * N:cc_off + (g + 1) * N].astype(BF16)
        bg2 = jnp.concatenate([bg, bg], axis=0)
        cb2 = _dot_nt(cg16, bg2.astype(BF16))
        bgT2 = bg2.T
        gw = SSM_PPG * PW
        yoff = _dot(cg16, st[:, g * gw:(g + 1) * gw].astype(BF16))
        for k in range(SSM_PPG):
            q = g * SSM_PPG + k
            ps = slice(q * PW, (q + 1) * PW)
            b0 = jnp.broadcast_to(cum[:, 2 * q:2 * q + 1], (T, PW))
            b1 = jnp.broadcast_to(cum[:, 2 * q + 1:2 * q + 2], (T, PW))
            col_p = jnp.where(left, b0, b1)
            col_2 = col_p if T2 == PW else jnp.where(left2, b0[:, 0:T2], b1[:, 0:T2])
            dec = jnp.where(tri2, jnp.exp(col_2 - cumT[q:q + 1, :]), 0.0)
            m2 = (cb2 * dec * dtT[q:q + 1, :]).astype(BF16)
            xp = xc[:, ps]
            bd = jnp.concatenate([jnp.where(left, xp, 0.0), jnp.where(left, 0.0, xp)],
                                 axis=0).astype(BF16)
            yacc[:, ps] = _dot(m2, bd) + yoff[:, k * PW:(k + 1) * PW] * jnp.exp(col_p)
            btw = (bgT2 * wT[q:q + 1, :]).astype(BF16)
            st[:, ps] = st[:, ps] * edec[q:q + 1, :] + _dot(btw, bd)

    y = yacc[...] + xc[:, 0:SSM_INNER] * dskip_ref[...]
    z = z_ref[...].astype(F32)
    yg = y * (z * _sigmoid(z))
    y_ref[...] = (_rms_scale(yg) * gssm_ref[...]).astype(y_ref.dtype)

    @pl.when(c == nc - 1)
    def _():
        stn_ref[0] = st[...]


def ssd_mixer(u, dt_raw, conv0, st0, conv_w, conv_b, dt_bias, a_log, d_skip, g_ssm,
              *, row0, batch, nc, T):
    rows = batch * nc * T
    r0 = row0 // T
    nblk = batch * nc
    dtT = (dt_raw[row0:row0 + rows, :SSM_HEADS].reshape(nblk, T, SSM_PAIRS, 2)
           .transpose(0, 2, 3, 1).reshape(nblk, SSM_PAIRS, 2 * T))
    pair_rows = lambda v: jnp.repeat(v.astype(F32).reshape(SSM_PAIRS, 2), T, axis=1)
    row = lambda v: v.reshape(1, -1).astype(F32)
    const = lambda shape: pl.BlockSpec(shape, lambda b, c: (0,) * len(shape))
    xbc_specs = [pl.BlockSpec((T, XBC_BLOCK), functools.partial(
        lambda b, c, q: (r0 + b * nc + c, SSM_INNER // XBC_BLOCK + q), q=q)) for q in range(CONV_DIM // XBC_BLOCK)]
    return pl.pallas_call(
        functools.partial(_ssd_body, T=T),
        grid=(batch, nc),
        in_specs=[
            pl.BlockSpec((T, SSM_INNER), lambda b, c: (r0 + b * nc + c, 0)),
            *xbc_specs,
            pl.BlockSpec((T, DT_PAD), lambda b, c: (r0 + b * nc + c, 0)),
            pl.BlockSpec((1, SSM_PAIRS, 2 * T), lambda b, c: (b * nc + c, 0, 0)),
            pl.BlockSpec((1, 8, CONV_DIM), lambda b, c: (b, 0, 0)),
            pl.BlockSpec((1, SSM_STATE, SSM_INNER), lambda b, c: (b, 0, 0)),
            const((CONV_WIDTH, CONV_DIM)), const((1, CONV_DIM)),
            const((1, SSM_HEADS)), const((SSM_PAIRS, 2 * T)),
            const((1, SSM_HEADS)), const((SSM_PAIRS, 2 * T)),
            const((1, SSM_INNER)), const((1, SSM_INNER)),
        ],
        out_specs=[
            pl.BlockSpec((T, SSM_INNER), lambda b, c: (b * nc + c, 0)),
            pl.BlockSpec((1, 8, CONV_DIM), lambda b, c: (b, 0, 0)),
            pl.BlockSpec((1, SSM_STATE, SSM_INNER), lambda b, c: (b, 0, 0)),
        ],
        out_shape=[
            jax.ShapeDtypeStruct((rows, SSM_INNER), BF16),
            jax.ShapeDtypeStruct((batch, 8, CONV_DIM), F32),
            jax.ShapeDtypeStruct((batch, SSM_STATE, SSM_INNER), F32),
        ],
        scratch_shapes=[
            pltpu.VMEM((8 + T, CONV_DIM), F32),
            pltpu.VMEM((T, CONV_DIM), F32),
            pltpu.VMEM((SSM_STATE, SSM_INNER), F32),
            pltpu.VMEM((T, SSM_INNER), F32),
        ],
        compiler_params=_params("parallel", "arbitrary"),
        name="ssd_mixer",
    )(u, u, u, u, u, dt_raw, dtT, conv0, st0, conv_w.T.astype(F32), row(conv_b),
      row(dt_bias), pair_rows(dt_bias), row(a_log), pair_rows(a_log),
      row(jnp.repeat(d_skip, SSM_HEAD_DIM)), row(g_ssm))


def _band_body(q_ref, k_ref, v_ref, tab_ref, qg_ref, o_ref, bias_var, bias_const, *, T, KB, pad, cps):
    c = pl.program_id(2)
    KA = BAND_CONST_KEYS
    d = B_HEAD_DIM

    @pl.when(c == 0)
    def _():
        for j in range(B_GQ):
            tj = tab_ref[0, j:j + 1, :]
            band = pltpu.roll(jnp.broadcast_to(tj, (T, 2 * d)), 2 * d - REL_CLIP, 1,
                              stride=1, stride_axis=0)
            bias_var[j * T:(j + 1) * T, :] = band[:, 0:KB - KA]
            bias_const[j * T:(j + 1) * T, :] = jnp.broadcast_to(tj[:, 0:1], (T, d))

    scale = 1.0 / math.sqrt(d)

    def chunk(cl, masked):
        start = pl.multiple_of((c * cps + cl) * T, T)
        k = k_ref[0, 0, pl.ds(start, KB), :]
        v = v_ref[0, 0, pl.ds(start, KB), :]
        q = jnp.concatenate(
            [(_rms_scale(q_ref[cl * T:(cl + 1) * T, j * d:(j + 1) * d].astype(F32)) * qg_ref[...] * scale
              ).astype(BF16) for j in range(B_GQ)], axis=0)
        s = _dot_nt(q, k)
        bc = bias_const[...]
        sa = s[:, 0:KA] + jnp.concatenate([bc] * (KA // d), axis=1)
        sb = s[:, KA:KB] + bias_var[...]
        if masked:
            sa = jnp.where(start + lax.broadcasted_iota(jnp.int32, (1, KA), 1) >= pad, sa, -jnp.inf)
            sb = jnp.where(start + KA + lax.broadcasted_iota(jnp.int32, (1, KB - KA), 1) >= pad, sb, -jnp.inf)
        m = jnp.maximum(jnp.max(sa, axis=-1, keepdims=True), jnp.max(sb, axis=-1, keepdims=True))
        pa = jnp.exp(sa - m)
        pb = jnp.exp(sb - m)
        l = jnp.sum(pa, axis=-1, keepdims=True) + jnp.sum(pb, axis=-1, keepdims=True)
        o = (_dot(pa.astype(BF16), v[0:KA]) + _dot(pb.astype(BF16), v[KA:KB])) / l
        for j in range(B_GQ):
            o_ref[cl * T:(cl + 1) * T, j * d:(j + 1) * d] = o[j * T:(j + 1) * T].astype(o_ref.dtype)

    if pad:
        @pl.when(c * (cps * T) < pad)
        def _():
            for cl in range(cps):
                chunk(cl, True)

        @pl.when(c * (cps * T) >= pad)
        def _():
            for cl in range(cps):
                chunk(cl, False)
    else:
        for cl in range(cps):
            chunk(cl, False)


def band_attention(u, k, v, table, q_gain, *, row0, batch, nc, T, KB, pad, cps):
    rows = batch * nc * T
    r0 = row0 // (cps * T)
    ns = nc // cps
    ktot = k.shape[2]
    gw = B_GQ * B_HEAD_DIM
    tab = table[:, jnp.clip(2 * B_HEAD_DIM - jnp.arange(2 * B_HEAD_DIM), 0, 2 * REL_CLIP)].astype(F32)
    tab = tab.reshape(B_KV_HEADS, B_GQ, 2 * B_HEAD_DIM)
    return pl.pallas_call(
        functools.partial(_band_body, T=T, KB=KB, pad=pad, cps=cps),
        grid=(batch, B_KV_HEADS, ns),
        in_specs=[
            pl.BlockSpec((cps * T, gw), lambda b, g, c: (r0 + b * ns + c, g)),
            pl.BlockSpec((1, 1, ktot, B_HEAD_DIM), lambda b, g, c: (b, g, 0, 0)),
            pl.BlockSpec((1, 1, ktot, B_HEAD_DIM), lambda b, g, c: (b, g, 0, 0)),
            pl.BlockSpec((1, B_GQ, 2 * B_HEAD_DIM), lambda b, g, c: (g, 0, 0)),
            pl.BlockSpec((1, B_HEAD_DIM), lambda b, g, c: (0, 0)),
        ],
        out_specs=pl.BlockSpec((cps * T, gw), lambda b, g, c: (b * ns + c, g)),
        out_shape=jax.ShapeDtypeStruct((rows, B_Q_WIDTH), BF16),
        scratch_shapes=[pltpu.VMEM((B_GQ * T, KB - BAND_CONST_KEYS), F32),
                        pltpu.VMEM((B_GQ * T, B_HEAD_DIM), F32)],
        compiler_params=_params("parallel", "parallel", "arbitrary"),
        name="band_attention",
    )(u, k, v, tab, q_gain.reshape(1, B_HEAD_DIM).astype(F32))


def _mem_body(q_ref, k_ref, v_ref, qg_ref, o_ref):
    scale = 1.0 / math.sqrt(MEM_HEAD_DIM)
    for h in range(MEM_HEADS):
        sl = slice(h * MEM_HEAD_DIM, (h + 1) * MEM_HEAD_DIM)
        qn = (_rms_scale(q_ref[:, sl].astype(F32)) * qg_ref[...] * scale).astype(BF16)
        s = _dot_nt(qn, k_ref[0, :, sl])
        m = jnp.max(s, axis=-1, keepdims=True)
        p = jnp.exp(s - m)
        l = jnp.sum(p, axis=-1, keepdims=True)
        o_ref[:, sl] = (_dot(p.astype(BF16), v_ref[0, :, sl]) / l).astype(o_ref.dtype)


def memory_attention(u, k, v, q_gain, *, q_col_block, row0, batch, nt, tq):
    rows = batch * nt * tq
    r0 = row0 // tq
    return pl.pallas_call(
        _mem_body,
        grid=(batch, nt),
        in_specs=[
            pl.BlockSpec((tq, MEM_WIDTH), lambda b, i: (r0 + b * nt + i, q_col_block)),
            pl.BlockSpec((1, MEM_TOKENS, MEM_WIDTH), lambda b, i: (b, 0, 0)),
            pl.BlockSpec((1, MEM_TOKENS, MEM_WIDTH), lambda b, i: (b, 0, 0)),
            pl.BlockSpec((1, MEM_HEAD_DIM), lambda b, i: (0, 0)),
        ],
        out_specs=pl.BlockSpec((tq, MEM_WIDTH), lambda b, i: (b * nt + i, 0)),
        out_shape=jax.ShapeDtypeStruct((rows, MEM_WIDTH), BF16),
        compiler_params=_params("parallel", "arbitrary"),
        name="memory_attention",
    )(u, k, v, q_gain.reshape(1, MEM_HEAD_DIM).astype(F32))


def _heads_major(x, n_heads):
    b, l, w = x.shape
    return x.reshape(b, l, n_heads, w // n_heads).transpose(0, 2, 1, 3)


def _ffn_act(h, g_ffn, w_gate, w_up, layer):
    (hn,) = rmsnorm(h, g_ffn.reshape(1, -1))
    return matmul_wres(hn, [w_gate, w_up], layer, D_FF_PAD, BF16, WRES_ROW_TILE, 512, _silu_gate)


def kernel(x_prompt, x_sample, mem_prompt, state_ssm, state_conv, cache_kv_k, cache_kv_v, cache_mem_k, cache_mem_v, g_mix, w_in_a, conv_w, conv_b, dt_bias, a_log, d_skip, g_ssm, w_in_b, rel_bias, q_norm_b, g_kv, w_kv, k_norm_kv, g_mem, w_mem_kv, q_norm_mem, k_norm_mem, w_out, g_ffn, w_ffn_gate, w_ffn_up, w_ffn_down):
    S, NS = SEQ, N_SAMPLE_ROWS
    bf = lambda w: w.astype(BF16)

    main_w = SSM_INNER + CONV_DIM
    wa_tail = lax.optimization_barrier(w_in_a[0][:, main_w:])
    wa_qm = bf(wa_tail[:, SSM_HEADS:])
    wa_dt = jnp.pad(bf(wa_tail[:, :SSM_HEADS]), ((0, 0), (0, DT_PAD - SSM_HEADS)))
    wout = cast_rows(w_out, w_out.shape[1])
    wd = cast_rows(w_ffn_down, D_FF_PAD)

    xp = x_prompt.reshape(S, D_MODEL)
    xs = x_sample.reshape(NS, D_MODEL)

    mem_k32, mem_v32, mem_k16, mem_v16 = [], [], [], []
    mem_x = mem_prompt.reshape(MEM_TOKENS, D_MODEL)
    for l in range(2):
        (mn,) = rmsnorm(mem_x, g_mem[l].reshape(1, -1))
        kv = matmul_wres(mn, [w_mem_kv], l, 2 * MEM_WIDTH, F32, MEM_TOKENS, 1024)
        k32, k16, v16 = kv_post(kv, k_norm_mem[l], MEM_HEAD_DIM)
        mem_k32.append(k32.reshape(1, MEM_TOKENS, MEM_HEADS, MEM_HEAD_DIM))
        mem_v32.append(kv[:, MEM_WIDTH:].reshape(1, MEM_TOKENS, MEM_HEADS, MEM_HEAD_DIM))
        mem_k16.append(k16.reshape(1, MEM_TOKENS, MEM_WIDTH))
        mem_v16.append(v16.reshape(1, MEM_TOKENS, MEM_WIDTH))
    cmk = [bf(cache_mem_k[l]).reshape(DEC_BATCH, MEM_TOKENS, MEM_WIDTH) for l in range(2)]
    cmv = [bf(cache_mem_v[l]).reshape(DEC_BATCH, MEM_TOKENS, MEM_WIDTH) for l in range(2)]

    def mem_attn(u, q_col_block, l):
        attn = functools.partial(memory_attention, u, q_gain=q_norm_mem[l], q_col_block=q_col_block)
        yp = attn(k=mem_k16[l], v=mem_v16[l], row0=0, batch=1, nt=S // 512, tq=512)
        ys = attn(k=cmk[l], v=cmv[l], row0=S, batch=DEC_BATCH, nt=1, tq=DEC_SEQ)
        return yp, ys

    hn = rmsnorm_rows2(xp, xs, g_mix[0])
    u = matmul_wres(hn, [w_in_a], 0, main_w, BF16, WRES_ROW_TILE, 1024)
    q_mem = matmul(hn, wa_qm, BF16, ROW_TILE, 1024)
    dt_raw = matmul(hn, wa_dt, F32, ROW_TILE, DT_PAD)

    def to_state_t(s):
        b = s.shape[0]
        return s.reshape(b, SSM_INNER, SSM_STATE).transpose(0, 2, 1)

    def from_state_t(s):
        b = s.shape[0]
        return s.transpose(0, 2, 1).reshape(b, SSM_HEADS, SSM_HEAD_DIM, SSM_STATE)

    ssd = functools.partial(ssd_mixer, u, dt_raw, conv_w=conv_w[0], conv_b=conv_b[0], dt_bias=dt_bias[0],
                            a_log=a_log[0], d_skip=d_skip[0], g_ssm=g_ssm[0])
    tail = CONV_WIDTH - 1
    conv0_p = jnp.zeros((1, 8, CONV_DIM), F32)
    st0_p = jnp.zeros((1, SSM_STATE, SSM_INNER), F32)
    y_p, conv_p, st_p = ssd(conv0=conv0_p, st0=st0_p, row0=0, batch=1, nc=S // CHUNK, T=CHUNK)
    conv0_s = jnp.pad(state_conv[0], ((0, 0), (8 - tail, 0), (0, 0)))
    y_s, conv_s, st_s = ssd(conv0=conv0_s, st0=to_state_t(state_ssm[0]), row0=S, batch=DEC_BATCH, nc=1, T=DEC_SEQ)
    ym_p, ym_s = mem_attn(q_mem, 0, 0)
    h = matmul_out(y_p, y_s, ym_p, ym_s, wout, 0, xp, xs, 0, 1024, 2048)
    h = ffn_down(_ffn_act(h, g_ffn[0], w_ffn_gate, w_ffn_up, 0), wd, 0, h, row0=0, rows=N_ROWS,
                 tm=ROW_TILE, tn=1024, tk=2816)

    hkv, hn = rmsnorm(h, jnp.stack([g_kv, g_mix[1]]))
    kvw = B_KV_HEADS * B_HEAD_DIM
    kv = matmul_wres(hkv, [w_kv[None]], 0, 2 * kvw, F32, WRES_ROW_TILE, 1024)
    k32, k16, v16 = kv_post(kv, k_norm_kv, B_HEAD_DIM)
    u = matmul_wres(hn, [w_in_b], 0, B_Q_WIDTH + MEM_WIDTH, BF16, WRES_ROW_TILE, 1024)

    kp = jnp.pad(_heads_major(k16[:S].reshape(1, S, kvw), B_KV_HEADS), ((0, 0), (0, 0), (BAND_PAST, 0), (0, 0)))
    vp = jnp.pad(_heads_major(v16[:S].reshape(1, S, kvw), B_KV_HEADS), ((0, 0), (0, 0), (BAND_PAST, 0), (0, 0)))
    yb_p = band_attention(u, kp, vp, rel_bias[0], q_norm_b[0], row0=0, batch=1, nc=S // CHUNK, T=CHUNK,
                          KB=BAND_PAST + CHUNK, pad=BAND_PAST, cps=BAND_CHUNKS_PER_STEP)

    wc = cache_kv_k.shape[1]
    ks = jnp.concatenate([bf(cache_kv_k).reshape(DEC_BATCH, wc, kvw), k16[S:].reshape(DEC_BATCH, DEC_SEQ, kvw)], axis=1)
    vs = jnp.concatenate([bf(cache_kv_v).reshape(DEC_BATCH, wc, kvw), v16[S:].reshape(DEC_BATCH, DEC_SEQ, kvw)], axis=1)
    yb_s = band_attention(u, _heads_major(ks, B_KV_HEADS), _heads_major(vs, B_KV_HEADS), rel_bias[0], q_norm_b[0],
                          row0=S, batch=DEC_BATCH, nc=1, T=DEC_SEQ, KB=wc + DEC_SEQ, pad=0, cps=1)
    ym_p, ym_s = mem_attn(u, B_Q_WIDTH // MEM_WIDTH, 1)
    h = matmul_out(yb_p, yb_s, ym_p, ym_s, wout, 1, h, h, N_ROWS // SUB_ROWS - 1, 1024, 2048)
    act = _ffn_act(h, g_ffn[1], w_ffn_gate, w_ffn_up, 1)
    y_prompt = ffn_down(act, wd, 1, h, row0=0, rows=S, tm=1024, tn=1024, tk=2816)
    y_sample = ffn_down(act, wd, 1, h, row0=S, rows=NS, tm=NS, tn=1024, tk=2816)

    keep = min(BAND_PAST, S)
    kv_shape = (B_KV_HEADS, B_HEAD_DIM)
    return (
        y_prompt.reshape(1, S, D_MODEL),
        y_sample.reshape(DEC_BATCH, DEC_SEQ, D_MODEL),
        from_state_t(st_p)[None],
        conv_p[:, 8 - tail:][None],
        k32[S - keep:S].reshape(1, keep, *kv_shape),
        kv[S - keep:S, kvw:].reshape(1, keep, *kv_shape),
        jnp.stack(mem_k32),
        jnp.stack(mem_v32),
        from_state_t(st_s)[None],
        conv_s[:, 8 - tail:][None],
        k32[S:].reshape(DEC_BATCH, DEC_SEQ, *kv_shape),
        kv[S:, kvw:].reshape(DEC_BATCH, DEC_SEQ, *kv_shape),
    )
```

```python
import functools
import math

import jax
import jax.numpy as jnp
from jax import lax
from jax.experimental import pallas as pl
from jax.experimental.pallas import tpu as pltpu

F32 = jnp.float32
BF16 = jnp.bfloat16

D_MODEL = 4096
SEQ = 8192
DEC_BATCH = 8
DEC_SEQ = 32
PAST_LEN = 2048
CHUNK = 64
RMS_EPS = 1e-6

SSM_HEAD_DIM = 64
SSM_INNER = 6144
SSM_HEADS = 96
SSM_PAIRS = SSM_HEADS // 2
SSM_GROUPS = 8
SSM_PPG = SSM_PAIRS // SSM_GROUPS
SSM_STATE = 128
CONV_WIDTH = 4
CONV_DIM = 8192
XBC_BLOCK = 2048

B_HEAD_DIM = 128
B_Q_WIDTH = 6144
B_HEADS = 48
B_KV_HEADS = 8
B_GQ = 6
BAND_PAST = 512
REL_CLIP = 64
BAND_CONST_KEYS = 384
BAND_CHUNKS_PER_STEP = 4

MEM_TOKENS = 256
MEM_WIDTH = 2048
MEM_HEADS = 4
MEM_HEAD_DIM = 512

D_FF = 11008
D_FF_PAD = 11264
DT_PAD = 128

N_SAMPLE_ROWS = DEC_BATCH * DEC_SEQ
N_ROWS = SEQ + N_SAMPLE_ROWS
ROW_TILE = 768
WRES_ROW_TILE = 384
SUB_ROWS = 256
SUBS_PER_TILE = ROW_TILE // SUB_ROWS

V7X_VMEM_LIMIT_BYTES = 56 * 1024 * 1024


def _params(*sem):
    return pltpu.CompilerParams(dimension_semantics=sem, vmem_limit_bytes=V7X_VMEM_LIMIT_BYTES)


def _sigmoid(x):
    return 1.0 / (1.0 + jnp.exp(-x))


def _softplus(x):
    return jnp.maximum(x, 0.0) + jnp.log1p(jnp.exp(-jnp.abs(x)))


def _dot(a, b):
    return jnp.dot(a, b, preferred_element_type=F32)


def _dot_nt(a, b):
    return lax.dot_general(a, b, (((1,), (1,)), ((), ())), preferred_element_type=F32)


def _dot_exact(a, b):
    return jnp.dot(a, b, precision=lax.Precision.HIGHEST, preferred_element_type=F32)


def _rms_scale(x):
    return x * lax.rsqrt(jnp.mean(x * x, axis=-1, keepdims=True) + RMS_EPS)


def _rmsnorm_body(x_ref, g_ref, *o_refs):
    xn = _rms_scale(x_ref[...])
    for j, o_ref in enumerate(o_refs):
        o_ref[...] = (xn * g_ref[j:j + 1, :]).astype(o_ref.dtype)


def rmsnorm(x, gains, tm=256):
    m, d = x.shape
    n = gains.shape[0]
    return pl.pallas_call(
        _rmsnorm_body,
        grid=(m // tm,),
        in_specs=[pl.BlockSpec((tm, d), lambda i: (i, 0)),
                  pl.BlockSpec((n, d), lambda i: (0, 0))],
        out_specs=[pl.BlockSpec((tm, d), lambda i: (i, 0))] * n,
        out_shape=[jax.ShapeDtypeStruct((m, d), BF16)] * n,
        compiler_params=_params("parallel"),
        name="rmsnorm",
    )(x, gains)


def _rmsnorm2_body(xp_ref, xs_ref, g_ref, o_ref, *, n_p):
    i = pl.program_id(0)

    @pl.when(i < n_p)
    def _():
        o_ref[...] = (_rms_scale(xp_ref[...]) * g_ref[...]).astype(o_ref.dtype)

    @pl.when(i >= n_p)
    def _():
        o_ref[...] = (_rms_scale(xs_ref[...]) * g_ref[...]).astype(o_ref.dtype)


def rmsnorm_rows2(xp, xs, gain):
    tm = SUB_ROWS
    d = xp.shape[1]
    n_p, n_s = xp.shape[0] // tm, xs.shape[0] // tm
    return pl.pallas_call(
        functools.partial(_rmsnorm2_body, n_p=n_p),
        grid=(n_p + n_s,),
        in_specs=[pl.BlockSpec((tm, d), lambda i: (jnp.minimum(i, n_p - 1), 0)),
                  pl.BlockSpec((tm, d), lambda i: (jnp.maximum(i - n_p, 0), 0)),
                  pl.BlockSpec((1, d), lambda i: (0, 0))],
        out_specs=pl.BlockSpec((tm, d), lambda i: (i, 0)),
        out_shape=jax.ShapeDtypeStruct((xp.shape[0] + xs.shape[0], d), BF16),
        compiler_params=_params("parallel"),
        name="rmsnorm_rows2",
    )(xp, xs, gain.reshape(1, d))


def _wres_body(x_ref, *refs, n_w, layer, col0, n_valid, tn, transposed, epilogue):
    w_hbm, o_ref = refs[:n_w], refs[n_w]
    wbf, wf32, sem = refs[n_w + 1:2 * n_w + 1], refs[2 * n_w + 1:3 * n_w + 1], refs[3 * n_w + 1]
    j, i = pl.program_id(0), pl.program_id(1)
    nj = pl.num_programs(0)
    rem = n_valid % tn

    def for_each_copy(jj, slot, fn):
        def go(width):
            for w in range(n_w):
                if transposed:
                    src = w_hbm[w].at[layer, pl.ds(col0 + jj * tn, width), :]
                    dst = wf32[w].at[slot, pl.ds(0, width), :]
                else:
                    src = w_hbm[w].at[layer, :, pl.ds(col0 + jj * tn, width)]
                    dst = wf32[w].at[slot, :, pl.ds(0, width)]
                fn(pltpu.make_async_copy(src, dst, sem.at[w, slot]))
        if rem:
            pl.when(jj < nj - 1)(lambda: go(tn))
            pl.when(jj == nj - 1)(lambda: go(rem))
        else:
            go(tn)

    @pl.when(i == 0)
    def _():
        slot = j % 2

        @pl.when(j == 0)
        def _():
            for_each_copy(0, 0, lambda cp: cp.start())

        for_each_copy(j, slot, lambda cp: cp.wait())

        @pl.when(j + 1 < nj)
        def _():
            for_each_copy(j + 1, 1 - slot, lambda cp: cp.start())

        for w in range(n_w):
            wt = wf32[w][slot]
            if rem:
                col = j * tn + lax.broadcasted_iota(jnp.int32, wt.shape, 0 if transposed else 1)
                wt = jnp.where(col < n_valid, wt, 0.0)
            wbf[w][...] = (wt.T if transposed else wt).astype(BF16)

    x = x_ref[...]
    o_ref[...] = epilogue(*[_dot(x, wb[...]) for wb in wbf]).astype(o_ref.dtype)


def matmul_wres(x, ws, layer, n_out, out_dtype, tm, tn, epilogue=lambda acc: acc, transposed=False, col0=0):
    m, k = x.shape
    n_w = len(ws)
    n_valid = min(n_out, ws[0].shape[1 if transposed else 2] - col0)
    return pl.pallas_call(
        functools.partial(_wres_body, n_w=n_w, layer=layer, col0=col0, n_valid=n_valid, tn=tn,
                          transposed=transposed, epilogue=epilogue),
        grid=(n_out // tn, m // tm),
        in_specs=[pl.BlockSpec((tm, k), lambda j, i: (i, 0))] + [pl.BlockSpec(memory_space=pl.ANY)] * n_w,
        out_specs=pl.BlockSpec((tm, tn), lambda j, i: (i, j)),
        out_shape=jax.ShapeDtypeStruct((m, n_out), out_dtype),
        scratch_shapes=[pltpu.VMEM((k, tn), BF16)] * n_w
                       + [pltpu.VMEM((2, tn, k) if transposed else (2, k, tn), F32)] * n_w
                       + [pltpu.SemaphoreType.DMA((n_w, 2))],
        compiler_params=_params("arbitrary", "arbitrary"),
        name="matmul_wres",
    )(x, *ws)


def _cast_rows_body(w_ref, o_ref, *, rows_valid, tr):
    w = w_ref[...]
    if rows_valid % tr:
        row = pl.program_id(1) * tr + lax.broadcasted_iota(jnp.int32, w.shape, 0)
        w = jnp.where(row < rows_valid, w, 0.0)
    o_ref[...] = w.astype(o_ref.dtype)


def cast_rows(w, rows_out, tr=512):
    nl, r, n = w.shape
    return pl.pallas_call(
        functools.partial(_cast_rows_body, rows_valid=r, tr=tr),
        grid=(nl, rows_out // tr),
        in_specs=[pl.BlockSpec((None, tr, n), lambda l, i: (l, i, 0))],
        out_specs=pl.BlockSpec((None, tr, n), lambda l, i: (l, i, 0)),
        out_shape=jax.ShapeDtypeStruct((nl, rows_out, n), BF16),
        compiler_params=_params("parallel", "parallel"),
        name="cast_rows",
    )(w)


def _mm_out_body(xap_ref, xas_ref, xbp_ref, xbs_ref, w_ref, rp_ref, rs_ref, o_ref, acc_ref, *, n_a):
    i, k = pl.program_id(0), pl.program_id(2)
    last_tile = i == pl.num_programs(0) - 1
    head = (SUBS_PER_TILE - 1) * SUB_ROWS

    @pl.when(k == 0)
    def _():
        acc_ref[0:head, :] = rp_ref[0:SUBS_PER_TILE - 1].reshape(head, -1)

        @pl.when(jnp.logical_not(last_tile))
        def _():
            acc_ref[head:, :] = rp_ref[SUBS_PER_TILE - 1]

        @pl.when(last_tile)
        def _():
            acc_ref[head:, :] = rs_ref[0]

    def accumulate(p_ref, s_ref):
        @pl.when(jnp.logical_not(last_tile))
        def _():
            acc_ref[...] += _dot(p_ref[...].reshape(ROW_TILE, -1), w_ref[...])

        @pl.when(last_tile)
        def _():
            acc_ref[0:head, :] += _dot(p_ref[0:SUBS_PER_TILE - 1].reshape(head, -1), w_ref[...])
            acc_ref[head:, :] += _dot(s_ref[0], w_ref[...])

    @pl.when(k < n_a)
    def _():
        accumulate(xap_ref, xas_ref)

    @pl.when(k >= n_a)
    def _():
        accumulate(xbp_ref, xbs_ref)

    @pl.when(k == pl.num_programs(2) - 1)
    def _():
        o_ref[...] = acc_ref[...]


def matmul_out(xa_p, xa_s, xb_p, xb_s, w, layer, r_p, r_s, r_s_block, tn, tk):
    sub3 = lambda a: a.reshape(a.shape[0] // SUB_ROWS, SUB_ROWS, a.shape[1])
    ka, kb, n = xa_p.shape[1], xb_p.shape[1], w.shape[2]
    n_a, n_b = ka // tk, kb // tk
    ka_idx = lambda k: jnp.minimum(k, n_a - 1)
    kb_idx = lambda k: jnp.maximum(k - n_a, 0)
    nt = N_ROWS // ROW_TILE
    spt = SUBS_PER_TILE
    return pl.pallas_call(
        functools.partial(_mm_out_body, n_a=n_a),
        grid=(nt, n // tn, n_a + n_b),
        in_specs=[pl.BlockSpec((spt, SUB_ROWS, tk), lambda i, j, k: (i, 0, ka_idx(k))),
                  pl.BlockSpec((1, SUB_ROWS, tk), lambda i, j, k: (0, 0, ka_idx(k))),
                  pl.BlockSpec((spt, SUB_ROWS, tk), lambda i, j, k: (i, 0, kb_idx(k))),
                  pl.BlockSpec((1, SUB_ROWS, tk), lambda i, j, k: (0, 0, kb_idx(k))),
                  pl.BlockSpec((None, tk, tn), lambda i, j, k: (layer, k, j)),
                  pl.BlockSpec((spt, SUB_ROWS, tn), lambda i, j, k: (i, 0, j)),
                  pl.BlockSpec((1, SUB_ROWS, tn), lambda i, j, k: (r_s_block, 0, j))],
        out_specs=pl.BlockSpec((ROW_TILE, tn), lambda i, j, k: (i, j)),
        out_shape=jax.ShapeDtypeStruct((N_ROWS, n), F32),
        scratch_shapes=[pltpu.VMEM((ROW_TILE, tn), F32)],
        compiler_params=_params("parallel", "arbitrary", "arbitrary"),
        name="matmul_out",
    )(sub3(xa_p), sub3(xa_s), sub3(xb_p), sub3(xb_s), w, sub3(r_p), sub3(r_s))


def _silu_gate(g, u):
    return g * _sigmoid(g) * u


def _ffn_down_body(x_ref, w_ref, r_ref, o_ref, acc_ref):
    k = pl.program_id(2)

    @pl.when(k == 0)
    def _():
        acc_ref[...] = r_ref[...]

    acc_ref[...] += _dot(x_ref[...], w_ref[...])

    @pl.when(k == pl.num_programs(2) - 1)
    def _():
        o_ref[...] = acc_ref[...]


def ffn_down(x, w, layer, res, *, row0, rows, tm, tn, tk):
    kk = x.shape[1]
    n = w.shape[2]
    r0 = row0 // tm
    return pl.pallas_call(
        _ffn_down_body,
        grid=(rows // tm, n // tn, kk // tk),
        in_specs=[pl.BlockSpec((tm, tk), lambda i, j, k: (r0 + i, k)),
                  pl.BlockSpec((None, tk, tn), lambda i, j, k: (layer, k, j)),
                  pl.BlockSpec((tm, tn), lambda i, j, k: (r0 + i, j))],
        out_specs=pl.BlockSpec((tm, tn), lambda i, j, k: (i, j)),
        out_shape=jax.ShapeDtypeStruct((rows, n), F32),
        scratch_shapes=[pltpu.VMEM((tm, tn), F32)],
        compiler_params=_params("parallel", "arbitrary", "arbitrary"),
        name="ffn_down",
    )(x, w, res)


def _kv_post_body(kv_ref, g_ref, k32_ref, k16_ref, v16_ref, *, width, head_dim):
    for h in range(width // head_dim):
        sl = slice(h * head_dim, (h + 1) * head_dim)
        kn = _rms_scale(kv_ref[:, sl]) * g_ref[...]
        k32_ref[:, sl] = kn
        k16_ref[:, sl] = kn.astype(BF16)
    v16_ref[...] = kv_ref[:, width:].astype(BF16)


def kv_post(kv, gain, head_dim, tm=256):
    m, w2 = kv.shape
    width = w2 // 2
    return pl.pallas_call(
        functools.partial(_kv_post_body, width=width, head_dim=head_dim),
        grid=(m // tm,),
        in_specs=[pl.BlockSpec((tm, w2), lambda i: (i, 0)),
                  pl.BlockSpec((1, head_dim), lambda i: (0, 0))],
        out_specs=[pl.BlockSpec((tm, width), lambda i: (i, 0))] * 3,
        out_shape=[jax.ShapeDtypeStruct((m, width), F32),
                   jax.ShapeDtypeStruct((m, width), BF16),
                   jax.ShapeDtypeStruct((m, width), BF16)],
        compiler_params=_params("parallel"),
        name="kv_post",
    )(kv, gain.reshape(1, head_dim))


def _ssd_body(z_ref, xbc0_ref, xbc1_ref, xbc2_ref, xbc3_ref, dt_ref, dtT_ref, conv0_ref, st0_ref, cw_ref, cb_ref,
              dtb_ref, dtbT_ref, alog_ref, alogT_ref, dskip_ref, gssm_ref,
              y_ref, convn_ref, stn_ref,
              xbuf, xc, st, yacc, *, T):
    c = pl.program_id(1)
    nc = pl.num_programs(1)
    P, N = SSM_HEAD_DIM, SSM_STATE
    T2, PW = 2 * T, 2 * P
    tail = CONV_WIDTH - 1

    @pl.when(c == 0)
    def _():
        xbuf[0:8, :] = conv0_ref[0]
        st[...] = st0_ref[0]

    for q, xbc_ref in enumerate((xbc0_ref, xbc1_ref, xbc2_ref, xbc3_ref)):
        xbuf[8:8 + T, q * XBC_BLOCK:(q + 1) * XBC_BLOCK] = xbc_ref[...].astype(F32)
    acc = cb_ref[...] + cw_ref[0:1, :] * xbuf[8 - tail:8 - tail + T, :]
    for tap in range(1, CONV_WIDTH):
        acc = acc + cw_ref[tap:tap + 1, :] * xbuf[8 - tail + tap:8 - tail + tap + T, :]
    xc[...] = acc * _sigmoid(acc)
    last_rows = xbuf[T:T + 8, :]
    convn_ref[0] = last_rows
    xbuf[0:8, :] = last_rows

    iota = lambda shape, d: lax.broadcasted_iota(jnp.int32, shape, d)
    tri = iota((T, T), 1) <= iota((T, T), 0)
    dt = _softplus(dt_ref[:, 0:SSM_HEADS] + dtb_ref[...])
    cum = _dot_exact(tri.astype(F32), dt * (-jnp.exp(alog_ref[...])))

    half = lambda x, size: jnp.where(x >= size, 1, 0)
    ra, rb = iota((T2, T2), 0), iota((T2, T2), 1)
    same = half(ra, T) == half(rb, T)
    dtT = _softplus(dtT_ref[0] + dtbT_ref[...])
    dtaT = dtT * (-jnp.exp(alogT_ref[...]))
    cumT = _dot_exact(dtaT, jnp.where(same & (ra <= rb), 1.0, 0.0))
    lastT = _dot_exact(dtaT, jnp.where(same, 1.0, 0.0))
    same_p = half(iota((T2, PW), 0), T) == half(iota((T2, PW), 1), P)
    edec = jnp.exp(_dot_exact(dtaT, jnp.where(same_p, 1.0, 0.0)))
    wT = dtT * jnp.exp(lastT - cumT)

    left = iota((T, PW), 1) < P
    left2 = iota((T, T2), 1) < T
    tri2 = (iota((T, T2), 1) - T * half(iota((T, T2), 1), T)) <= iota((T, T2), 0)
    bc_off = SSM_INNER
    cc_off = SSM_INNER + SSM_GROUPS * N

    for g in range(SSM_GROUPS):
        bg = xc[:, bc_off + g * N:bc_off + (g + 1) * N]
        cg16 = xc[:, cc_off + g * N:cc_off + (g + 1) * N].astype(BF16)
        bg2 = jnp.concatenate([bg, bg], axis=0)
        cb2 = _dot_nt(cg16, bg2.astype(BF16))
        bgT2 = bg2.T
        gw = SSM_PPG * PW
        yoff = _dot(cg16, st[:, g * gw:(g + 1) * gw].astype(BF16))
        for k in range(SSM_PPG):
            q = g * SSM_PPG + k
            ps = slice(q * PW, (q + 1) * PW)
            b0 = jnp.broadcast_to(cum[:, 2 * q:2 * q + 1], (T, PW))
            b1 = jnp.broadcast_to(cum[:, 2 * q + 1:2 * q + 2], (T, PW))
            col_p = jnp.where(left, b0, b1)
            col_2 = col_p if T2 == PW else jnp.where(left2, b0[:, 0:T2], b1[:, 0:T2])
            dec = jnp.where(tri2, jnp.exp(col_2 - cumT[q:q + 1, :]), 0.0)
            m2 = (cb2 * dec * dtT[q:q + 1, :]).astype(BF16)
            xp = xc[:, ps]
            bd = jnp.concatenate([jnp.where(left, xp, 0.0), jnp.where(left, 0.0, xp)],
                                 axis=0).astype(BF16)
            yacc[:, ps] = _dot(m2, bd) + yoff[:, k * PW:(k + 1) * PW] * jnp.exp(col_p)
            btw = (bgT2 * wT[q:q + 1, :]).astype(BF16)
            st[:, ps] = st[:, ps] * edec[q:q + 1, :] + _dot(btw, bd)

    y = yacc[...] + xc[:, 0:SSM_INNER] * dskip_ref[...]
    z = z_ref[...].astype(F32)
    yg = y * (z * _sigmoid(z))
    y_ref[...] = (_rms_scale(yg) * gssm_ref[...]).astype(y_ref.dtype)

    @pl.when(c == nc - 1)
    def _():
        stn_ref[0] = st[...]


def ssd_mixer(u, dt_raw, conv0, st0, conv_w, conv_b, dt_bias, a_log, d_skip, g_ssm,
              *, row0, batch, nc, T):
    rows = batch * nc * T
    r0 = row0 // T
    nblk = batch * nc
    dtT = (dt_raw[row0:row0 + rows, :SSM_HEADS].reshape(nblk, T, SSM_PAIRS, 2)
           .transpose(0, 2, 3, 1).reshape(nblk, SSM_PAIRS, 2 * T))
    pair_rows = lambda v: jnp.repeat(v.astype(F32).reshape(SSM_PAIRS, 2), T, axis=1)
    row = lambda v: v.reshape(1, -1).astype(F32)
    const = lambda shape: pl.BlockSpec(shape, lambda b, c: (0,) * len(shape))
    xbc_specs = [pl.BlockSpec((T, XBC_BLOCK), functools.partial(
        lambda b, c, q: (r0 + b * nc + c, SSM_INNER // XBC_BLOCK + q), q=q)) for q in range(CONV_DIM // XBC_BLOCK)]
    return pl.pallas_call(
        functools.partial(_ssd_body, T=T),
        grid=(batch, nc),
        in_specs=[
            pl.BlockSpec((T, SSM_INNER), lambda b, c: (r0 + b * nc + c, 0)),
            *xbc_specs,
            pl.BlockSpec((T, DT_PAD), lambda b, c: (r0 + b * nc + c, 0)),
            pl.BlockSpec((1, SSM_PAIRS, 2 * T), lambda b, c: (b * nc + c, 0, 0)),
            pl.BlockSpec((1, 8, CONV_DIM), lambda b, c: (b, 0, 0)),
            pl.BlockSpec((1, SSM_STATE, SSM_INNER), lambda b, c: (b, 0, 0)),
            const((CONV_WIDTH, CONV_DIM)), const((1, CONV_DIM)),
            const((1, SSM_HEADS)), const((SSM_PAIRS, 2 * T)),
            const((1, SSM_HEADS)), const((SSM_PAIRS, 2 * T)),
            const((1, SSM_INNER)), const((1, SSM_INNER)),
        ],
        out_specs=[
            pl.BlockSpec((T, SSM_INNER), lambda b, c: (b * nc + c, 0)),
            pl.BlockSpec((1, 8, CONV_DIM), lambda b, c: (b, 0, 0)),
            pl.BlockSpec((1, SSM_STATE, SSM_INNER), lambda b, c: (b, 0, 0)),
        ],
        out_shape=[
            jax.ShapeDtypeStruct((rows, SSM_INNER), BF16),
            jax.ShapeDtypeStruct((batch, 8, CONV_DIM), F32),
            jax.ShapeDtypeStruct((batch, SSM_STATE, SSM_INNER), F32),
        ],
        scratch_shapes=[
            pltpu.VMEM((8 + T, CONV_DIM), F32),
            pltpu.VMEM((T, CONV_DIM), F32),
            pltpu.VMEM((SSM_STATE, SSM_INNER), F32),
            pltpu.VMEM((T, SSM_INNER), F32),
        ],
        compiler_params=_params("parallel", "arbitrary"),
        name="ssd_mixer",
    )(u, u, u, u, u, dt_raw, dtT, conv0, st0, conv_w.T.astype(F32), row(conv_b),
      row(dt_bias), pair_rows(dt_bias), row(a_log), pair_rows(a_log),
      row(jnp.repeat(d_skip, SSM_HEAD_DIM)), row(g_ssm))


def _band_body(q_ref, k_ref, v_ref, tab_ref, qg_ref, o_ref, bias_var, *, T, KB, pad, cps):
    c = pl.program_id(2)
    KA = BAND_CONST_KEYS
    d = B_HEAD_DIM

    @pl.when(c == 0)
    def _():
        for j in range(B_GQ):
            tj = tab_ref[0, j:j + 1, :]
            band = pltpu.roll(jnp.broadcast_to(tj, (T, 2 * d)), 2 * d - REL_CLIP, 1,
                              stride=1, stride_axis=0)
            bias_var[j * T:(j + 1) * T, :] = band[:, 0:KB - KA] - tj[:, 0:1]

    scale = 1.0 / math.sqrt(d)

    def chunk(cl, masked):
        start = pl.multiple_of((c * cps + cl) * T, T)
        k = k_ref[0, 0, pl.ds(start, KB), :]
        v = v_ref[0, 0, pl.ds(start, KB), :]
        q = jnp.concatenate(
            [(_rms_scale(q_ref[cl * T:(cl + 1) * T, j * d:(j + 1) * d].astype(F32)) * qg_ref[...] * scale
              ).astype(BF16) for j in range(B_GQ)], axis=0)
        s = _dot_nt(q, k)
        sa = s[:, 0:KA]
        sb = s[:, KA:KB] + bias_var[...]
        if masked:
            sa = jnp.where(start + lax.broadcasted_iota(jnp.int32, (1, KA), 1) >= pad, sa, -jnp.inf)
            sb = jnp.where(start + KA + lax.broadcasted_iota(jnp.int32, (1, KB - KA), 1) >= pad, sb, -jnp.inf)
        m = jnp.maximum(jnp.max(sa, axis=-1, keepdims=True), jnp.max(sb, axis=-1, keepdims=True))
        pa = jnp.exp(sa - m)
        pb = jnp.exp(sb - m)
        l = jnp.sum(pa, axis=-1, keepdims=True) + jnp.sum(pb, axis=-1, keepdims=True)
        o = (_dot(pa.astype(BF16), v[0:KA]) + _dot(pb.astype(BF16), v[KA:KB])) / l
        for j in range(B_GQ):
            o_ref[cl * T:(cl + 1) * T, j * d:(j + 1) * d] = o[j * T:(j + 1) * T].astype(o_ref.dtype)

    if pad:
        @pl.when(c * (cps * T) < pad)
        def _():
            for cl in range(cps):
                chunk(cl, True)

        @pl.when(c * (cps * T) >= pad)
        def _():
            for cl in range(cps):
                chunk(cl, False)
    else:
        for cl in range(cps):
            chunk(cl, False)


def band_attention(u, k, v, table, q_gain, *, row0, batch, nc, T, KB, pad, cps):
    rows = batch * nc * T
    r0 = row0 // (cps * T)
    ns = nc // cps
    ktot = k.shape[2]
    gw = B_GQ * B_HEAD_DIM
    tab = table[:, jnp.clip(2 * B_HEAD_DIM - jnp.arange(2 * B_HEAD_DIM), 0, 2 * REL_CLIP)].astype(F32)
    tab = tab.reshape(B_KV_HEADS, B_GQ, 2 * B_HEAD_DIM)
    return pl.pallas_call(
        functools.partial(_band_body, T=T, KB=KB, pad=pad, cps=cps),
        grid=(batch, B_KV_HEADS, ns),
        in_specs=[
            pl.BlockSpec((cps * T, gw), lambda b, g, c: (r0 + b * ns + c, g)),
            pl.BlockSpec((1, 1, ktot, B_HEAD_DIM), lambda b, g, c: (b, g, 0, 0)),
            pl.BlockSpec((1, 1, ktot, B_HEAD_DIM), lambda b, g, c: (b, g, 0, 0)),
            pl.BlockSpec((1, B_GQ, 2 * B_HEAD_DIM), lambda b, g, c: (g, 0, 0)),
            pl.BlockSpec((1, B_HEAD_DIM), lambda b, g, c: (0, 0)),
        ],
        out_specs=pl.BlockSpec((cps * T, gw), lambda b, g, c: (b * ns + c, g)),
        out_shape=jax.ShapeDtypeStruct((rows, B_Q_WIDTH), BF16),
        scratch_shapes=[pltpu.VMEM((B_GQ * T, KB - BAND_CONST_KEYS), F32)],
        compiler_params=_params("parallel", "parallel", "arbitrary"),
        name="band_attention",
    )(u, k, v, tab, q_gain.reshape(1, B_HEAD_DIM).astype(F32))


def _mem_body(q_ref, k_ref, v_ref, qg_ref, o_ref):
    scale = 1.0 / math.sqrt(MEM_HEAD_DIM)
    for h in range(MEM_HEADS):
        sl = slice(h * MEM_HEAD_DIM, (h + 1) * MEM_HEAD_DIM)
        qn = (_rms_scale(q_ref[:, sl].astype(F32)) * qg_ref[...] * scale).astype(BF16)
        s = _dot_nt(qn, k_ref[0, :, sl])
        m = jnp.max(s, axis=-1, keepdims=True)
        p = jnp.exp(s - m)
        l = jnp.sum(p, axis=-1, keepdims=True)
        o_ref[:, sl] = (_dot(p.astype(BF16), v_ref[0, :, sl]) / l).astype(o_ref.dtype)


def memory_attention(u, k, v, q_gain, *, q_col_block, row0, batch, nt, tq):
    rows = batch * nt * tq
    r0 = row0 // tq
    return pl.pallas_call(
        _mem_body,
        grid=(batch, nt),
        in_specs=[
            pl.BlockSpec((tq, MEM_WIDTH), lambda b, i: (r0 + b * nt + i, q_col_block)),
            pl.BlockSpec((1, MEM_TOKENS, MEM_WIDTH), lambda b, i: (b, 0, 0)),
            pl.BlockSpec((1, MEM_TOKENS, MEM_WIDTH), lambda b, i: (b, 0, 0)),
            pl.BlockSpec((1, MEM_HEAD_DIM), lambda b, i: (0, 0)),
        ],
        out_specs=pl.BlockSpec((tq, MEM_WIDTH), lambda b, i: (b * nt + i, 0)),
        out_shape=jax.ShapeDtypeStruct((rows, MEM_WIDTH), BF16),
        compiler_params=_params("parallel", "arbitrary"),
        name="memory_attention",
    )(u, k, v, q_gain.reshape(1, MEM_HEAD_DIM).astype(F32))


def _heads_major(x, n_heads):
    b, l, w = x.shape
    return x.reshape(b, l, n_heads, w // n_heads).transpose(0, 2, 1, 3)


def _ffn_act(h, g_ffn, w_gate, w_up, layer):
    (hn,) = rmsnorm(h, g_ffn.reshape(1, -1))
    return matmul_wres(hn, [w_gate, w_up], layer, D_FF_PAD, BF16, WRES_ROW_TILE, 512, _silu_gate)


def kernel(x_prompt, x_sample, mem_prompt, state_ssm, state_conv, cache_kv_k, cache_kv_v, cache_mem_k, cache_mem_v, g_mix, w_in_a, conv_w, conv_b, dt_bias, a_log, d_skip, g_ssm, w_in_b, rel_bias, q_norm_b, g_kv, w_kv, k_norm_kv, g_mem, w_mem_kv, q_norm_mem, k_norm_mem, w_out, g_ffn, w_ffn_gate, w_ffn_up, w_ffn_down):
    S, NS = SEQ, N_SAMPLE_ROWS
    bf = lambda w: w.astype(BF16)

    main_w = SSM_INNER + CONV_DIM
    w_in_a_t = jnp.swapaxes(w_in_a, 1, 2)
    in_a = functools.partial(matmul_wres, ws=[w_in_a_t], layer=0, tm=WRES_ROW_TILE, transposed=True)
    wout = cast_rows(w_out, w_out.shape[1])
    wd = cast_rows(w_ffn_down, D_FF_PAD)

    xp = x_prompt.reshape(S, D_MODEL)
    xs = x_sample.reshape(NS, D_MODEL)

    mem_k32, mem_v32, mem_k16, mem_v16 = [], [], [], []
    mem_x = mem_prompt.reshape(MEM_TOKENS, D_MODEL)
    for l in range(2):
        (mn,) = rmsnorm(mem_x, g_mem[l].reshape(1, -1))
        kv = matmul_wres(mn, [w_mem_kv], l, 2 * MEM_WIDTH, F32, MEM_TOKENS, 1024)
        k32, k16, v16 = kv_post(kv, k_norm_mem[l], MEM_HEAD_DIM)
        mem_k32.append(k32.reshape(1, MEM_TOKENS, MEM_HEADS, MEM_HEAD_DIM))
        mem_v32.append(kv[:, MEM_WIDTH:].reshape(1, MEM_TOKENS, MEM_HEADS, MEM_HEAD_DIM))
        mem_k16.append(k16.reshape(1, MEM_TOKENS, MEM_WIDTH))
        mem_v16.append(v16.reshape(1, MEM_TOKENS, MEM_WIDTH))
    cmk = [bf(cache_mem_k[l]).reshape(DEC_BATCH, MEM_TOKENS, MEM_WIDTH) for l in range(2)]
    cmv = [bf(cache_mem_v[l]).reshape(DEC_BATCH, MEM_TOKENS, MEM_WIDTH) for l in range(2)]

    def mem_attn(u, q_col_block, l):
        attn = functools.partial(memory_attention, u, q_gain=q_norm_mem[l], q_col_block=q_col_block)
        yp = attn(k=mem_k16[l], v=mem_v16[l], row0=0, batch=1, nt=S // 512, tq=512)
        ys = attn(k=cmk[l], v=cmv[l], row0=S, batch=DEC_BATCH, nt=1, tq=DEC_SEQ)
        return yp, ys

    hn = rmsnorm_rows2(xp, xs, g_mix[0])
    u = in_a(hn, n_out=main_w, out_dtype=BF16, tn=1024)
    dt_raw = in_a(hn, n_out=DT_PAD, out_dtype=F32, tn=DT_PAD, col0=main_w)
    q_mem = in_a(hn, n_out=MEM_WIDTH, out_dtype=BF16, tn=1024, col0=main_w + SSM_HEADS)

    def to_state_t(s):
        b = s.shape[0]
        return s.reshape(b, SSM_INNER, SSM_STATE).transpose(0, 2, 1)

    def from_state_t(s):
        b = s.shape[0]
        return s.transpose(0, 2, 1).reshape(b, SSM_HEADS, SSM_HEAD_DIM, SSM_STATE)

    ssd = functools.partial(ssd_mixer, u, dt_raw, conv_w=conv_w[0], conv_b=conv_b[0], dt_bias=dt_bias[0],
                            a_log=a_log[0], d_skip=d_skip[0], g_ssm=g_ssm[0])
    tail = CONV_WIDTH - 1
    conv0_p = jnp.zeros((1, 8, CONV_DIM), F32)
    st0_p = jnp.zeros((1, SSM_STATE, SSM_INNER), F32)
    y_p, conv_p, st_p = ssd(conv0=conv0_p, st0=st0_p, row0=0, batch=1, nc=S // CHUNK, T=CHUNK)
    conv0_s = jnp.pad(state_conv[0], ((0, 0), (8 - tail, 0), (0, 0)))
    y_s, conv_s, st_s = ssd(conv0=conv0_s, st0=to_state_t(state_ssm[0]), row0=S, batch=DEC_BATCH, nc=1, T=DEC_SEQ)
    ym_p, ym_s = mem_attn(q_mem, 0, 0)
    h = matmul_out(y_p, y_s, ym_p, ym_s, wout, 0, xp, xs, 0, 1024, 2048)
    h = ffn_down(_ffn_act(h, g_ffn[0], w_ffn_gate, w_ffn_up, 0), wd, 0, h, row0=0, rows=N_ROWS,
                 tm=ROW_TILE, tn=1024, tk=2816)

    hkv, hn = rmsnorm(h, jnp.stack([g_kv, g_mix[1]]))
    kvw = B_KV_HEADS * B_HEAD_DIM
    kv = matmul_wres(hkv, [w_kv[None]], 0, 2 * kvw, F32, WRES_ROW_TILE, 1024)
    k32, k16, v16 = kv_post(kv, k_norm_kv, B_HEAD_DIM)
    u = matmul_wres(hn, [w_in_b], 0, B_Q_WIDTH + MEM_WIDTH, BF16, WRES_ROW_TILE, 1024)

    kp = jnp.pad(_heads_major(k16[:S].reshape(1, S, kvw), B_KV_HEADS), ((0, 0), (0, 0), (BAND_PAST, 0), (0, 0)))
    vp = jnp.pad(_heads_major(v16[:S].reshape(1, S, kvw), B_KV_HEADS), ((0, 0), (0, 0), (BAND_PAST, 0), (0, 0)))
    yb_p = band_attention(u, kp, vp, rel_bias[0], q_norm_b[0], row0=0, batch=1, nc=S // CHUNK, T=CHUNK,
                          KB=BAND_PAST + CHUNK, pad=BAND_PAST, cps=BAND_CHUNKS_PER_STEP)

    wc = cache_kv_k.shape[1]
    ks = jnp.concatenate([bf(cache_kv_k).reshape(DEC_BATCH, wc, kvw), k16[S:].reshape(DEC_BATCH, DEC_SEQ, kvw)], axis=1)
    vs = jnp.concatenate([bf(cache_kv_v).reshape(DEC_BATCH, wc, kvw), v16[S:].reshape(DEC_BATCH, DEC_SEQ, kvw)], axis=1)
    yb_s = band_attention(u, _heads_major(ks, B_KV_HEADS), _heads_major(vs, B_KV_HEADS), rel_bias[0], q_norm_b[0],
                          row0=S, batch=DEC_BATCH, nc=1, T=DEC_SEQ, KB=wc + DEC_SEQ, pad=0, cps=1)
    ym_p, ym_s = mem_attn(u, B_Q_WIDTH // MEM_WIDTH, 1)
    h = matmul_out(yb_p, yb_s, ym_p, ym_s, wout, 1, h, h, N_ROWS // SUB_ROWS - 1, 1024, 2048)
    act = _ffn_act(h, g_ffn[1], w_ffn_gate, w_ffn_up, 1)
    y_prompt = ffn_down(act, wd, 1, h, row0=0, rows=S, tm=1024, tn=1024, tk=2816)
    y_sample = ffn_down(act, wd, 1, h, row0=S, rows=NS, tm=NS, tn=1024, tk=2816)

    keep = min(BAND_PAST, S)
    kv_shape = (B_KV_HEADS, B_HEAD_DIM)
    return (
        y_prompt.reshape(1, S, D_MODEL),
        y_sample.reshape(DEC_BATCH, DEC_SEQ, D_MODEL),
        from_state_t(st_p)[None],
        conv_p[:, 8 - tail:][None],
        k32[S - keep:S].reshape(1, keep, *kv_shape),
        kv[S - keep:S, kvw:].reshape(1, keep, *kv_shape),
        jnp.stack(mem_k32),
        jnp.stack(mem_v32),
        from_state_t(st_s)[None],
        conv_s[:, 8 - tail:][None],
        k32[S:].reshape(DEC_BATCH, DEC_SEQ, *kv_shape),
        kv[S:, kvw:].reshape(DEC_BATCH, DEC_SEQ, *kv_shape),
    )
```

```python
import functools
import math

import jax
import jax.numpy as jnp
from jax import lax
from jax.experimental import pallas as pl
from jax.experimental.pallas import tpu as pltpu

F32 = jnp.float32
BF16 = jnp.bfloat16

D_MODEL = 4096
SEQ = 8192
DEC_BATCH = 8
DEC_SEQ = 32
PAST_LEN = 2048
CHUNK = 64
RMS_EPS = 1e-6

SSM_HEAD_DIM = 64
SSM_INNER = 6144
SSM_HEADS = 96
SSM_PAIRS = SSM_HEADS // 2
SSM_GROUPS = 8
SSM_PPG = SSM_PAIRS // SSM_GROUPS
SSM_STATE = 128
CONV_WIDTH = 4
CONV_DIM = 8192
XBC_BLOCK = 2048

B_HEAD_DIM = 128
B_Q_WIDTH = 6144
B_HEADS = 48
B_KV_HEADS = 8
B_GQ = 6
BAND_PAST = 512
REL_CLIP = 64
BAND_CONST_KEYS = 384
BAND_CHUNKS_PER_STEP = 4

MEM_TOKENS = 256
MEM_WIDTH = 2048
MEM_HEADS = 4
MEM_HEAD_DIM = 512

D_FF = 11008
D_FF_PAD = 11264
DT_PAD = 128

N_SAMPLE_ROWS = DEC_BATCH * DEC_SEQ
N_ROWS = SEQ + N_SAMPLE_ROWS
ROW_TILE = 768
WRES_ROW_TILE = 384
SUB_ROWS = 256
SUBS_PER_TILE = ROW_TILE // SUB_ROWS

V7X_VMEM_LIMIT_BYTES = 56 * 1024 * 1024


def _params(*sem):
    return pltpu.CompilerParams(dimension_semantics=sem, vmem_limit_bytes=V7X_VMEM_LIMIT_BYTES)


def _sigmoid(x):
    return 1.0 / (1.0 + jnp.exp(-x))


def _softplus(x):
    return jnp.maximum(x, 0.0) + jnp.log1p(jnp.exp(-jnp.abs(x)))


def _dot(a, b):
    return jnp.dot(a, b, preferred_element_type=F32)


def _dot_nt(a, b):
    return lax.dot_general(a, b, (((1,), (1,)), ((), ())), preferred_element_type=F32)


def _dot_exact(a, b):
    return jnp.dot(a, b, precision=lax.Precision.HIGHEST, preferred_element_type=F32)


def _rms_scale(x):
    return x * lax.rsqrt(jnp.mean(x * x, axis=-1, keepdims=True) + RMS_EPS)


def _rmsnorm_body(x_ref, g_ref, *o_refs):
    xn = _rms_scale(x_ref[...])
    for j, o_ref in enumerate(o_refs):
        o_ref[...] = (xn * g_ref[j:j + 1, :]).astype(o_ref.dtype)


def rmsnorm(x, gains, tm=256):
    m, d = x.shape
    n = gains.shape[0]
    return pl.pallas_call(
        _rmsnorm_body,
        grid=(m // tm,),
        in_specs=[pl.BlockSpec((tm, d), lambda i: (i, 0)),
                  pl.BlockSpec((n, d), lambda i: (0, 0))],
        out_specs=[pl.BlockSpec((tm, d), lambda i: (i, 0))] * n,
        out_shape=[jax.ShapeDtypeStruct((m, d), BF16)] * n,
        compiler_params=_params("parallel"),
        name="rmsnorm",
    )(x, gains)


def _rmsnorm2_body(xp_ref, xs_ref, g_ref, o_ref, *, n_p):
    i = pl.program_id(0)

    @pl.when(i < n_p)
    def _():
        o_ref[...] = (_rms_scale(xp_ref[...]) * g_ref[...]).astype(o_ref.dtype)

    @pl.when(i >= n_p)
    def _():
        o_ref[...] = (_rms_scale(xs_ref[...]) * g_ref[...]).astype(o_ref.dtype)


def rmsnorm_rows2(xp, xs, gain):
    tm = SUB_ROWS
    d = xp.shape[1]
    n_p, n_s = xp.shape[0] // tm, xs.shape[0] // tm
    return pl.pallas_call(
        functools.partial(_rmsnorm2_body, n_p=n_p),
        grid=(n_p + n_s,),
        in_specs=[pl.BlockSpec((tm, d), lambda i: (jnp.minimum(i, n_p - 1), 0)),
                  pl.BlockSpec((tm, d), lambda i: (jnp.maximum(i - n_p, 0), 0)),
                  pl.BlockSpec((1, d), lambda i: (0, 0))],
        out_specs=pl.BlockSpec((tm, d), lambda i: (i, 0)),
        out_shape=jax.ShapeDtypeStruct((xp.shape[0] + xs.shape[0], d), BF16),
        compiler_params=_params("parallel"),
        name="rmsnorm_rows2",
    )(xp, xs, gain.reshape(1, d))


def _wres_body(x_ref, *refs, n_w, layer, col0, n_valid, tn, transposed, epilogue, side):
    w_hbm, refs = refs[:n_w], refs[n_w:]
    if side:
        side_w, o_ref, side_o, refs = refs[0], refs[1], refs[2], refs[3:]
    else:
        o_ref, refs = refs[0], refs[1:]
    wbf, wf32, sem = refs[:n_w], refs[n_w:2 * n_w], refs[2 * n_w]
    j, i = pl.program_id(0), pl.program_id(1)
    nj = pl.num_programs(0)
    rem = n_valid % tn

    if side:
        rows_valid, rb, n_blocks = side
        blk = jnp.minimum(j * pl.num_programs(1) + i, n_blocks - 1)
        row = blk * rb + lax.broadcasted_iota(jnp.int32, side_w.shape, 0)
        side_o[...] = jnp.where(row < rows_valid, side_w[...], 0.0).astype(side_o.dtype)

    def for_each_copy(jj, slot, fn):
        def go(width):
            for w in range(n_w):
                if transposed:
                    src = w_hbm[w].at[layer, pl.ds(col0 + jj * tn, width), :]
                    dst = wf32[w].at[slot, pl.ds(0, width), :]
                else:
                    src = w_hbm[w].at[layer, :, pl.ds(col0 + jj * tn, width)]
                    dst = wf32[w].at[slot, :, pl.ds(0, width)]
                fn(pltpu.make_async_copy(src, dst, sem.at[w, slot]))
        if rem:
            pl.when(jj < nj - 1)(lambda: go(tn))
            pl.when(jj == nj - 1)(lambda: go(rem))
        else:
            go(tn)

    @pl.when(i == 0)
    def _():
        slot = j % 2

        start = lambda cp: cp.start(priority=1)

        @pl.when(j == 0)
        def _():
            for_each_copy(0, 0, start)

        for_each_copy(j, slot, lambda cp: cp.wait())

        @pl.when(j + 1 < nj)
        def _():
            for_each_copy(j + 1, 1 - slot, start)

        for w in range(n_w):
            wt = wf32[w][slot]
            if rem:
                col = j * tn + lax.broadcasted_iota(jnp.int32, wt.shape, 0 if transposed else 1)
                wt = jnp.where(col < n_valid, wt, 0.0)
            wbf[w][...] = (wt.T if transposed else wt).astype(BF16)

    x = x_ref[...]
    o_ref[...] = epilogue(*[_dot(x, wb[...]) for wb in wbf]).astype(o_ref.dtype)


def matmul_wres(x, ws, layer, n_out, out_dtype, tm, tn, epilogue=lambda acc: acc, transposed=False, col0=0,
                side_cast=None):
    m, k = x.shape
    n_w = len(ws)
    n_valid = min(n_out, ws[0].shape[1 if transposed else 2] - col0)
    nj, ni = n_out // tn, m // tm
    in_specs = [pl.BlockSpec((tm, k), lambda j, i: (i, 0))] + [pl.BlockSpec(memory_space=pl.ANY)] * n_w
    out_specs = [pl.BlockSpec((tm, tn), lambda j, i: (i, j))]
    out_shape = [jax.ShapeDtypeStruct((m, n_out), out_dtype)]
    args = [x, *ws]
    side = None
    if side_cast:
        w2, layer2, rows_out, rb = side_cast
        rows_valid, n2 = w2.shape[1], w2.shape[2]
        n_blocks, n_src_blocks = rows_out // rb, rows_valid // rb
        assert rows_out % rb == 0 and rows_valid % rb == 0 and n_blocks <= nj * ni
        side = (rows_valid, rb, n_blocks)
        in_specs.append(pl.BlockSpec((None, rb, n2), lambda j, i: (layer2, jnp.minimum(j * ni + i, n_src_blocks - 1), 0)))
        out_specs.append(pl.BlockSpec((rb, n2), lambda j, i: (jnp.minimum(j * ni + i, n_blocks - 1), 0)))
        out_shape.append(jax.ShapeDtypeStruct((rows_out, n2), BF16))
        args.append(w2)
    outs = pl.pallas_call(
        functools.partial(_wres_body, n_w=n_w, layer=layer, col0=col0, n_valid=n_valid, tn=tn,
                          transposed=transposed, epilogue=epilogue, side=side),
        grid=(nj, ni),
        in_specs=in_specs,
        out_specs=out_specs,
        out_shape=out_shape,
        scratch_shapes=[pltpu.VMEM((k, tn), BF16)] * n_w
                       + [pltpu.VMEM((2, tn, k) if transposed else (2, k, tn), F32)] * n_w
                       + [pltpu.SemaphoreType.DMA((n_w, 2))],
        compiler_params=_params("arbitrary", "arbitrary"),
        name="matmul_wres",
    )(*args)
    return outs if side_cast else outs[0]


def _mm_out_body(xap_ref, xas_ref, xbp_ref, xbs_ref, w_ref, rp_ref, rs_ref, o_ref, acc_ref, *, n_a):
    i, k = pl.program_id(0), pl.program_id(2)
    last_tile = i == pl.num_programs(0) - 1
    head = (SUBS_PER_TILE - 1) * SUB_ROWS

    @pl.when(k == 0)
    def _():
        acc_ref[0:head, :] = rp_ref[0:SUBS_PER_TILE - 1].reshape(head, -1)

        @pl.when(jnp.logical_not(last_tile))
        def _():
            acc_ref[head:, :] = rp_ref[SUBS_PER_TILE - 1]

        @pl.when(last_tile)
        def _():
            acc_ref[head:, :] = rs_ref[0]

    def accumulate(p_ref, s_ref):
        @pl.when(jnp.logical_not(last_tile))
        def _():
            acc_ref[...] += _dot(p_ref[...].reshape(ROW_TILE, -1), w_ref[...])

        @pl.when(last_tile)
        def _():
            acc_ref[0:head, :] += _dot(p_ref[0:SUBS_PER_TILE - 1].reshape(head, -1), w_ref[...])
            acc_ref[head:, :] += _dot(s_ref[0], w_ref[...])

    @pl.when(k < n_a)
    def _():
        accumulate(xap_ref, xas_ref)

    @pl.when(k >= n_a)
    def _():
        accumulate(xbp_ref, xbs_ref)

    @pl.when(k == pl.num_programs(2) - 1)
    def _():
        o_ref[...] = acc_ref[...]


def matmul_out(xa_p, xa_s, xb_p, xb_s, w, r_p, r_s, r_s_block, tn, tk):
    sub3 = lambda a: a.reshape(a.shape[0] // SUB_ROWS, SUB_ROWS, a.shape[1])
    ka, kb, n = xa_p.shape[1], xb_p.shape[1], w.shape[1]
    n_a, n_b = ka // tk, kb // tk
    ka_idx = lambda k: jnp.minimum(k, n_a - 1)
    kb_idx = lambda k: jnp.maximum(k - n_a, 0)
    nt = N_ROWS // ROW_TILE
    spt = SUBS_PER_TILE
    return pl.pallas_call(
        functools.partial(_mm_out_body, n_a=n_a),
        grid=(nt, n // tn, n_a + n_b),
        in_specs=[pl.BlockSpec((spt, SUB_ROWS, tk), lambda i, j, k: (i, 0, ka_idx(k))),
                  pl.BlockSpec((1, SUB_ROWS, tk), lambda i, j, k: (0, 0, ka_idx(k))),
                  pl.BlockSpec((spt, SUB_ROWS, tk), lambda i, j, k: (i, 0, kb_idx(k))),
                  pl.BlockSpec((1, SUB_ROWS, tk), lambda i, j, k: (0, 0, kb_idx(k))),
                  pl.BlockSpec((tk, tn), lambda i, j, k: (k, j)),
                  pl.BlockSpec((spt, SUB_ROWS, tn), lambda i, j, k: (i, 0, j)),
                  pl.BlockSpec((1, SUB_ROWS, tn), lambda i, j, k: (r_s_block, 0, j))],
        out_specs=pl.BlockSpec((ROW_TILE, tn), lambda i, j, k: (i, j)),
        out_shape=jax.ShapeDtypeStruct((N_ROWS, n), F32),
        scratch_shapes=[pltpu.VMEM((ROW_TILE, tn), F32)],
        compiler_params=_params("parallel", "arbitrary", "arbitrary"),
        name="matmul_out",
    )(sub3(xa_p), sub3(xa_s), sub3(xb_p), sub3(xb_s), w, sub3(r_p), sub3(r_s))


def _silu_gate(g, u):
    return g * _sigmoid(g) * u


def _ffn_down_body(x_ref, w_ref, r_ref, o_ref, acc_ref):
    k = pl.program_id(2)

    @pl.when(k == 0)
    def _():
        acc_ref[...] = r_ref[...]

    acc_ref[...] += _dot(x_ref[...], w_ref[...])

    @pl.when(k == pl.num_programs(2) - 1)
    def _():
        o_ref[...] = acc_ref[...]


def ffn_down(x, w, res, *, row0, rows, tm, tn, tk):
    kk = x.shape[1]
    n = w.shape[1]
    r0 = row0 // tm
    return pl.pallas_call(
        _ffn_down_body,
        grid=(rows // tm, n // tn, kk // tk),
        in_specs=[pl.BlockSpec((tm, tk), lambda i, j, k: (r0 + i, k)),
                  pl.BlockSpec((tk, tn), lambda i, j, k: (k, j)),
                  pl.BlockSpec((tm, tn), lambda i, j, k: (r0 + i, j))],
        out_specs=pl.BlockSpec((tm, tn), lambda i, j, k: (i, j)),
        out_shape=jax.ShapeDtypeStruct((rows, n), F32),
        scratch_shapes=[pltpu.VMEM((tm, tn), F32)],
        compiler_params=_params("parallel", "arbitrary", "arbitrary"),
        name="ffn_down",
    )(x, w, res)


def _kv_post_body(kv_ref, g_ref, k32_ref, k16_ref, v16_ref, *, width, head_dim):
    for h in range(width // head_dim):
        sl = slice(h * head_dim, (h + 1) * head_dim)
        kn = _rms_scale(kv_ref[:, sl]) * g_ref[...]
        k32_ref[:, sl] = kn
        k16_ref[:, sl] = kn.astype(BF16)
    v16_ref[...] = kv_ref[:, width:].astype(BF16)


def kv_post(kv, gain, head_dim, tm=256):
    m, w2 = kv.shape
    width = w2 // 2
    return pl.pallas_call(
        functools.partial(_kv_post_body, width=width, head_dim=head_dim),
        grid=(m // tm,),
        in_specs=[pl.BlockSpec((tm, w2), lambda i: (i, 0)),
                  pl.BlockSpec((1, head_dim), lambda i: (0, 0))],
        out_specs=[pl.BlockSpec((tm, width), lambda i: (i, 0))] * 3,
        out_shape=[jax.ShapeDtypeStruct((m, width), F32),
                   jax.ShapeDtypeStruct((m, width), BF16),
                   jax.ShapeDtypeStruct((m, width), BF16)],
        compiler_params=_params("parallel"),
        name="kv_post",
    )(kv, gain.reshape(1, head_dim))


def _ssd_body(z_ref, xbc0_ref, xbc1_ref, xbc2_ref, xbc3_ref, dt_ref, dtT_ref, conv0_ref, st0_ref, cw_ref, cb_ref,
              dtb_ref, dtbT_ref, alog_ref, alogT_ref, dskip_ref, gssm_ref,
              y_ref, convn_ref, stn_ref,
              xbuf, xc, st, yacc, *, T):
    c = pl.program_id(1)
    nc = pl.num_programs(1)
    P, N = SSM_HEAD_DIM, SSM_STATE
    T2, PW = 2 * T, 2 * P
    tail = CONV_WIDTH - 1

    @pl.when(c == 0)
    def _():
        xbuf[0:8, :] = conv0_ref[0]
        st[...] = st0_ref[0]

    for q, xbc_ref in enumerate((xbc0_ref, xbc1_ref, xbc2_ref, xbc3_ref)):
        xbuf[8:8 + T, q * XBC_BLOCK:(q + 1) * XBC_BLOCK] = xbc_ref[...].astype(F32)
    acc = cb_ref[...] + cw_ref[0:1, :] * xbuf[8 - tail:8 - tail + T, :]
    for tap in range(1, CONV_WIDTH):
        acc = acc + cw_ref[tap:tap + 1, :] * xbuf[8 - tail + tap:8 - tail + tap + T, :]
    xc[...] = acc * _sigmoid(acc)
    last_rows = xbuf[T:T + 8, :]
    convn_ref[0] = last_rows
    xbuf[0:8, :] = last_rows

    iota = lambda shape, d: lax.broadcasted_iota(jnp.int32, shape, d)
    tri = iota((T, T), 1) <= iota((T, T), 0)
    dt = _softplus(dt_ref[:, 0:SSM_HEADS] + dtb_ref[...])
    cum = _dot_exact(tri.astype(F32), dt * (-jnp.exp(alog_ref[...])))

    half = lambda x, size: jnp.where(x >= size, 1, 0)
    ra, rb = iota((T2, T2), 0), iota((T2, T2), 1)
    same = half(ra, T) == half(rb, T)
    dtT = _softplus(dtT_ref[0] + dtbT_ref[...])
    dtaT = dtT * (-jnp.exp(alogT_ref[...]))
    cumT = _dot_exact(dtaT, jnp.where(same & (ra <= rb), 1.0, 0.0))
    lastT = _dot_exact(dtaT, jnp.where(same, 1.0, 0.0))
    same_p = half(iota((T2, PW), 0), T) == half(iota((T2, PW), 1), P)
    edec = jnp.exp(_dot_exact(dtaT, jnp.where(same_p, 1.0, 0.0)))
    wT = dtT * jnp.exp(lastT - cumT)

    left = iota((T, PW), 1) < P
    left2 = iota((T, T2), 1) < T
    tri2 = (iota((T, T2), 1) - T * half(iota((T, T2), 1), T)) <= iota((T, T2), 0)
    bc_off = SSM_INNER
    cc_off = SSM_INNER + SSM_GROUPS * N

    for g in range(SSM_GROUPS):
        bg = xc[:, bc_off + g * N:bc_off + (g + 1) * N]
        cg16 = xc[:, cc_off + g * N:cc_off + (g + 1) * N].astype(BF16)
        bg2 = jnp.concatenate([bg, bg], axis=0)
        cb2 = _dot_nt(cg16, bg2.astype(BF16))
        bgT2 = bg2.T
        gw = SSM_PPG * PW
        yoff = _dot(cg16, st[:, g * gw:(g + 1) * gw].astype(BF16))
        for k in range(SSM_PPG):
            q = g * SSM_PPG + k
            ps = slice(q * PW, (q + 1) * PW)
            b0 = jnp.broadcast_to(cum[:, 2 * q:2 * q + 1], (T, PW))
            b1 = jnp.broadcast_to(cum[:, 2 * q + 1:2 * q + 2], (T, PW))
            col_p = jnp.where(left, b0, b1)
            col_2 = col_p if T2 == PW else jnp.where(left2, b0[:, 0:T2], b1[:, 0:T2])
            dec = jnp.where(tri2, jnp.exp(col_2 - cumT[q:q + 1, :]), 0.0)
            m2 = (cb2 * dec * dtT[q:q + 1, :]).astype(BF16)
            xp = xc[:, ps]
            bd = jnp.concatenate([jnp.where(left, xp, 0.0), jnp.where(left, 0.0, xp)],
                                 axis=0).astype(BF16)
            yacc[:, ps] = _dot(m2, bd) + yoff[:, k * PW:(k + 1) * PW] * jnp.exp(col_p)
            btw = (bgT2 * wT[q:q + 1, :]).astype(BF16)
            st[:, ps] = st[:, ps] * edec[q:q + 1, :] + _dot(btw, bd)

    y = yacc[...] + xc[:, 0:SSM_INNER] * dskip_ref[...]
    z = z_ref[...].astype(F32)
    yg = y * (z * _sigmoid(z))
    y_ref[...] = (_rms_scale(yg) * gssm_ref[...]).astype(y_ref.dtype)

    @pl.when(c == nc - 1)
    def _():
        stn_ref[0] = st[...]


def ssd_mixer(u, dt_raw, conv0, st0, conv_w, conv_b, dt_bias, a_log, d_skip, g_ssm,
              *, row0, batch, nc, T):
    rows = batch * nc * T
    r0 = row0 // T
    nblk = batch * nc
    dtT = (dt_raw[row0:row0 + rows, :SSM_HEADS].reshape(nblk, T, SSM_PAIRS, 2)
           .transpose(0, 2, 3, 1).reshape(nblk, SSM_PAIRS, 2 * T))
    pair_rows = lambda v: jnp.repeat(v.astype(F32).reshape(SSM_PAIRS, 2), T, axis=1)
    row = lambda v: v.reshape(1, -1).astype(F32)
    const = lambda shape: pl.BlockSpec(shape, lambda b, c: (0,) * len(shape))
    xbc_specs = [pl.BlockSpec((T, XBC_BLOCK), functools.partial(
        lambda b, c, q: (r0 + b * nc + c, SSM_INNER // XBC_BLOCK + q), q=q)) for q in range(CONV_DIM // XBC_BLOCK)]
    return pl.pallas_call(
        functools.partial(_ssd_body, T=T),
        grid=(batch, nc),
        in_specs=[
            pl.BlockSpec((T, SSM_INNER), lambda b, c: (r0 + b * nc + c, 0)),
            *xbc_specs,
            pl.BlockSpec((T, DT_PAD), lambda b, c: (r0 + b * nc + c, 0)),
            pl.BlockSpec((1, SSM_PAIRS, 2 * T), lambda b, c: (b * nc + c, 0, 0)),
            pl.BlockSpec((1, 8, CONV_DIM), lambda b, c: (b, 0, 0)),
            pl.BlockSpec((1, SSM_STATE, SSM_INNER), lambda b, c: (b, 0, 0)),
            const((CONV_WIDTH, CONV_DIM)), const((1, CONV_DIM)),
            const((1, SSM_HEADS)), const((SSM_PAIRS, 2 * T)),
            const((1, SSM_HEADS)), const((SSM_PAIRS, 2 * T)),
            const((1, SSM_INNER)), const((1, SSM_INNER)),
        ],
        out_specs=[
            pl.BlockSpec((T, SSM_INNER), lambda b, c: (b * nc + c, 0)),
            pl.BlockSpec((1, 8, CONV_DIM), lambda b, c: (b, 0, 0)),
            pl.BlockSpec((1, SSM_STATE, SSM_INNER), lambda b, c: (b, 0, 0)),
        ],
        out_shape=[
            jax.ShapeDtypeStruct((rows, SSM_INNER), BF16),
            jax.ShapeDtypeStruct((batch, 8, CONV_DIM), F32),
            jax.ShapeDtypeStruct((batch, SSM_STATE, SSM_INNER), F32),
        ],
        scratch_shapes=[
            pltpu.VMEM((8 + T, CONV_DIM), F32),
            pltpu.VMEM((T, CONV_DIM), F32),
            pltpu.VMEM((SSM_STATE, SSM_INNER), F32),
            pltpu.VMEM((T, SSM_INNER), F32),
        ],
        compiler_params=_params("parallel", "arbitrary"),
        name="ssd_mixer",
    )(u, u, u, u, u, dt_raw, dtT, conv0, st0, conv_w.T.astype(F32), row(conv_b),
      row(dt_bias), pair_rows(dt_bias), row(a_log), pair_rows(a_log),
      row(jnp.repeat(d_skip, SSM_HEAD_DIM)), row(g_ssm))


def _band_body(q_ref, k_ref, v_ref, tab_ref, qg_ref, o_ref, bias_var, *, T, KB, pad, cps):
    c = pl.program_id(2)
    KA = BAND_CONST_KEYS
    d = B_HEAD_DIM

    @pl.when(c == 0)
    def _():
        for j in range(B_GQ):
            tj = tab_ref[0, j:j + 1, :]
            band = pltpu.roll(jnp.broadcast_to(tj, (T, 2 * d)), 2 * d - REL_CLIP, 1,
                              stride=1, stride_axis=0)
            bias_var[j * T:(j + 1) * T, :] = band[:, 0:KB - KA] - tj[:, 0:1]

    scale = 1.0 / math.sqrt(d)

    def chunk(cl, masked):
        start = pl.multiple_of((c * cps + cl) * T, T)
        k = k_ref[0, 0, pl.ds(start, KB), :]
        v = v_ref[0, 0, pl.ds(start, KB), :]
        q = jnp.concatenate(
            [(_rms_scale(q_ref[cl * T:(cl + 1) * T, j * d:(j + 1) * d].astype(F32)) * qg_ref[...] * scale
              ).astype(BF16) for j in range(B_GQ)], axis=0)
        s = _dot_nt(q, k)
        sa = s[:, 0:KA]
        sb = s[:, KA:KB] + bias_var[...]
        if masked:
            sa = jnp.where(start + lax.broadcasted_iota(jnp.int32, (1, KA), 1) >= pad, sa, -jnp.inf)
            sb = jnp.where(start + KA + lax.broadcasted_iota(jnp.int32, (1, KB - KA), 1) >= pad, sb, -jnp.inf)
        m = jnp.maximum(jnp.max(sa, axis=-1, keepdims=True), jnp.max(sb, axis=-1, keepdims=True))
        pa = jnp.exp((sa - m).astype(BF16))
        pb = jnp.exp((sb - m).astype(BF16))
        v1 = jnp.concatenate([v, jnp.ones_like(v)], axis=1)
        ol = _dot(pa, v1[0:KA]) + _dot(pb, v1[KA:KB])
        o = ol[:, 0:d] / ol[:, d:2 * d]
        for j in range(B_GQ):
            o_ref[cl * T:(cl + 1) * T, j * d:(j + 1) * d] = o[j * T:(j + 1) * T].astype(o_ref.dtype)

    if pad:
        @pl.when(c * (cps * T) < pad)
        def _():
            for cl in range(cps):
                chunk(cl, True)

        @pl.when(c * (cps * T) >= pad)
        def _():
            for cl in range(cps):
                chunk(cl, False)
    else:
        for cl in range(cps):
            chunk(cl, False)


def band_attention(u, k, v, table, q_gain, *, row0, batch, nc, T, KB, pad, cps):
    rows = batch * nc * T
    r0 = row0 // (cps * T)
    ns = nc // cps
    ktot = k.shape[2]
    gw = B_GQ * B_HEAD_DIM
    tab = table[:, jnp.clip(2 * B_HEAD_DIM - jnp.arange(2 * B_HEAD_DIM), 0, 2 * REL_CLIP)].astype(F32)
    tab = tab.reshape(B_KV_HEADS, B_GQ, 2 * B_HEAD_DIM)
    return pl.pallas_call(
        functools.partial(_band_body, T=T, KB=KB, pad=pad, cps=cps),
        grid=(batch, B_KV_HEADS, ns),
        in_specs=[
            pl.BlockSpec((cps * T, gw), lambda b, g, c: (r0 + b * ns + c, g)),
            pl.BlockSpec((1, 1, ktot, B_HEAD_DIM), lambda b, g, c: (b, g, 0, 0)),
            pl.BlockSpec((1, 1, ktot, B_HEAD_DIM), lambda b, g, c: (b, g, 0, 0)),
            pl.BlockSpec((1, B_GQ, 2 * B_HEAD_DIM), lambda b, g, c: (g, 0, 0)),
            pl.BlockSpec((1, B_HEAD_DIM), lambda b, g, c: (0, 0)),
        ],
        out_specs=pl.BlockSpec((cps * T, gw), lambda b, g, c: (b * ns + c, g)),
        out_shape=jax.ShapeDtypeStruct((rows, B_Q_WIDTH), BF16),
        scratch_shapes=[pltpu.VMEM((B_GQ * T, KB - BAND_CONST_KEYS), F32)],
        compiler_params=_params("parallel", "parallel", "arbitrary"),
        name="band_attention",
    )(u, k, v, tab, q_gain.reshape(1, B_HEAD_DIM).astype(F32))


def _mem_body(q_ref, k_ref, v_ref, qg_ref, o_ref):
    scale = 1.0 / math.sqrt(MEM_HEAD_DIM)
    for h in range(MEM_HEADS):
        sl = slice(h * MEM_HEAD_DIM, (h + 1) * MEM_HEAD_DIM)
        qn = (_rms_scale(q_ref[:, sl].astype(F32)) * qg_ref[...] * scale).astype(BF16)
        s = _dot_nt(qn, k_ref[0, :, sl])
        m = jnp.max(s, axis=-1, keepdims=True)
        p = jnp.exp(s - m)
        l = jnp.sum(p, axis=-1, keepdims=True)
        o_ref[:, sl] = (_dot(p.astype(BF16), v_ref[0, :, sl]) / l).astype(o_ref.dtype)


def memory_attention(u, k, v, q_gain, *, q_col_block, row0, batch, nt, tq):
    rows = batch * nt * tq
    r0 = row0 // tq
    return pl.pallas_call(
        _mem_body,
        grid=(batch, nt),
        in_specs=[
            pl.BlockSpec((tq, MEM_WIDTH), lambda b, i: (r0 + b * nt + i, q_col_block)),
            pl.BlockSpec((1, MEM_TOKENS, MEM_WIDTH), lambda b, i: (b, 0, 0)),
            pl.BlockSpec((1, MEM_TOKENS, MEM_WIDTH), lambda b, i: (b, 0, 0)),
            pl.BlockSpec((1, MEM_HEAD_DIM), lambda b, i: (0, 0)),
        ],
        out_specs=pl.BlockSpec((tq, MEM_WIDTH), lambda b, i: (b * nt + i, 0)),
        out_shape=jax.ShapeDtypeStruct((rows, MEM_WIDTH), BF16),
        compiler_params=_params("parallel", "arbitrary"),
        name="memory_attention",
    )(u, k, v, q_gain.reshape(1, MEM_HEAD_DIM).astype(F32))


def _heads_major(x, n_heads):
    b, l, w = x.shape
    return x.reshape(b, l, n_heads, w // n_heads).transpose(0, 2, 1, 3)


def _ffn_act(h, g_ffn, w_gate, w_up, w_down, layer):
    (hn,) = rmsnorm(h, g_ffn.reshape(1, -1))
    return matmul_wres(hn, [w_gate, w_up], layer, D_FF_PAD, BF16, WRES_ROW_TILE, 512, _silu_gate,
                       side_cast=(w_down, layer, D_FF_PAD, 32))


def kernel(x_prompt, x_sample, mem_prompt, state_ssm, state_conv, cache_kv_k, cache_kv_v, cache_mem_k, cache_mem_v, g_mix, w_in_a, conv_w, conv_b, dt_bias, a_log, d_skip, g_ssm, w_in_b, rel_bias, q_norm_b, g_kv, w_kv, k_norm_kv, g_mem, w_mem_kv, q_norm_mem, k_norm_mem, w_out, g_ffn, w_ffn_gate, w_ffn_up, w_ffn_down):
    S, NS = SEQ, N_SAMPLE_ROWS
    bf = lambda w: w.astype(BF16)

    main_w = SSM_INNER + CONV_DIM
    w_in_a_t = jnp.swapaxes(w_in_a, 1, 2)
    in_a = functools.partial(matmul_wres, ws=[w_in_a_t], layer=0, tm=WRES_ROW_TILE, transposed=True)

    xp = x_prompt.reshape(S, D_MODEL)
    xs = x_sample.reshape(NS, D_MODEL)

    mem_k32, mem_v32, mem_k16, mem_v16 = [], [], [], []
    mem_x = mem_prompt.reshape(MEM_TOKENS, D_MODEL)
    for l in range(2):
        (mn,) = rmsnorm(mem_x, g_mem[l].reshape(1, -1))
        kv = matmul_wres(mn, [w_mem_kv], l, 2 * MEM_WIDTH, F32, MEM_TOKENS, 1024)
        k32, k16, v16 = kv_post(kv, k_norm_mem[l], MEM_HEAD_DIM)
        mem_k32.append(k32.reshape(1, MEM_TOKENS, MEM_HEADS, MEM_HEAD_DIM))
        mem_v32.append(kv[:, MEM_WIDTH:].reshape(1, MEM_TOKENS, MEM_HEADS, MEM_HEAD_DIM))
        mem_k16.append(k16.reshape(1, MEM_TOKENS, MEM_WIDTH))
        mem_v16.append(v16.reshape(1, MEM_TOKENS, MEM_WIDTH))
    cmk = [bf(cache_mem_k[l]).reshape(DEC_BATCH, MEM_TOKENS, MEM_WIDTH) for l in range(2)]
    cmv = [bf(cache_mem_v[l]).reshape(DEC_BATCH, MEM_TOKENS, MEM_WIDTH) for l in range(2)]

    def mem_attn(u, q_col_block, l):
        attn = functools.partial(memory_attention, u, q_gain=q_norm_mem[l], q_col_block=q_col_block)
        yp = attn(k=mem_k16[l], v=mem_v16[l], row0=0, batch=1, nt=S // 512, tq=512)
        ys = attn(k=cmk[l], v=cmv[l], row0=S, batch=DEC_BATCH, nt=1, tq=DEC_SEQ)
        return yp, ys

    hn = rmsnorm_rows2(xp, xs, g_mix[0])
    u, wout0 = in_a(hn, n_out=main_w, out_dtype=BF16, tn=1024,
                    side_cast=(w_out, 0, w_out.shape[1], 32))
    dt_raw = in_a(hn, n_out=DT_PAD, out_dtype=F32, tn=DT_PAD, col0=main_w)
    q_mem = in_a(hn, n_out=MEM_WIDTH, out_dtype=BF16, tn=1024, col0=main_w + SSM_HEADS)

    def to_state_t(s):
        b = s.shape[0]
        return s.reshape(b, SSM_INNER, SSM_STATE).transpose(0, 2, 1)

    def from_state_t(s):
        b = s.shape[0]
        return s.transpose(0, 2, 1).reshape(b, SSM_HEADS, SSM_HEAD_DIM, SSM_STATE)

    ssd = functools.partial(ssd_mixer, u, dt_raw, conv_w=conv_w[0], conv_b=conv_b[0], dt_bias=dt_bias[0],
                            a_log=a_log[0], d_skip=d_skip[0], g_ssm=g_ssm[0])
    tail = CONV_WIDTH - 1
    conv0_p = jnp.zeros((1, 8, CONV_DIM), F32)
    st0_p = jnp.zeros((1, SSM_STATE, SSM_INNER), F32)
    y_p, conv_p, st_p = ssd(conv0=conv0_p, st0=st0_p, row0=0, batch=1, nc=S // CHUNK, T=CHUNK)
    conv0_s = jnp.pad(state_conv[0], ((0, 0), (8 - tail, 0), (0, 0)))
    y_s, conv_s, st_s = ssd(conv0=conv0_s, st0=to_state_t(state_ssm[0]), row0=S, batch=DEC_BATCH, nc=1, T=DEC_SEQ)
    ym_p, ym_s = mem_attn(q_mem, 0, 0)
    h = matmul_out(y_p, y_s, ym_p, ym_s, wout0, xp, xs, 0, 1024, 2048)
    act, wd0 = _ffn_act(h, g_ffn[0], w_ffn_gate, w_ffn_up, w_ffn_down, 0)
    h = ffn_down(act, wd0, h, row0=0, rows=N_ROWS, tm=ROW_TILE, tn=1024, tk=2816)

    hkv, hn = rmsnorm(h, jnp.stack([g_kv, g_mix[1]]))
    kvw = B_KV_HEADS * B_HEAD_DIM
    kv = matmul_wres(hkv, [w_kv[None]], 0, 2 * kvw, F32, WRES_ROW_TILE, 1024)
    k32, k16, v16 = kv_post(kv, k_norm_kv, B_HEAD_DIM)
    u, wout1 = matmul_wres(hn, [w_in_b], 0, B_Q_WIDTH + MEM_WIDTH, BF16, WRES_ROW_TILE, 1024,
                           side_cast=(w_out, 1, w_out.shape[1], 64))

    kp = jnp.pad(_heads_major(k16[:S].reshape(1, S, kvw), B_KV_HEADS), ((0, 0), (0, 0), (BAND_PAST, 0), (0, 0)))
    vp = jnp.pad(_heads_major(v16[:S].reshape(1, S, kvw), B_KV_HEADS), ((0, 0), (0, 0), (BAND_PAST, 0), (0, 0)))
    yb_p = band_attention(u, kp, vp, rel_bias[0], q_norm_b[0], row0=0, batch=1, nc=S // CHUNK, T=CHUNK,
                          KB=BAND_PAST + CHUNK, pad=BAND_PAST, cps=BAND_CHUNKS_PER_STEP)

    wc = cache_kv_k.shape[1]
    ks = jnp.concatenate([bf(cache_kv_k).reshape(DEC_BATCH, wc, kvw), k16[S:].reshape(DEC_BATCH, DEC_SEQ, kvw)], axis=1)
    vs = jnp.concatenate([bf(cache_kv_v).reshape(DEC_BATCH, wc, kvw), v16[S:].reshape(DEC_BATCH, DEC_SEQ, kvw)], axis=1)
    yb_s = band_attention(u, _heads_major(ks, B_KV_HEADS), _heads_major(vs, B_KV_HEADS), rel_bias[0], q_norm_b[0],
                          row0=S, batch=DEC_BATCH, nc=1, T=DEC_SEQ, KB=wc + DEC_SEQ, pad=0, cps=1)
    ym_p, ym_s = mem_attn(u, B_Q_WIDTH // MEM_WIDTH, 1)
    h = matmul_out(yb_p, yb_s, ym_p, ym_s, wout1, h, h, N_ROWS // SUB_ROWS - 1, 1024, 2048)
    act, wd1 = _ffn_act(h, g_ffn[1], w_ffn_gate, w_ffn_up, w_ffn_down, 1)
    y_prompt = ffn_down(act, wd1, h, row0=0, rows=S, tm=1024, tn=1024, tk=2816)
    y_sample = ffn_down(act, wd1, h, row0=S, rows=NS, tm=NS, tn=1024, tk=2816)

    keep = min(BAND_PAST, S)
    kv_shape = (B_KV_HEADS, B_HEAD_DIM)
    return (
        y_prompt.reshape(1, S, D_MODEL),
        y_sample.reshape(DEC_BATCH, DEC_SEQ, D_MODEL),
        from_state_t(st_p)[None],
        conv_p[:, 8 - tail:][None],
        k32[S - keep:S].reshape(1, keep, *kv_shape),
        kv[S - keep:S, kvw:].reshape(1, keep, *kv_shape),
        jnp.stack(mem_k32),
        jnp.stack(mem_v32),
        from_state_t(st_s)[None],
        conv_s[:, 8 - tail:][None],
        k32[S:].reshape(DEC_BATCH, DEC_SEQ, *kv_shape),
        kv[S:, kvw:].reshape(DEC_BATCH, DEC_SEQ, *kv_shape),
    )
```

```python
import functools
import math

import jax
import jax.numpy as jnp
from jax import lax
from jax.experimental import pallas as pl
from jax.experimental.pallas import tpu as pltpu

F32 = jnp.float32
BF16 = jnp.bfloat16

D_MODEL = 4096
SEQ = 8192
DEC_BATCH = 8
DEC_SEQ = 32
PAST_LEN = 2048
CHUNK = 64
RMS_EPS = 1e-6

SSM_HEAD_DIM = 64
SSM_INNER = 6144
SSM_HEADS = 96
SSM_PAIRS = SSM_HEADS // 2
SSM_GROUPS = 8
SSM_PPG = SSM_PAIRS // SSM_GROUPS
SSM_STATE = 128
CONV_WIDTH = 4
CONV_DIM = 8192
XBC_BLOCK = 2048

B_HEAD_DIM = 128
B_Q_WIDTH = 6144
B_HEADS = 48
B_KV_HEADS = 8
B_GQ = 6
BAND_PAST = 512
REL_CLIP = 64
BAND_CONST_KEYS = 384
BAND_CHUNKS_PER_STEP = 4

MEM_TOKENS = 256
MEM_WIDTH = 2048
MEM_HEADS = 4
MEM_HEAD_DIM = 512

D_FF = 11008
D_FF_PAD = 11264
DT_PAD = 128

N_SAMPLE_ROWS = DEC_BATCH * DEC_SEQ
N_ROWS = SEQ + N_SAMPLE_ROWS
ROW_TILE = 768
WRES_ROW_TILE = 384
SUB_ROWS = 256
SUBS_PER_TILE = ROW_TILE // SUB_ROWS

V7X_VMEM_LIMIT_BYTES = 56 * 1024 * 1024


def _params(*sem):
    return pltpu.CompilerParams(dimension_semantics=sem, vmem_limit_bytes=V7X_VMEM_LIMIT_BYTES)


def _sigmoid(x):
    return 1.0 / (1.0 + jnp.exp(-x))


def _softplus(x):
    return jnp.maximum(x, 0.0) + jnp.log1p(jnp.exp(-jnp.abs(x)))


def _dot(a, b):
    return jnp.dot(a, b, preferred_element_type=F32)


def _dot_nt(a, b):
    return lax.dot_general(a, b, (((1,), (1,)), ((), ())), preferred_element_type=F32)


def _dot_exact(a, b):
    return jnp.dot(a, b, precision=lax.Precision.HIGHEST, preferred_element_type=F32)


def _rms_scale(x):
    return x * lax.rsqrt(jnp.mean(x * x, axis=-1, keepdims=True) + RMS_EPS)


def _rmsnorm_body(x_ref, g_ref, *o_refs):
    xn = _rms_scale(x_ref[...])
    for j, o_ref in enumerate(o_refs):
        o_ref[...] = (xn * g_ref[j:j + 1, :]).astype(o_ref.dtype)


def rmsnorm(x, gains, tm=256):
    m, d = x.shape
    n = gains.shape[0]
    return pl.pallas_call(
        _rmsnorm_body,
        grid=(m // tm,),
        in_specs=[pl.BlockSpec((tm, d), lambda i: (i, 0)),
                  pl.BlockSpec((n, d), lambda i: (0, 0))],
        out_specs=[pl.BlockSpec((tm, d), lambda i: (i, 0))] * n,
        out_shape=[jax.ShapeDtypeStruct((m, d), BF16)] * n,
        compiler_params=_params("parallel"),
        name="rmsnorm",
    )(x, gains)


def _rmsnorm2_body(xp_ref, xs_ref, g_ref, o_ref, *, n_p):
    i = pl.program_id(0)

    @pl.when(i < n_p)
    def _():
        o_ref[...] = (_rms_scale(xp_ref[...]) * g_ref[...]).astype(o_ref.dtype)

    @pl.when(i >= n_p)
    def _():
        o_ref[...] = (_rms_scale(xs_ref[...]) * g_ref[...]).astype(o_ref.dtype)


def rmsnorm_rows2(xp, xs, gain):
    tm = SUB_ROWS
    d = xp.shape[1]
    n_p, n_s = xp.shape[0] // tm, xs.shape[0] // tm
    return pl.pallas_call(
        functools.partial(_rmsnorm2_body, n_p=n_p),
        grid=(n_p + n_s,),
        in_specs=[pl.BlockSpec((tm, d), lambda i: (jnp.minimum(i, n_p - 1), 0)),
                  pl.BlockSpec((tm, d), lambda i: (jnp.maximum(i - n_p, 0), 0)),
                  pl.BlockSpec((1, d), lambda i: (0, 0))],
        out_specs=pl.BlockSpec((tm, d), lambda i: (i, 0)),
        out_shape=jax.ShapeDtypeStruct((xp.shape[0] + xs.shape[0], d), BF16),
        compiler_params=_params("parallel"),
        name="rmsnorm_rows2",
    )(xp, xs, gain.reshape(1, d))


def _wres_body(x_ref, *refs, n_w, layer, col0, n_valid, tn, transposed, epilogue, side):
    w_hbm, refs = refs[:n_w], refs[n_w:]
    if side:
        side_w, o_ref, side_o, refs = refs[0], refs[1], refs[2], refs[3:]
    else:
        o_ref, refs = refs[0], refs[1:]
    wbf, wf32, sem = refs[:n_w], refs[n_w:2 * n_w], refs[2 * n_w]
    j, i = pl.program_id(0), pl.program_id(1)
    nj = pl.num_programs(0)
    rem = n_valid % tn

    if side:
        rows_valid, rb, n_blocks = side
        blk = jnp.minimum(j * pl.num_programs(1) + i, n_blocks - 1)
        row = blk * rb + lax.broadcasted_iota(jnp.int32, side_w.shape, 0)
        side_o[...] = jnp.where(row < rows_valid, side_w[...], 0.0).astype(side_o.dtype)

    def for_each_copy(jj, slot, fn):
        def go(width):
            for w in range(n_w):
                if transposed:
                    src = w_hbm[w].at[layer, pl.ds(col0 + jj * tn, width), :]
                    dst = wf32[w].at[slot, pl.ds(0, width), :]
                else:
                    src = w_hbm[w].at[layer, :, pl.ds(col0 + jj * tn, width)]
                    dst = wf32[w].at[slot, :, pl.ds(0, width)]
                fn(pltpu.make_async_copy(src, dst, sem.at[w, slot]))
        if rem:
            pl.when(jj < nj - 1)(lambda: go(tn))
            pl.when(jj == nj - 1)(lambda: go(rem))
        else:
            go(tn)

    @pl.when(i == 0)
    def _():
        slot = j % 2

        start = lambda cp: cp.start(priority=1)

        @pl.when(j == 0)
        def _():
            for_each_copy(0, 0, start)

        for_each_copy(j, slot, lambda cp: cp.wait())

        @pl.when(j + 1 < nj)
        def _():
            for_each_copy(j + 1, 1 - slot, start)

        def convert(ragged):
            for w in range(n_w):
                wt = wf32[w][slot]
                if ragged:
                    col = lax.broadcasted_iota(jnp.int32, wt.shape, 0 if transposed else 1)
                    wt = jnp.where(col < rem, wt, 0.0)
                wbf[w][...] = (wt.T if transposed else wt).astype(BF16)

        if rem:
            pl.when(j < nj - 1)(lambda: convert(False))
            pl.when(j == nj - 1)(lambda: convert(True))
        else:
            convert(False)

    x = x_ref[...]
    o_ref[...] = epilogue(*[_dot(x, wb[...]) for wb in wbf]).astype(o_ref.dtype)


def matmul_wres(x, ws, layer, n_out, out_dtype, tm, tn, epilogue=lambda acc: acc, transposed=False, col0=0,
                side_cast=None):
    m, k = x.shape
    n_w = len(ws)
    n_valid = min(n_out, ws[0].shape[1 if transposed else 2] - col0)
    nj, ni = n_out // tn, m // tm
    in_specs = [pl.BlockSpec((tm, k), lambda j, i: (i, 0))] + [pl.BlockSpec(memory_space=pl.ANY)] * n_w
    out_specs = [pl.BlockSpec((tm, tn), lambda j, i: (i, j))]
    out_shape = [jax.ShapeDtypeStruct((m, n_out), out_dtype)]
    args = [x, *ws]
    side = None
    if side_cast:
        w2, layer2, rows_out, rb = side_cast
        rows_valid, n2 = w2.shape[1], w2.shape[2]
        n_blocks, n_src_blocks = rows_out // rb, rows_valid // rb
        assert rows_out % rb == 0 and rows_valid % rb == 0 and n_blocks <= nj * ni
        side = (rows_valid, rb, n_blocks)
        in_specs.append(pl.BlockSpec((None, rb, n2), lambda j, i: (layer2, jnp.minimum(j * ni + i, n_src_blocks - 1), 0)))
        out_specs.append(pl.BlockSpec((rb, n2), lambda j, i: (jnp.minimum(j * ni + i, n_blocks - 1), 0)))
        out_shape.append(jax.ShapeDtypeStruct((rows_out, n2), BF16))
        args.append(w2)
    outs = pl.pallas_call(
        functools.partial(_wres_body, n_w=n_w, layer=layer, col0=col0, n_valid=n_valid, tn=tn,
                          transposed=transposed, epilogue=epilogue, side=side),
        grid=(nj, ni),
        in_specs=in_specs,
        out_specs=out_specs,
        out_shape=out_shape,
        scratch_shapes=[pltpu.VMEM((k, tn), BF16)] * n_w
                       + [pltpu.VMEM((2, tn, k) if transposed else (2, k, tn), F32)] * n_w
                       + [pltpu.SemaphoreType.DMA((n_w, 2))],
        compiler_params=_params("arbitrary", "arbitrary"),
        name="matmul_wres",
    )(*args)
    return outs if side_cast else outs[0]


def _mm_out_body(xap_ref, xas_ref, xbp_ref, xbs_ref, w_ref, rp_ref, rs_ref, o_ref, *, ka):
    last_tile = pl.program_id(0) == pl.num_programs(0) - 1
    head = (SUBS_PER_TILE - 1) * SUB_ROWS

    def emit(rows, xa, xb, res):
        o_ref[rows, :] = res + _dot(xa, w_ref[0:ka, :]) + _dot(xb, w_ref[ka:, :])

    @pl.when(jnp.logical_not(last_tile))
    def _():
        emit(slice(None), xap_ref[...].reshape(ROW_TILE, -1), xbp_ref[...].reshape(ROW_TILE, -1),
             rp_ref[...].reshape(ROW_TILE, -1))

    @pl.when(last_tile)
    def _():
        n_head = SUBS_PER_TILE - 1
        emit(slice(0, head), xap_ref[0:n_head].reshape(head, -1), xbp_ref[0:n_head].reshape(head, -1),
             rp_ref[0:n_head].reshape(head, -1))
        emit(slice(head, ROW_TILE), xas_ref[0], xbs_ref[0], rs_ref[0])


def matmul_out(xa_p, xa_s, xb_p, xb_s, w, r_p, r_s, r_s_block, tn):
    sub3 = lambda a: a.reshape(a.shape[0] // SUB_ROWS, SUB_ROWS, a.shape[1])
    ka, kb, n = xa_p.shape[1], xb_p.shape[1], w.shape[1]
    nt = N_ROWS // ROW_TILE
    spt = SUBS_PER_TILE
    once = dict(pipeline_mode=pl.Buffered(1))
    return pl.pallas_call(
        functools.partial(_mm_out_body, ka=ka),
        grid=(nt, n // tn),
        in_specs=[pl.BlockSpec((spt, SUB_ROWS, ka), lambda i, j: (i, 0, 0)),
                  pl.BlockSpec((1, SUB_ROWS, ka), lambda i, j: (0, 0, 0), **once),
                  pl.BlockSpec((spt, SUB_ROWS, kb), lambda i, j: (i, 0, 0)),
                  pl.BlockSpec((1, SUB_ROWS, kb), lambda i, j: (0, 0, 0), **once),
                  pl.BlockSpec((ka + kb, tn), lambda i, j: (0, j)),
                  pl.BlockSpec((spt, SUB_ROWS, tn), lambda i, j: (i, 0, j)),
                  pl.BlockSpec((1, SUB_ROWS, tn), lambda i, j: (r_s_block, 0, j))],
        out_specs=pl.BlockSpec((ROW_TILE, tn), lambda i, j: (i, j)),
        out_shape=jax.ShapeDtypeStruct((N_ROWS, n), F32),
        compiler_params=_params("parallel", "arbitrary"),
        name="matmul_out",
    )(sub3(xa_p), sub3(xa_s), sub3(xb_p), sub3(xb_s), w, sub3(r_p), sub3(r_s))


def _silu_gate(g, u):
    return g * _sigmoid(g) * u


def _ffn_down_body(x_ref, w_ref, r_ref, o_ref, acc_ref):
    k = pl.program_id(2)

    @pl.when(k == 0)
    def _():
        acc_ref[...] = r_ref[...]

    acc_ref[...] += _dot(x_ref[...], w_ref[...])

    @pl.when(k == pl.num_programs(2) - 1)
    def _():
        o_ref[...] = acc_ref[...]


def ffn_down(x, w, res, *, row0, rows, tm, tn, tk):
    kk = x.shape[1]
    n = w.shape[1]
    r0 = row0 // tm
    return pl.pallas_call(
        _ffn_down_body,
        grid=(rows // tm, n // tn, kk // tk),
        in_specs=[pl.BlockSpec((tm, tk), lambda i, j, k: (r0 + i, k)),
                  pl.BlockSpec((tk, tn), lambda i, j, k: (k, j)),
                  pl.BlockSpec((tm, tn), lambda i, j, k: (r0 + i, j))],
        out_specs=pl.BlockSpec((tm, tn), lambda i, j, k: (i, j)),
        out_shape=jax.ShapeDtypeStruct((rows, n), F32),
        scratch_shapes=[pltpu.VMEM((tm, tn), F32)],
        compiler_params=_params("parallel", "arbitrary", "arbitrary"),
        name="ffn_down",
    )(x, w, res)


def _kv_post_body(kv_ref, g_ref, k32_ref, k16_ref, v16_ref, *, width, head_dim, pad_blocks):
    keep = pl.program_id(0) >= pad_blocks
    for h in range(width // head_dim):
        sl = slice(h * head_dim, (h + 1) * head_dim)
        kn = _rms_scale(kv_ref[:, sl]) * g_ref[...]
        k32_ref[:, sl] = kn
        k16_ref[:, sl] = jnp.where(keep, kn, 0.0).astype(BF16)
    v16_ref[...] = jnp.where(keep, kv_ref[:, width:], 0.0).astype(BF16)


def kv_post(kv, gain, head_dim, pad_rows=0, tm=256):
    m, w2 = kv.shape
    width = w2 // 2
    pb = pad_rows // tm
    src = lambda i: (jnp.maximum(i - pb, 0), 0)
    return pl.pallas_call(
        functools.partial(_kv_post_body, width=width, head_dim=head_dim, pad_blocks=pb),
        grid=(pb + m // tm,),
        in_specs=[pl.BlockSpec((tm, w2), src),
                  pl.BlockSpec((1, head_dim), lambda i: (0, 0))],
        out_specs=[pl.BlockSpec((tm, width), src)] + [pl.BlockSpec((tm, width), lambda i: (i, 0))] * 2,
        out_shape=[jax.ShapeDtypeStruct((m, width), F32),
                   jax.ShapeDtypeStruct((pad_rows + m, width), BF16),
                   jax.ShapeDtypeStruct((pad_rows + m, width), BF16)],
        compiler_params=_params("arbitrary"),
        name="kv_post",
    )(kv, gain.reshape(1, head_dim))


def _ssd_body(z_ref, xbc0_ref, xbc1_ref, xbc2_ref, xbc3_ref, dt_ref, dtT_ref, conv0_ref, st0_ref, cw_ref, cb_ref,
              dtb_ref, dtbT_ref, alog_ref, alogT_ref, dskip_ref, gssm_ref,
              y_ref, convn_ref, stn_ref,
              xbuf, xc, st, yacc, *, T):
    c = pl.program_id(1)
    nc = pl.num_programs(1)
    P, N = SSM_HEAD_DIM, SSM_STATE
    T2, PW = 2 * T, 2 * P
    tail = CONV_WIDTH - 1

    @pl.when(c == 0)
    def _():
        xbuf[0:8, :] = conv0_ref[0]
        st[...] = st0_ref[0]

    for q, xbc_ref in enumerate((xbc0_ref, xbc1_ref, xbc2_ref, xbc3_ref)):
        xbuf[8:8 + T, q * XBC_BLOCK:(q + 1) * XBC_BLOCK] = xbc_ref[...].astype(F32)
    acc = cb_ref[...] + cw_ref[0:1, :] * xbuf[8 - tail:8 - tail + T, :]
    for tap in range(1, CONV_WIDTH):
        acc = acc + cw_ref[tap:tap + 1, :] * xbuf[8 - tail + tap:8 - tail + tap + T, :]
    xc[...] = acc * _sigmoid(acc)
    last_rows = xbuf[T:T + 8, :]
    convn_ref[0] = last_rows
    xbuf[0:8, :] = last_rows

    iota = lambda shape, d: lax.broadcasted_iota(jnp.int32, shape, d)
    tri = iota((T, T), 1) <= iota((T, T), 0)
    dt = _softplus(dt_ref[:, 0:SSM_HEADS] + dtb_ref[...])
    cum = _dot_exact(tri.astype(F32), dt * (-jnp.exp(alog_ref[...])))

    half = lambda x, size: jnp.where(x >= size, 1, 0)
    ra, rb = iota((T2, T2), 0), iota((T2, T2), 1)
    same = half(ra, T) == half(rb, T)
    dtT = _softplus(dtT_ref[0] + dtbT_ref[...])
    dtaT = dtT * (-jnp.exp(alogT_ref[...]))
    cumT = _dot_exact(dtaT, jnp.where(same & (ra <= rb), 1.0, 0.0))
    lastT = _dot_exact(dtaT, jnp.where(same, 1.0, 0.0))
    same_p = half(iota((T2, PW), 0), T) == half(iota((T2, PW), 1), P)
    edec = jnp.exp(_dot_exact(dtaT, jnp.where(same_p, 1.0, 0.0)))
    wT = dtT * jnp.exp(lastT - cumT)

    left = iota((T, PW), 1) < P
    left2 = iota((T, T2), 1) < T
    tri2 = (iota((T, T2), 1) - T * half(iota((T, T2), 1), T)) <= iota((T, T2), 0)
    bc_off = SSM_INNER
    cc_off = SSM_INNER + SSM_GROUPS * N

    for g in range(SSM_GROUPS):
        bg = xc[:, bc_off + g * N:bc_off + (g + 1) * N]
        cg16 = xc[:, cc_off + g * N:cc_off + (g + 1) * N].astype(BF16)
        bg2 = jnp.concatenate([bg, bg], axis=0)
        cb2 = _dot_nt(cg16, bg2.astype(BF16))
        bgT2 = bg2.T
        gw = SSM_PPG * PW
        yoff = _dot(cg16, st[:, g * gw:(g + 1) * gw].astype(BF16))
        for k in range(SSM_PPG):
            q = g * SSM_PPG + k
            ps = slice(q * PW, (q + 1) * PW)
            b0 = jnp.broadcast_to(cum[:, 2 * q:2 * q + 1], (T, PW))
            b1 = jnp.broadcast_to(cum[:, 2 * q + 1:2 * q + 2], (T, PW))
            col_p = jnp.where(left, b0, b1)
            col_2 = col_p if T2 == PW else jnp.where(left2, b0[:, 0:T2], b1[:, 0:T2])
            dec = jnp.where(tri2, jnp.exp(col_2 - cumT[q:q + 1, :]), 0.0)
            m2 = (cb2 * dec * dtT[q:q + 1, :]).astype(BF16)
            xp = xc[:, ps]
            bd = jnp.concatenate([jnp.where(left, xp, 0.0), jnp.where(left, 0.0, xp)],
                                 axis=0).astype(BF16)
            yacc[:, ps] = _dot(m2, bd) + yoff[:, k * PW:(k + 1) * PW] * jnp.exp(col_p)
            btw = (bgT2 * wT[q:q + 1, :]).astype(BF16)
            st[:, ps] = st[:, ps] * edec[q:q + 1, :] + _dot(btw, bd)

    y = yacc[...] + xc[:, 0:SSM_INNER] * dskip_ref[...]
    z = z_ref[...].astype(F32)
    yg = y * (z * _sigmoid(z))
    y_ref[...] = (_rms_scale(yg) * gssm_ref[...]).astype(y_ref.dtype)

    @pl.when(c == nc - 1)
    def _():
        stn_ref[0] = st[...]


def ssd_mixer(u, dt_raw, conv0, st0, conv_w, conv_b, dt_bias, a_log, d_skip, g_ssm,
              *, row0, batch, nc, T):
    rows = batch * nc * T
    r0 = row0 // T
    nblk = batch * nc
    dtT = (dt_raw[row0:row0 + rows, :SSM_HEADS].reshape(nblk, T, SSM_PAIRS, 2)
           .transpose(0, 2, 3, 1).reshape(nblk, SSM_PAIRS, 2 * T))
    pair_rows = lambda v: jnp.repeat(v.astype(F32).reshape(SSM_PAIRS, 2), T, axis=1)
    row = lambda v: v.reshape(1, -1).astype(F32)
    const = lambda shape: pl.BlockSpec(shape, lambda b, c: (0,) * len(shape))
    xbc_specs = [pl.BlockSpec((T, XBC_BLOCK), functools.partial(
        lambda b, c, q: (r0 + b * nc + c, SSM_INNER // XBC_BLOCK + q), q=q)) for q in range(CONV_DIM // XBC_BLOCK)]
    return pl.pallas_call(
        functools.partial(_ssd_body, T=T),
        grid=(batch, nc),
        in_specs=[
            pl.BlockSpec((T, SSM_INNER), lambda b, c: (r0 + b * nc + c, 0)),
            *xbc_specs,
            pl.BlockSpec((T, DT_PAD), lambda b, c: (r0 + b * nc + c, 0)),
            pl.BlockSpec((1, SSM_PAIRS, 2 * T), lambda b, c: (b * nc + c, 0, 0)),
            pl.BlockSpec((1, 8, CONV_DIM), lambda b, c: (b, 0, 0)),
            pl.BlockSpec((1, SSM_STATE, SSM_INNER), lambda b, c: (b, 0, 0)),
            const((CONV_WIDTH, CONV_DIM)), const((1, CONV_DIM)),
            const((1, SSM_HEADS)), const((SSM_PAIRS, 2 * T)),
            const((1, SSM_HEADS)), const((SSM_PAIRS, 2 * T)),
            const((1, SSM_INNER)), const((1, SSM_INNER)),
        ],
        out_specs=[
            pl.BlockSpec((T, SSM_INNER), lambda b, c: (b * nc + c, 0)),
            pl.BlockSpec((1, 8, CONV_DIM), lambda b, c: (b, 0, 0)),
            pl.BlockSpec((1, SSM_STATE, SSM_INNER), lambda b, c: (b, 0, 0)),
        ],
        out_shape=[
            jax.ShapeDtypeStruct((rows, SSM_INNER), BF16),
            jax.ShapeDtypeStruct((batch, 8, CONV_DIM), F32),
            jax.ShapeDtypeStruct((batch, SSM_STATE, SSM_INNER), F32),
        ],
        scratch_shapes=[
            pltpu.VMEM((8 + T, CONV_DIM), F32),
            pltpu.VMEM((T, CONV_DIM), F32),
            pltpu.VMEM((SSM_STATE, SSM_INNER), F32),
            pltpu.VMEM((T, SSM_INNER), F32),
        ],
        compiler_params=_params("parallel", "arbitrary"),
        name="ssd_mixer",
    )(u, u, u, u, u, dt_raw, dtT, conv0, st0, conv_w.T.astype(F32), row(conv_b),
      row(dt_bias), pair_rows(dt_bias), row(a_log), pair_rows(a_log),
      row(jnp.repeat(d_skip, SSM_HEAD_DIM)), row(g_ssm))


def _band_body(q_ref, k_ref, v_ref, tab_ref, qg_ref, o_ref, bias_var, *, T, KB, pad, cps):
    c = pl.program_id(2)
    KA = BAND_CONST_KEYS
    d = B_HEAD_DIM

    @pl.when(c == 0)
    def _():
        for j in range(B_GQ):
            tj = tab_ref[0, j:j + 1, :]
            band = pltpu.roll(jnp.broadcast_to(tj, (T, 2 * d)), 2 * d - REL_CLIP, 1,
                              stride=1, stride_axis=0)
            bias_var[j * T:(j + 1) * T, :] = band[:, 0:KB - KA] - tj[:, 0:1]

    scale = 1.0 / math.sqrt(d)

    def chunk(cl, masked):
        start = pl.multiple_of((c * cps + cl) * T, T)
        k = k_ref[0, pl.ds(start, KB), :]
        v = v_ref[0, pl.ds(start, KB), :]
        q = jnp.concatenate(
            [(_rms_scale(q_ref[cl * T:(cl + 1) * T, j * d:(j + 1) * d].astype(F32)) * qg_ref[...] * scale
              ).astype(BF16) for j in range(B_GQ)], axis=0)
        s = _dot_nt(q, k)
        sa = s[:, 0:KA]
        sb = s[:, KA:KB] + bias_var[...]
        if masked:
            sa = jnp.where(start + lax.broadcasted_iota(jnp.int32, (1, KA), 1) >= pad, sa, -jnp.inf)
            sb = jnp.where(start + KA + lax.broadcasted_iota(jnp.int32, (1, KB - KA), 1) >= pad, sb, -jnp.inf)
        m = jnp.maximum(jnp.max(sa, axis=-1, keepdims=True), jnp.max(sb, axis=-1, keepdims=True))
        pa = jnp.exp((sa - m).astype(BF16))
        pb = jnp.exp((sb - m).astype(BF16))
        v1 = jnp.concatenate([v, jnp.ones_like(v)], axis=1)
        ol = _dot(pa, v1[0:KA]) + _dot(pb, v1[KA:KB])
        o = ol[:, 0:d] / ol[:, d:2 * d]
        for j in range(B_GQ):
            o_ref[cl * T:(cl + 1) * T, j * d:(j + 1) * d] = o[j * T:(j + 1) * T].astype(o_ref.dtype)

    if pad:
        @pl.when(c * (cps * T) < pad)
        def _():
            for cl in range(cps):
                chunk(cl, True)

        @pl.when(c * (cps * T) >= pad)
        def _():
            for cl in range(cps):
                chunk(cl, False)
    else:
        for cl in range(cps):
            chunk(cl, False)


def band_attention(u, k, v, table, q_gain, *, row0, batch, nc, T, KB, pad, cps):
    rows = batch * nc * T
    r0 = row0 // (cps * T)
    ns = nc // cps
    ktot = (nc - 1) * T + KB
    gw = B_GQ * B_HEAD_DIM
    tab = table[:, jnp.clip(2 * B_HEAD_DIM - jnp.arange(2 * B_HEAD_DIM), 0, 2 * REL_CLIP)].astype(F32)
    tab = tab.reshape(B_KV_HEADS, B_GQ, 2 * B_HEAD_DIM)
    return pl.pallas_call(
        functools.partial(_band_body, T=T, KB=KB, pad=pad, cps=cps),
        grid=(batch, B_KV_HEADS, ns),
        in_specs=[
            pl.BlockSpec((cps * T, gw), lambda b, g, c: (r0 + b * ns + c, g)),
            pl.BlockSpec((1, ktot, B_HEAD_DIM), lambda b, g, c: (b, 0, g)),
            pl.BlockSpec((1, ktot, B_HEAD_DIM), lambda b, g, c: (b, 0, g)),
            pl.BlockSpec((1, B_GQ, 2 * B_HEAD_DIM), lambda b, g, c: (g, 0, 0)),
            pl.BlockSpec((1, B_HEAD_DIM), lambda b, g, c: (0, 0)),
        ],
        out_specs=pl.BlockSpec((cps * T, gw), lambda b, g, c: (b * ns + c, g)),
        out_shape=jax.ShapeDtypeStruct((rows, B_Q_WIDTH), BF16),
        scratch_shapes=[pltpu.VMEM((B_GQ * T, KB - BAND_CONST_KEYS), F32)],
        compiler_params=_params("parallel", "parallel", "arbitrary"),
        name="band_attention",
    )(u, k, v, tab, q_gain.reshape(1, B_HEAD_DIM).astype(F32))


def _mem_body(q_ref, k_ref, v_ref, qg_ref, o_ref):
    scale = 1.0 / math.sqrt(MEM_HEAD_DIM)
    for h in range(MEM_HEADS):
        sl = slice(h * MEM_HEAD_DIM, (h + 1) * MEM_HEAD_DIM)
        qn = (_rms_scale(q_ref[:, sl].astype(F32)) * qg_ref[...] * scale).astype(BF16)
        s = _dot_nt(qn, k_ref[0, :, sl])
        m = jnp.max(s, axis=-1, keepdims=True)
        p = jnp.exp(s - m)
        l = jnp.sum(p, axis=-1, keepdims=True)
        o_ref[:, sl] = (_dot(p.astype(BF16), v_ref[0, :, sl]) / l).astype(o_ref.dtype)


def memory_attention(u, k, v, q_gain, *, q_col_block, row0, batch, nt, tq):
    rows = batch * nt * tq
    r0 = row0 // tq
    return pl.pallas_call(
        _mem_body,
        grid=(batch, nt),
        in_specs=[
            pl.BlockSpec((tq, MEM_WIDTH), lambda b, i: (r0 + b * nt + i, q_col_block)),
            pl.BlockSpec((1, MEM_TOKENS, MEM_WIDTH), lambda b, i: (b, 0, 0)),
            pl.BlockSpec((1, MEM_TOKENS, MEM_WIDTH), lambda b, i: (b, 0, 0)),
            pl.BlockSpec((1, MEM_HEAD_DIM), lambda b, i: (0, 0)),
        ],
        out_specs=pl.BlockSpec((tq, MEM_WIDTH), lambda b, i: (b * nt + i, 0)),
        out_shape=jax.ShapeDtypeStruct((rows, MEM_WIDTH), BF16),
        compiler_params=_params("parallel", "arbitrary"),
        name="memory_attention",
    )(u, k, v, q_gain.reshape(1, MEM_HEAD_DIM).astype(F32))


def _ffn_act(h, g_ffn, w_gate, w_up, w_down, layer):
    (hn,) = rmsnorm(h, g_ffn.reshape(1, -1))
    return matmul_wres(hn, [w_gate, w_up], layer, D_FF_PAD, BF16, WRES_ROW_TILE, 512, _silu_gate,
                       side_cast=(w_down, layer, D_FF_PAD, 32))


def kernel(x_prompt, x_sample, mem_prompt, state_ssm, state_conv, cache_kv_k, cache_kv_v, cache_mem_k, cache_mem_v, g_mix, w_in_a, conv_w, conv_b, dt_bias, a_log, d_skip, g_ssm, w_in_b, rel_bias, q_norm_b, g_kv, w_kv, k_norm_kv, g_mem, w_mem_kv, q_norm_mem, k_norm_mem, w_out, g_ffn, w_ffn_gate, w_ffn_up, w_ffn_down):
    S, NS = SEQ, N_SAMPLE_ROWS
    bf = lambda w: w.astype(BF16)

    main_w = SSM_INNER + CONV_DIM
    w_in_a_t = jnp.swapaxes(w_in_a, 1, 2)
    in_a = functools.partial(matmul_wres, ws=[w_in_a_t], layer=0, tm=WRES_ROW_TILE, transposed=True)

    xp = x_prompt.reshape(S, D_MODEL)
    xs = x_sample.reshape(NS, D_MODEL)

    mem_k32, mem_v32, mem_k16, mem_v16 = [], [], [], []
    mem_x = mem_prompt.reshape(MEM_TOKENS, D_MODEL)
    for l in range(2):
        (mn,) = rmsnorm(mem_x, g_mem[l].reshape(1, -1))
        kv = matmul_wres(mn, [w_mem_kv], l, 2 * MEM_WIDTH, F32, MEM_TOKENS, 1024)
        k32, k16, v16 = kv_post(kv, k_norm_mem[l], MEM_HEAD_DIM)
        mem_k32.append(k32.reshape(1, MEM_TOKENS, MEM_HEADS, MEM_HEAD_DIM))
        mem_v32.append(kv[:, MEM_WIDTH:].reshape(1, MEM_TOKENS, MEM_HEADS, MEM_HEAD_DIM))
        mem_k16.append(k16.reshape(1, MEM_TOKENS, MEM_WIDTH))
        mem_v16.append(v16.reshape(1, MEM_TOKENS, MEM_WIDTH))
    cmk = [bf(cache_mem_k[l]).reshape(DEC_BATCH, MEM_TOKENS, MEM_WIDTH) for l in range(2)]
    cmv = [bf(cache_mem_v[l]).reshape(DEC_BATCH, MEM_TOKENS, MEM_WIDTH) for l in range(2)]

    def mem_attn(u, q_col_block, l):
        attn = functools.partial(memory_attention, u, q_gain=q_norm_mem[l], q_col_block=q_col_block)
        yp = attn(k=mem_k16[l], v=mem_v16[l], row0=0, batch=1, nt=S // 512, tq=512)
        ys = attn(k=cmk[l], v=cmv[l], row0=S, batch=DEC_BATCH, nt=1, tq=DEC_SEQ)
        return yp, ys

    hn = rmsnorm_rows2(xp, xs, g_mix[0])
    u, wout0 = in_a(hn, n_out=main_w, out_dtype=BF16, tn=1024,
                    side_cast=(w_out, 0, w_out.shape[1], 32))
    dt_raw = in_a(hn, n_out=DT_PAD, out_dtype=F32, tn=DT_PAD, col0=main_w)
    q_mem = in_a(hn, n_out=MEM_WIDTH, out_dtype=BF16, tn=1024, col0=main_w + SSM_HEADS)

    def to_state_t(s):
        b = s.shape[0]
        return s.reshape(b, SSM_INNER, SSM_STATE).transpose(0, 2, 1)

    def from_state_t(s):
        b = s.shape[0]
        return s.transpose(0, 2, 1).reshape(b, SSM_HEADS, SSM_HEAD_DIM, SSM_STATE)

    ssd = functools.partial(ssd_mixer, u, dt_raw, conv_w=conv_w[0], conv_b=conv_b[0], dt_bias=dt_bias[0],
                            a_log=a_log[0], d_skip=d_skip[0], g_ssm=g_ssm[0])
    tail = CONV_WIDTH - 1
    conv0_p = jnp.zeros((1, 8, CONV_DIM), F32)
    st0_p = jnp.zeros((1, SSM_STATE, SSM_INNER), F32)
    y_p, conv_p, st_p = ssd(conv0=conv0_p, st0=st0_p, row0=0, batch=1, nc=S // CHUNK, T=CHUNK)
    conv0_s = jnp.pad(state_conv[0], ((0, 0), (8 - tail, 0), (0, 0)))
    y_s, conv_s, st_s = ssd(conv0=conv0_s, st0=to_state_t(state_ssm[0]), row0=S, batch=DEC_BATCH, nc=1, T=DEC_SEQ)
    ym_p, ym_s = mem_attn(q_mem, 0, 0)
    h = matmul_out(y_p, y_s, ym_p, ym_s, wout0, xp, xs, 0, 512)
    act, wd0 = _ffn_act(h, g_ffn[0], w_ffn_gate, w_ffn_up, w_ffn_down, 0)
    h = ffn_down(act, wd0, h, row0=0, rows=N_ROWS, tm=ROW_TILE, tn=1024, tk=2816)

    hkv, hn = rmsnorm(h, jnp.stack([g_kv, g_mix[1]]))
    kvw = B_KV_HEADS * B_HEAD_DIM
    kv = matmul_wres(hkv, [w_kv[None]], 0, 2 * kvw, F32, WRES_ROW_TILE, 1024)
    k32, k16, v16 = kv_post(kv, k_norm_kv, B_HEAD_DIM, pad_rows=BAND_PAST)
    u, wout1 = matmul_wres(hn, [w_in_b], 0, B_Q_WIDTH + MEM_WIDTH, BF16, WRES_ROW_TILE, 1024,
                           side_cast=(w_out, 1, w_out.shape[1], 64))

    yb_p = band_attention(u, k16[None], v16[None], rel_bias[0], q_norm_b[0], row0=0, batch=1, nc=S // CHUNK,
                          T=CHUNK, KB=BAND_PAST + CHUNK, pad=BAND_PAST, cps=BAND_CHUNKS_PER_STEP)

    wc = cache_kv_k.shape[1]
    new_rows = lambda a: a[BAND_PAST + S:].reshape(DEC_BATCH, DEC_SEQ, kvw)
    ks = jnp.concatenate([bf(cache_kv_k).reshape(DEC_BATCH, wc, kvw), new_rows(k16)], axis=1)
    vs = jnp.concatenate([bf(cache_kv_v).reshape(DEC_BATCH, wc, kvw), new_rows(v16)], axis=1)
    yb_s = band_attention(u, ks, vs, rel_bias[0], q_norm_b[0],
                          row0=S, batch=DEC_BATCH, nc=1, T=DEC_SEQ, KB=wc + DEC_SEQ, pad=0, cps=1)
    ym_p, ym_s = mem_attn(u, B_Q_WIDTH // MEM_WIDTH, 1)
    h = matmul_out(yb_p, yb_s, ym_p, ym_s, wout1, h, h, N_ROWS // SUB_ROWS - 1, 512)
    act, wd1 = _ffn_act(h, g_ffn[1], w_ffn_gate, w_ffn_up, w_ffn_down, 1)
    y_prompt = ffn_down(act, wd1, h, row0=0, rows=S, tm=1024, tn=1024, tk=2816)
    y_sample = ffn_down(act, wd1, h, row0=S, rows=NS, tm=NS, tn=1024, tk=2816)

    keep = min(BAND_PAST, S)
    kv_shape = (B_KV_HEADS, B_HEAD_DIM)
    return (
        y_prompt.reshape(1, S, D_MODEL),
        y_sample.reshape(DEC_BATCH, DEC_SEQ, D_MODEL),
        from_state_t(st_p)[None],
        conv_p[:, 8 - tail:][None],
        k32[S - keep:S].reshape(1, keep, *kv_shape),
        kv[S - keep:S, kvw:].reshape(1, keep, *kv_shape),
        jnp.stack(mem_k32),
        jnp.stack(mem_v32),
        from_state_t(st_s)[None],
        conv_s[:, 8 - tail:][None],
        k32[S:].reshape(DEC_BATCH, DEC_SEQ, *kv_shape),
        kv[S:, kvw:].reshape(DEC_BATCH, DEC_SEQ, *kv_shape),
    )
```

```python
import functools
import math

import jax
import jax.numpy as jnp
from jax import lax
from jax.experimental import pallas as pl
from jax.experimental.pallas import tpu as pltpu

F32 = jnp.float32
BF16 = jnp.bfloat16

D_MODEL = 4096
SEQ = 8192
DEC_BATCH = 8
DEC_SEQ = 32
PAST_LEN = 2048
CHUNK = 64
RMS_EPS = 1e-6

SSM_HEAD_DIM = 64
SSM_INNER = 6144
SSM_HEADS = 96
SSM_PAIRS = SSM_HEADS // 2
SSM_GROUPS = 8
SSM_PPG = SSM_PAIRS // SSM_GROUPS
SSM_STATE = 128
CONV_WIDTH = 4
CONV_DIM = 8192
XBC_BLOCK = 2048

B_HEAD_DIM = 128
B_Q_WIDTH = 6144
B_HEADS = 48
B_KV_HEADS = 8
B_GQ = 6
BAND_PAST = 512
REL_CLIP = 64
BAND_CONST_KEYS = 384
BAND_CHUNKS_PER_STEP = 4

MEM_TOKENS = 256
MEM_WIDTH = 2048
MEM_HEADS = 4
MEM_HEAD_DIM = 512

D_FF = 11008
D_FF_PAD = 11264
DT_PAD = 128

N_SAMPLE_ROWS = DEC_BATCH * DEC_SEQ
N_ROWS = SEQ + N_SAMPLE_ROWS
ROW_TILE = 768
WRES_ROW_TILE = 384
SUB_ROWS = 256
SUBS_PER_TILE = ROW_TILE // SUB_ROWS

V7X_VMEM_LIMIT_BYTES = 59 * 1024 * 1024


def _params(*sem):
    return pltpu.CompilerParams(dimension_semantics=sem, vmem_limit_bytes=V7X_VMEM_LIMIT_BYTES)


def _sigmoid(x):
    return 1.0 / (1.0 + jnp.exp(-x))


def _softplus(x):
    return jnp.maximum(x, 0.0) + jnp.log1p(jnp.exp(-jnp.abs(x)))


def _dot(a, b):
    return jnp.dot(a, b, preferred_element_type=F32)


def _dot_nt(a, b):
    return lax.dot_general(a, b, (((1,), (1,)), ((), ())), preferred_element_type=F32)


def _dot_exact(a, b):
    return jnp.dot(a, b, precision=lax.Precision.HIGHEST, preferred_element_type=F32)


def _rms_scale(x):
    return x * lax.rsqrt(jnp.mean(x * x, axis=-1, keepdims=True) + RMS_EPS)


def _rmsnorm_body(x_ref, g_ref, *o_refs):
    xn = _rms_scale(x_ref[...])
    for j, o_ref in enumerate(o_refs):
        o_ref[...] = (xn * g_ref[j:j + 1, :]).astype(o_ref.dtype)


def rmsnorm(x, gains, tm=256):
    m, d = x.shape
    n = gains.shape[0]
    return pl.pallas_call(
        _rmsnorm_body,
        grid=(m // tm,),
        in_specs=[pl.BlockSpec((tm, d), lambda i: (i, 0)),
                  pl.BlockSpec((n, d), lambda i: (0, 0))],
        out_specs=[pl.BlockSpec((tm, d), lambda i: (i, 0))] * n,
        out_shape=[jax.ShapeDtypeStruct((m, d), BF16)] * n,
        compiler_params=_params("parallel"),
        name="rmsnorm",
    )(x, gains)


def _rmsnorm2_body(xp_ref, xs_ref, g_ref, o_ref, *, n_p):
    i = pl.program_id(0)

    @pl.when(i < n_p)
    def _():
        o_ref[...] = (_rms_scale(xp_ref[...]) * g_ref[...]).astype(o_ref.dtype)

    @pl.when(i >= n_p)
    def _():
        o_ref[...] = (_rms_scale(xs_ref[...]) * g_ref[...]).astype(o_ref.dtype)


def rmsnorm_rows2(xp, xs, gain):
    tm = SUB_ROWS
    d = xp.shape[1]
    n_p, n_s = xp.shape[0] // tm, xs.shape[0] // tm
    return pl.pallas_call(
        functools.partial(_rmsnorm2_body, n_p=n_p),
        grid=(n_p + n_s,),
        in_specs=[pl.BlockSpec((tm, d), lambda i: (jnp.minimum(i, n_p - 1), 0)),
                  pl.BlockSpec((tm, d), lambda i: (jnp.maximum(i - n_p, 0), 0)),
                  pl.BlockSpec((1, d), lambda i: (0, 0))],
        out_specs=pl.BlockSpec((tm, d), lambda i: (i, 0)),
        out_shape=jax.ShapeDtypeStruct((xp.shape[0] + xs.shape[0], d), BF16),
        compiler_params=_params("parallel"),
        name="rmsnorm_rows2",
    )(xp, xs, gain.reshape(1, d))


def _wres_body(x_ref, *refs, n_w, layer, col0, n_valid, tn, transposed, epilogue, side, normed):
    w_hbm, refs = refs[:n_w], refs[n_w:]
    side_w = ssq_ref = gain_ref = side_o = None
    if side:
        side_w, refs = refs[0], refs[1:]
    if normed:
        ssq_ref, gain_ref, refs = refs[0], refs[1], refs[2:]
    o_ref, refs = refs[0], refs[1:]
    if side:
        side_o, refs = refs[0], refs[1:]
    wbf, wf32, sem = refs[:n_w], refs[n_w:2 * n_w], refs[2 * n_w]
    j, i = pl.program_id(0), pl.program_id(1)
    nj = pl.num_programs(0)
    rem = n_valid % tn

    if side:
        rows_valid, rb, n_blocks = side
        blk = jnp.minimum(j * pl.num_programs(1) + i, n_blocks - 1)
        row = blk * rb + lax.broadcasted_iota(jnp.int32, side_w.shape, 0)
        side_o[...] = jnp.where(row < rows_valid, side_w[...], 0.0).astype(side_o.dtype)

    def for_each_copy(jj, slot, fn):
        def go(width):
            for w in range(n_w):
                if transposed:
                    src = w_hbm[w].at[layer, pl.ds(col0 + jj * tn, width), :]
                    dst = wf32[w].at[slot, pl.ds(0, width), :]
                else:
                    src = w_hbm[w].at[layer, :, pl.ds(col0 + jj * tn, width)]
                    dst = wf32[w].at[slot, :, pl.ds(0, width)]
                fn(pltpu.make_async_copy(src, dst, sem.at[w, slot]))
        if rem:
            pl.when(jj < nj - 1)(lambda: go(tn))
            pl.when(jj == nj - 1)(lambda: go(rem))
        else:
            go(tn)

    @pl.when(i == 0)
    def _():
        slot = j % 2

        start = lambda cp: cp.start(priority=1)

        @pl.when(j == 0)
        def _():
            for_each_copy(0, 0, start)

        for_each_copy(j, slot, lambda cp: cp.wait())

        @pl.when(j + 1 < nj)
        def _():
            for_each_copy(j + 1, 1 - slot, start)

        def convert(ragged):
            for w in range(n_w):
                wt = wf32[w][slot]
                if ragged:
                    col = lax.broadcasted_iota(jnp.int32, wt.shape, 0 if transposed else 1)
                    wt = jnp.where(col < rem, wt, 0.0)
                if normed:
                    wt = wt * gain_ref[...]
                wbf[w][...] = (wt.T if transposed else wt).astype(BF16)

        if rem:
            pl.when(j < nj - 1)(lambda: convert(False))
            pl.when(j == nj - 1)(lambda: convert(True))
        else:
            convert(False)

    x = x_ref[...]
    accs = [_dot(x, wb[...]) for wb in wbf]
    if normed:
        inv = lax.rsqrt(jnp.sum(ssq_ref[...], axis=-1, keepdims=True) / x.shape[1] + RMS_EPS)
        accs = [acc * inv for acc in accs]
    o_ref[...] = epilogue(*accs).astype(o_ref.dtype)


def matmul_wres(x, ws, layer, n_out, out_dtype, tm, tn, epilogue=lambda acc: acc, transposed=False, col0=0,
                side_cast=None, norm=None):
    m, k = x.shape
    n_w = len(ws)
    n_valid = min(n_out, ws[0].shape[1 if transposed else 2] - col0)
    nj, ni = n_out // tn, m // tm
    in_specs = [pl.BlockSpec((tm, k), lambda j, i: (i, 0))] + [pl.BlockSpec(memory_space=pl.ANY)] * n_w
    out_specs = [pl.BlockSpec((tm, tn), lambda j, i: (i, j))]
    out_shape = [jax.ShapeDtypeStruct((m, n_out), out_dtype)]
    args = [x, *ws]
    side = None
    if side_cast:
        w2, layer2, rows_out, rb = side_cast
        rows_valid, n2 = w2.shape[1], w2.shape[2]
        n_blocks, n_src_blocks = rows_out // rb, rows_valid // rb
        assert rows_out % rb == 0 and rows_valid % rb == 0 and n_blocks <= nj * ni
        side = (rows_valid, rb, n_blocks)
        in_specs.append(pl.BlockSpec((None, rb, n2), lambda j, i: (layer2, jnp.minimum(j * ni + i, n_src_blocks - 1), 0)))
        out_specs.append(pl.BlockSpec((rb, n2), lambda j, i: (jnp.minimum(j * ni + i, n_blocks - 1), 0)))
        out_shape.append(jax.ShapeDtypeStruct((rows_out, n2), BF16))
        args.append(w2)
    if norm:
        assert not transposed
        ssq, gain = norm
        in_specs += [pl.BlockSpec((tm, ssq.shape[1]), lambda j, i: (i, 0)),
                     pl.BlockSpec((k, 1), lambda j, i: (0, 0), pipeline_mode=pl.Buffered(1))]
        args += [ssq, gain.reshape(k, 1).astype(F32)]
    outs = pl.pallas_call(
        functools.partial(_wres_body, n_w=n_w, layer=layer, col0=col0, n_valid=n_valid, tn=tn,
                          transposed=transposed, epilogue=epilogue, side=side, normed=bool(norm)),
        grid=(nj, ni),
        in_specs=in_specs,
        out_specs=out_specs,
        out_shape=out_shape,
        scratch_shapes=[pltpu.VMEM((k, tn), BF16)] * n_w
                       + [pltpu.VMEM((2, tn, k) if transposed else (2, k, tn), F32)] * n_w
                       + [pltpu.SemaphoreType.DMA((n_w, 2))],
        compiler_params=_params("arbitrary", "arbitrary"),
        name="matmul_wres",
    )(*args)
    return outs if side_cast else outs[0]


def _lane_group_sumsq(h):
    return functools.reduce(lambda a, b: a + b, [h[:, c:c + 128] * h[:, c:c + 128] for c in range(0, h.shape[1], 128)])


def _mm_out_body(xap_ref, xas_ref, xbp_ref, xbs_ref, w_ref, rp_ref, rs_ref, o_ref, o16_ref, ssq_ref, *, ka):
    last_tile = pl.program_id(0) == pl.num_programs(0) - 1
    head = (SUBS_PER_TILE - 1) * SUB_ROWS

    def emit(rows, xa, xb, res):
        h = res + _dot(xa, w_ref[0:ka, :]) + _dot(xb, w_ref[ka:, :])
        o_ref[rows, :] = h
        o16_ref[rows, :] = h.astype(BF16)
        ssq_ref[rows, :] = _lane_group_sumsq(h)

    @pl.when(jnp.logical_not(last_tile))
    def _():
        emit(slice(None), xap_ref[...].reshape(ROW_TILE, -1), xbp_ref[...].reshape(ROW_TILE, -1),
             rp_ref[...].reshape(ROW_TILE, -1))

    @pl.when(last_tile)
    def _():
        n_head = SUBS_PER_TILE - 1
        emit(slice(0, head), xap_ref[0:n_head].reshape(head, -1), xbp_ref[0:n_head].reshape(head, -1),
             rp_ref[0:n_head].reshape(head, -1))
        emit(slice(head, ROW_TILE), xas_ref[0], xbs_ref[0], rs_ref[0])


def matmul_out(xa_p, xa_s, xb_p, xb_s, w, r_p, r_s, r_s_block, tn):
    sub3 = lambda a: a.reshape(a.shape[0] // SUB_ROWS, SUB_ROWS, a.shape[1])
    ka, kb, n = xa_p.shape[1], xb_p.shape[1], w.shape[1]
    nt = N_ROWS // ROW_TILE
    spt = SUBS_PER_TILE
    once = dict(pipeline_mode=pl.Buffered(1))
    return pl.pallas_call(
        functools.partial(_mm_out_body, ka=ka),
        grid=(nt, n // tn),
        in_specs=[pl.BlockSpec((spt, SUB_ROWS, ka), lambda i, j: (i, 0, 0)),
                  pl.BlockSpec((1, SUB_ROWS, ka), lambda i, j: (0, 0, 0), **once),
                  pl.BlockSpec((spt, SUB_ROWS, kb), lambda i, j: (i, 0, 0)),
                  pl.BlockSpec((1, SUB_ROWS, kb), lambda i, j: (0, 0, 0), **once),
                  pl.BlockSpec((ka + kb, tn), lambda i, j: (0, j)),
                  pl.BlockSpec((spt, SUB_ROWS, tn), lambda i, j: (i, 0, j)),
                  pl.BlockSpec((1, SUB_ROWS, tn), lambda i, j: (r_s_block, 0, j))],
        out_specs=[pl.BlockSpec((ROW_TILE, tn), lambda i, j: (i, j)),
                   pl.BlockSpec((ROW_TILE, tn), lambda i, j: (i, j)),
                   pl.BlockSpec((ROW_TILE, 128), lambda i, j: (i, j))],
        out_shape=[jax.ShapeDtypeStruct((N_ROWS, n), F32),
                   jax.ShapeDtypeStruct((N_ROWS, n), BF16),
                   jax.ShapeDtypeStruct((N_ROWS, 128 * (n // tn)), F32)],
        compiler_params=_params("parallel", "arbitrary"),
        name="matmul_out",
    )(sub3(xa_p), sub3(xa_s), sub3(xb_p), sub3(xb_s), w, sub3(r_p), sub3(r_s))


def _silu_gate(g, u):
    return g * _sigmoid(g) * u


def _ffn_down_body(x_ref, w_ref, r_ref, o_ref, *rest):
    acc_ref, stat_refs = rest[-1], rest[:-1]
    k = pl.program_id(2)

    @pl.when(k == 0)
    def _():
        acc_ref[...] = r_ref[...]

    acc_ref[...] += _dot(x_ref[...], w_ref[...])

    @pl.when(k == pl.num_programs(2) - 1)
    def _():
        h = acc_ref[...]
        o_ref[...] = h
        if stat_refs:
            o16_ref, ssq_ref = stat_refs
            o16_ref[...] = h.astype(BF16)
            ssq_ref[...] = _lane_group_sumsq(h)


def ffn_down(x, w, res, *, row0, rows, tm, tn, tk, norm_stats=False):
    kk = x.shape[1]
    n = w.shape[1]
    r0 = row0 // tm
    out_specs = [pl.BlockSpec((tm, tn), lambda i, j, k: (i, j))]
    out_shape = [jax.ShapeDtypeStruct((rows, n), F32)]
    if norm_stats:
        out_specs += [pl.BlockSpec((tm, tn), lambda i, j, k: (i, j)), pl.BlockSpec((tm, 128), lambda i, j, k: (i, j))]
        out_shape += [jax.ShapeDtypeStruct((rows, n), BF16), jax.ShapeDtypeStruct((rows, 128 * (n // tn)), F32)]
    outs = pl.pallas_call(
        _ffn_down_body,
        grid=(rows // tm, n // tn, kk // tk),
        in_specs=[pl.BlockSpec((tm, tk), lambda i, j, k: (r0 + i, k)),
                  pl.BlockSpec((tk, tn), lambda i, j, k: (k, j)),
                  pl.BlockSpec((tm, tn), lambda i, j, k: (r0 + i, j))],
        out_specs=out_specs,
        out_shape=out_shape,
        scratch_shapes=[pltpu.VMEM((tm, tn), F32)],
        compiler_params=_params("parallel", "arbitrary", "arbitrary"),
        name="ffn_down",
    )(x, w, res)
    return outs if norm_stats else outs[0]


def _kv_post_body(kv_ref, g_ref, k32_ref, k16_ref, v16_ref, *, width, head_dim, pad_blocks):
    keep = pl.program_id(0) >= pad_blocks
    for h in range(width // head_dim):
        sl = slice(h * head_dim, (h + 1) * head_dim)
        kn = _rms_scale(kv_ref[:, sl]) * g_ref[...]
        k32_ref[:, sl] = kn
        k16_ref[:, sl] = jnp.where(keep, kn, 0.0).astype(BF16)
    v16_ref[...] = jnp.where(keep, kv_ref[:, width:], 0.0).astype(BF16)


def kv_post(kv, gain, head_dim, pad_rows=0, tm=256):
    m, w2 = kv.shape
    width = w2 // 2
    pb = pad_rows // tm
    src = lambda i: (jnp.maximum(i - pb, 0), 0)
    return pl.pallas_call(
        functools.partial(_kv_post_body, width=width, head_dim=head_dim, pad_blocks=pb),
        grid=(pb + m // tm,),
        in_specs=[pl.BlockSpec((tm, w2), src),
                  pl.BlockSpec((1, head_dim), lambda i: (0, 0))],
        out_specs=[pl.BlockSpec((tm, width), src)] + [pl.BlockSpec((tm, width), lambda i: (i, 0))] * 2,
        out_shape=[jax.ShapeDtypeStruct((m, width), F32),
                   jax.ShapeDtypeStruct((pad_rows + m, width), BF16),
                   jax.ShapeDtypeStruct((pad_rows + m, width), BF16)],
        compiler_params=_params("arbitrary"),
        name="kv_post",
    )(kv, gain.reshape(1, head_dim))


def _ssd_body(z_ref, xbc0_ref, xbc1_ref, xbc2_ref, xbc3_ref, dt_ref, dtT_ref, conv0_ref, st0_ref, cw_ref, cb_ref,
              dtb_ref, dtbT_ref, alog_ref, alogT_ref, dskip_ref, gssm_ref,
              y_ref, convn_ref, stn_ref,
              xbuf, xc, st, yacc, *, T):
    c = pl.program_id(1)
    nc = pl.num_programs(1)
    P, N = SSM_HEAD_DIM, SSM_STATE
    T2, PW = 2 * T, 2 * P
    tail = CONV_WIDTH - 1

    @pl.when(c == 0)
    def _():
        xbuf[0:8, :] = conv0_ref[0]
        st[...] = st0_ref[0]

    for q, xbc_ref in enumerate((xbc0_ref, xbc1_ref, xbc2_ref, xbc3_ref)):
        xbuf[8:8 + T, q * XBC_BLOCK:(q + 1) * XBC_BLOCK] = xbc_ref[...].astype(F32)
    acc = cb_ref[...] + cw_ref[0:1, :] * xbuf[8 - tail:8 - tail + T, :]
    for tap in range(1, CONV_WIDTH):
        acc = acc + cw_ref[tap:tap + 1, :] * xbuf[8 - tail + tap:8 - tail + tap + T, :]
    xc[...] = acc * _sigmoid(acc)
    last_rows = xbuf[T:T + 8, :]
    convn_ref[0] = last_rows
    xbuf[0:8, :] = last_rows

    iota = lambda shape, d: lax.broadcasted_iota(jnp.int32, shape, d)
    tri = iota((T, T), 1) <= iota((T, T), 0)
    dt = _softplus(dt_ref[:, 0:SSM_HEADS] + dtb_ref[...])
    cum = _dot_exact(tri.astype(F32), dt * (-jnp.exp(alog_ref[...])))

    half = lambda x, size: jnp.where(x >= size, 1, 0)
    ra, rb = iota((T2, T2), 0), iota((T2, T2), 1)
    same = half(ra, T) == half(rb, T)
    dtT = _softplus(dtT_ref[0] + dtbT_ref[...])
    dtaT = dtT * (-jnp.exp(alogT_ref[...]))
    cumT = _dot_exact(dtaT, jnp.where(same & (ra <= rb), 1.0, 0.0))
    lastT = _dot_exact(dtaT, jnp.where(same, 1.0, 0.0))
    same_p = half(iota((T2, PW), 0), T) == half(iota((T2, PW), 1), P)
    edec = jnp.exp(_dot_exact(dtaT, jnp.where(same_p, 1.0, 0.0)))
    wT = dtT * jnp.exp(lastT - cumT)

    left = iota((T, PW), 1) < P
    left2 = iota((T, T2), 1) < T
    tri2 = (iota((T, T2), 1) - T * half(iota((T, T2), 1), T)) <= iota((T, T2), 0)
    bc_off = SSM_INNER
    cc_off = SSM_INNER + SSM_GROUPS * N

    for g in range(SSM_GROUPS):
        bg = xc[:, bc_off + g * N:bc_off + (g + 1) * N]
        cg16 = xc[:, cc_off + g * N:cc_off + (g + 1) * N].astype(BF16)
        bg2 = jnp.concatenate([bg, bg], axis=0)
        cb2 = _dot_nt(cg16, bg2.astype(BF16))
        bgT2 = bg2.T
        gw = SSM_PPG * PW
        yoff = _dot(cg16, st[:, g * gw:(g + 1) * gw].astype(BF16))
        for k in range(SSM_PPG):
            q = g * SSM_PPG + k
            ps = slice(q * PW, (q + 1) * PW)
            b0 = jnp.broadcast_to(cum[:, 2 * q:2 * q + 1], (T, PW))
            b1 = jnp.broadcast_to(cum[:, 2 * q + 1:2 * q + 2], (T, PW))
            col_p = jnp.where(left, b0, b1)
            col_2 = col_p if T2 == PW else jnp.where(left2, b0[:, 0:T2], b1[:, 0:T2])
            dec = jnp.where(tri2, jnp.exp(col_2 - cumT[q:q + 1, :]), 0.0)
            m2 = (cb2 * dec * dtT[q:q + 1, :]).astype(BF16)
            xp = xc[:, ps]
            bd = jnp.concatenate([jnp.where(left, xp, 0.0), jnp.where(left, 0.0, xp)],
                                 axis=0).astype(BF16)
            yacc[:, ps] = _dot(m2, bd) + yoff[:, k * PW:(k + 1) * PW] * jnp.exp(col_p)
            btw = (bgT2 * wT[q:q + 1, :]).astype(BF16)
            st[:, ps] = st[:, ps] * edec[q:q + 1, :] + _dot(btw, bd)

    y = yacc[...] + xc[:, 0:SSM_INNER] * dskip_ref[...]
    z = z_ref[...].astype(F32)
    yg = y * (z * _sigmoid(z))
    y_ref[...] = (_rms_scale(yg) * gssm_ref[...]).astype(y_ref.dtype)

    @pl.when(c == nc - 1)
    def _():
        stn_ref[0] = st[...]


def ssd_mixer(u, dt_raw, conv0, st0, conv_w, conv_b, dt_bias, a_log, d_skip, g_ssm,
              *, row0, batch, nc, T):
    rows = batch * nc * T
    r0 = row0 // T
    nblk = batch * nc
    dtT = (dt_raw[row0:row0 + rows, :SSM_HEADS].reshape(nblk, T, SSM_PAIRS, 2)
           .transpose(0, 2, 3, 1).reshape(nblk, SSM_PAIRS, 2 * T))
    pair_rows = lambda v: jnp.repeat(v.astype(F32).reshape(SSM_PAIRS, 2), T, axis=1)
    row = lambda v: v.reshape(1, -1).astype(F32)
    const = lambda shape: pl.BlockSpec(shape, lambda b, c: (0,) * len(shape))
    xbc_specs = [pl.BlockSpec((T, XBC_BLOCK), functools.partial(
        lambda b, c, q: (r0 + b * nc + c, SSM_INNER // XBC_BLOCK + q), q=q)) for q in range(CONV_DIM // XBC_BLOCK)]
    return pl.pallas_call(
        functools.partial(_ssd_body, T=T),
        grid=(batch, nc),
        in_specs=[
            pl.BlockSpec((T, SSM_INNER), lambda b, c: (r0 + b * nc + c, 0)),
            *xbc_specs,
            pl.BlockSpec((T, DT_PAD), lambda b, c: (r0 + b * nc + c, 0)),
            pl.BlockSpec((1, SSM_PAIRS, 2 * T), lambda b, c: (b * nc + c, 0, 0)),
            pl.BlockSpec((1, 8, CONV_DIM), lambda b, c: (b, 0, 0)),
            pl.BlockSpec((1, SSM_STATE, SSM_INNER), lambda b, c: (b, 0, 0)),
            const((CONV_WIDTH, CONV_DIM)), const((1, CONV_DIM)),
            const((1, SSM_HEADS)), const((SSM_PAIRS, 2 * T)),
            const((1, SSM_HEADS)), const((SSM_PAIRS, 2 * T)),
            const((1, SSM_INNER)), const((1, SSM_INNER)),
        ],
        out_specs=[
            pl.BlockSpec((T, SSM_INNER), lambda b, c: (b * nc + c, 0)),
            pl.BlockSpec((1, 8, CONV_DIM), lambda b, c: (b, 0, 0)),
            pl.BlockSpec((1, SSM_STATE, SSM_INNER), lambda b, c: (b, 0, 0)),
        ],
        out_shape=[
            jax.ShapeDtypeStruct((rows, SSM_INNER), BF16),
            jax.ShapeDtypeStruct((batch, 8, CONV_DIM), F32),
            jax.ShapeDtypeStruct((batch, SSM_STATE, SSM_INNER), F32),
        ],
        scratch_shapes=[
            pltpu.VMEM((8 + T, CONV_DIM), F32),
            pltpu.VMEM((T, CONV_DIM), F32),
            pltpu.VMEM((SSM_STATE, SSM_INNER), F32),
            pltpu.VMEM((T, SSM_INNER), F32),
        ],
        compiler_params=_params("parallel", "arbitrary"),
        name="ssd_mixer",
    )(u, u, u, u, u, dt_raw, dtT, conv0, st0, conv_w.T.astype(F32), row(conv_b),
      row(dt_bias), pair_rows(dt_bias), row(a_log), pair_rows(a_log),
      row(jnp.repeat(d_skip, SSM_HEAD_DIM)), row(g_ssm))


def _band_body(q_ref, k_ref, v_ref, tab_ref, qg_ref, o_ref, bias_var, *, T, KB, pad, cps):
    c = pl.program_id(2)
    KA = BAND_CONST_KEYS
    d = B_HEAD_DIM

    @pl.when(c == 0)
    def _():
        for j in range(B_GQ):
            tj = tab_ref[0, j:j + 1, :]
            band = pltpu.roll(jnp.broadcast_to(tj, (T, 2 * d)), 2 * d - REL_CLIP, 1,
                              stride=1, stride_axis=0)
            bias_var[j * T:(j + 1) * T, :] = band[:, 0:KB - KA] - tj[:, 0:1]

    scale = 1.0 / math.sqrt(d)

    def chunk(cl, masked):
        start = pl.multiple_of((c * cps + cl) * T, T)
        k = k_ref[0, pl.ds(start, KB), :]
        v = v_ref[0, pl.ds(start, KB), :]
        q = jnp.concatenate(
            [(_rms_scale(q_ref[cl * T:(cl + 1) * T, j * d:(j + 1) * d].astype(F32)) * qg_ref[...] * scale
              ).astype(BF16) for j in range(B_GQ)], axis=0)
        s = _dot_nt(q, k)
        sa = s[:, 0:KA]
        sb = s[:, KA:KB] + bias_var[...]
        if masked:
            sa = jnp.where(start + lax.broadcasted_iota(jnp.int32, (1, KA), 1) >= pad, sa, -jnp.inf)
            sb = jnp.where(start + KA + lax.broadcasted_iota(jnp.int32, (1, KB - KA), 1) >= pad, sb, -jnp.inf)
        m = jnp.maximum(jnp.max(sa, axis=-1, keepdims=True), jnp.max(sb, axis=-1, keepdims=True))
        pa = jnp.exp((sa - m).astype(BF16))
        pb = jnp.exp((sb - m).astype(BF16))
        v1 = jnp.concatenate([v, jnp.ones_like(v)], axis=1)
        ol = _dot(pa, v1[0:KA]) + _dot(pb, v1[KA:KB])
        o = ol[:, 0:d] / ol[:, d:2 * d]
        for j in range(B_GQ):
            o_ref[cl * T:(cl + 1) * T, j * d:(j + 1) * d] = o[j * T:(j + 1) * T].astype(o_ref.dtype)

    if pad:
        @pl.when(c * (cps * T) < pad)
        def _():
            for cl in range(cps):
                chunk(cl, True)

        @pl.when(c * (cps * T) >= pad)
        def _():
            for cl in range(cps):
                chunk(cl, False)
    else:
        for cl in range(cps):
            chunk(cl, False)


def band_attention(u, k, v, table, q_gain, *, row0, batch, nc, T, KB, pad, cps):
    rows = batch * nc * T
    r0 = row0 // (cps * T)
    ns = nc // cps
    ktot = (nc - 1) * T + KB
    gw = B_GQ * B_HEAD_DIM
    tab = table[:, jnp.clip(2 * B_HEAD_DIM - jnp.arange(2 * B_HEAD_DIM), 0, 2 * REL_CLIP)].astype(F32)
    tab = tab.reshape(B_KV_HEADS, B_GQ, 2 * B_HEAD_DIM)
    return pl.pallas_call(
        functools.partial(_band_body, T=T, KB=KB, pad=pad, cps=cps),
        grid=(batch, B_KV_HEADS, ns),
        in_specs=[
            pl.BlockSpec((cps * T, gw), lambda b, g, c: (r0 + b * ns + c, g)),
            pl.BlockSpec((1, ktot, B_HEAD_DIM), lambda b, g, c: (b, 0, g)),
            pl.BlockSpec((1, ktot, B_HEAD_DIM), lambda b, g, c: (b, 0, g)),
            pl.BlockSpec((1, B_GQ, 2 * B_HEAD_DIM), lambda b, g, c: (g, 0, 0)),
            pl.BlockSpec((1, B_HEAD_DIM), lambda b, g, c: (0, 0)),
        ],
        out_specs=pl.BlockSpec((cps * T, gw), lambda b, g, c: (b * ns + c, g)),
        out_shape=jax.ShapeDtypeStruct((rows, B_Q_WIDTH), BF16),
        scratch_shapes=[pltpu.VMEM((B_GQ * T, KB - BAND_CONST_KEYS), F32)],
        compiler_params=_params("parallel", "parallel", "arbitrary"),
        name="band_attention",
    )(u, k, v, tab, q_gain.reshape(1, B_HEAD_DIM).astype(F32))


def _mem_body(q_ref, k_ref, v_ref, qg_ref, o_ref, kb, vb):
    @pl.when(pl.program_id(1) == 0)
    def _():
        for h in range(MEM_HEADS):
            kb[h] = k_ref[:, h, :].astype(BF16)
            vb[h] = v_ref[:, h, :].astype(BF16)

    scale = 1.0 / math.sqrt(MEM_HEAD_DIM)
    for h in range(MEM_HEADS):
        sl = slice(h * MEM_HEAD_DIM, (h + 1) * MEM_HEAD_DIM)
        qn = (_rms_scale(q_ref[:, sl].astype(F32)) * qg_ref[...] * scale).astype(BF16)
        s = _dot_nt(qn, kb[h])
        m = jnp.max(s, axis=-1, keepdims=True)
        p = jnp.exp(s - m)
        l = jnp.sum(p, axis=-1, keepdims=True)
        o_ref[:, sl] = (_dot(p.astype(BF16), vb[h]) / l).astype(o_ref.dtype)


def memory_attention(u, k, v, q_gain, *, q_col_block, layer, row0, batch, nt, tq):
    rows = batch * nt * tq
    r0 = row0 // tq
    return pl.pallas_call(
        _mem_body,
        grid=(batch, nt),
        in_specs=[
            pl.BlockSpec((tq, MEM_WIDTH), lambda b, i: (r0 + b * nt + i, q_col_block)),
            pl.BlockSpec((None, None, MEM_TOKENS, MEM_HEADS, MEM_HEAD_DIM), lambda b, i: (layer, b, 0, 0, 0)),
            pl.BlockSpec((None, None, MEM_TOKENS, MEM_HEADS, MEM_HEAD_DIM), lambda b, i: (layer, b, 0, 0, 0)),
            pl.BlockSpec((1, MEM_HEAD_DIM), lambda b, i: (0, 0)),
        ],
        out_specs=pl.BlockSpec((tq, MEM_WIDTH), lambda b, i: (b * nt + i, 0)),
        out_shape=jax.ShapeDtypeStruct((rows, MEM_WIDTH), BF16),
        scratch_shapes=[pltpu.VMEM((MEM_HEADS, MEM_TOKENS, MEM_HEAD_DIM), BF16)] * 2,
        compiler_params=_params("parallel", "arbitrary"),
        name="memory_attention",
    )(u, k, v, q_gain.reshape(1, MEM_HEAD_DIM).astype(F32))


def _ffn_act(h16, ssq, g_ffn, w_gate, w_up, w_down, layer):
    return matmul_wres(h16, [w_gate, w_up], layer, D_FF_PAD, BF16, WRES_ROW_TILE, 512, _silu_gate,
                       side_cast=(w_down, layer, D_FF_PAD, 32), norm=(ssq, g_ffn))


def kernel(x_prompt, x_sample, mem_prompt, state_ssm, state_conv, cache_kv_k, cache_kv_v, cache_mem_k, cache_mem_v, g_mix, w_in_a, conv_w, conv_b, dt_bias, a_log, d_skip, g_ssm, w_in_b, rel_bias, q_norm_b, g_kv, w_kv, k_norm_kv, g_mem, w_mem_kv, q_norm_mem, k_norm_mem, w_out, g_ffn, w_ffn_gate, w_ffn_up, w_ffn_down):
    S, NS = SEQ, N_SAMPLE_ROWS
    bf = lambda w: w.astype(BF16)

    main_w = SSM_INNER + CONV_DIM
    w_in_a_t = jnp.swapaxes(w_in_a, 1, 2)
    in_a = functools.partial(matmul_wres, ws=[w_in_a_t], layer=0, tm=WRES_ROW_TILE, transposed=True)

    xp = x_prompt.reshape(S, D_MODEL)
    xs = x_sample.reshape(NS, D_MODEL)

    mem_k32, mem_v32 = [], []
    mem_x = mem_prompt.reshape(MEM_TOKENS, D_MODEL)
    mem_shape = (1, MEM_TOKENS, MEM_HEADS, MEM_HEAD_DIM)
    for l in range(2):
        (mn,) = rmsnorm(mem_x, g_mem[l].reshape(1, -1))
        kv = matmul_wres(mn, [w_mem_kv], l, 2 * MEM_WIDTH, F32, MEM_TOKENS, 1024)
        k32, _, _ = kv_post(kv, k_norm_mem[l], MEM_HEAD_DIM)
        mem_k32.append(k32.reshape(mem_shape))
        mem_v32.append(kv[:, MEM_WIDTH:].reshape(mem_shape))

    mem_k_prompt, mem_v_prompt = jnp.stack(mem_k32), jnp.stack(mem_v32)

    def mem_attn(u, q_col_block, l):
        attn = functools.partial(memory_attention, u, q_gain=q_norm_mem[l], q_col_block=q_col_block, layer=l)
        yp = attn(k=mem_k_prompt, v=mem_v_prompt, row0=0, batch=1, nt=S // 512, tq=512)
        ys = attn(k=cache_mem_k, v=cache_mem_v, row0=S, batch=DEC_BATCH, nt=1, tq=DEC_SEQ)
        return yp, ys

    hn = rmsnorm_rows2(xp, xs, g_mix[0])
    u, wout0 = in_a(hn, n_out=main_w, out_dtype=BF16, tn=1024,
                    side_cast=(w_out, 0, w_out.shape[1], 32))
    dt_raw = in_a(hn, n_out=DT_PAD, out_dtype=F32, tn=DT_PAD, col0=main_w)
    q_mem = in_a(hn, n_out=MEM_WIDTH, out_dtype=BF16, tn=1024, col0=main_w + SSM_HEADS)

    def to_state_t(s):
        b = s.shape[0]
        return s.reshape(b, SSM_INNER, SSM_STATE).transpose(0, 2, 1)

    def from_state_t(s):
        b = s.shape[0]
        return s.transpose(0, 2, 1).reshape(b, SSM_HEADS, SSM_HEAD_DIM, SSM_STATE)

    ssd = functools.partial(ssd_mixer, u, dt_raw, conv_w=conv_w[0], conv_b=conv_b[0], dt_bias=dt_bias[0],
                            a_log=a_log[0], d_skip=d_skip[0], g_ssm=g_ssm[0])
    tail = CONV_WIDTH - 1
    conv0_p = jnp.zeros((1, 8, CONV_DIM), F32)
    st0_p = jnp.zeros((1, SSM_STATE, SSM_INNER), F32)
    y_p, conv_p, st_p = ssd(conv0=conv0_p, st0=st0_p, row0=0, batch=1, nc=S // CHUNK, T=CHUNK)
    conv0_s = jnp.pad(state_conv[0], ((0, 0), (8 - tail, 0), (0, 0)))
    y_s, conv_s, st_s = ssd(conv0=conv0_s, st0=to_state_t(state_ssm[0]), row0=S, batch=DEC_BATCH, nc=1, T=DEC_SEQ)
    ym_p, ym_s = mem_attn(q_mem, 0, 0)
    h, h16, ssq = matmul_out(y_p, y_s, ym_p, ym_s, wout0, xp, xs, 0, 512)
    act, wd0 = _ffn_act(h16, ssq, g_ffn[0], w_ffn_gate, w_ffn_up, w_ffn_down, 0)
    h, h16, ssq = ffn_down(act, wd0, h, row0=0, rows=N_ROWS, tm=ROW_TILE, tn=1024, tk=2816, norm_stats=True)

    kvw = B_KV_HEADS * B_HEAD_DIM
    kv = matmul_wres(h16, [w_kv[None]], 0, 2 * kvw, F32, WRES_ROW_TILE, 1024, norm=(ssq, g_kv))
    k32, k16, v16 = kv_post(kv, k_norm_kv, B_HEAD_DIM, pad_rows=BAND_PAST)
    u, wout1 = matmul_wres(h16, [w_in_b], 0, B_Q_WIDTH + MEM_WIDTH, BF16, WRES_ROW_TILE, 1024,
                           side_cast=(w_out, 1, w_out.shape[1], 64), norm=(ssq, g_mix[1]))

    yb_p = band_attention(u, k16[None], v16[None], rel_bias[0], q_norm_b[0], row0=0, batch=1, nc=S // CHUNK,
                          T=CHUNK, KB=BAND_PAST + CHUNK, pad=BAND_PAST, cps=BAND_CHUNKS_PER_STEP)

    wc = cache_kv_k.shape[1]
    new_rows = lambda a: a[BAND_PAST + S:].reshape(DEC_BATCH, DEC_SEQ, kvw)
    ks = jnp.concatenate([bf(cache_kv_k).reshape(DEC_BATCH, wc, kvw), new_rows(k16)], axis=1)
    vs = jnp.concatenate([bf(cache_kv_v).reshape(DEC_BATCH, wc, kvw), new_rows(v16)], axis=1)
    yb_s = band_attention(u, ks, vs, rel_bias[0], q_norm_b[0],
                          row0=S, batch=DEC_BATCH, nc=1, T=DEC_SEQ, KB=wc + DEC_SEQ, pad=0, cps=1)
    ym_p, ym_s = mem_attn(u, B_Q_WIDTH // MEM_WIDTH, 1)
    h, h16, ssq = matmul_out(yb_p, yb_s, ym_p, ym_s, wout1, h, h, N_ROWS // SUB_ROWS - 1, 512)
    act, wd1 = _ffn_act(h16, ssq, g_ffn[1], w_ffn_gate, w_ffn_up, w_ffn_down, 1)
    y_prompt = ffn_down(act, wd1, h, row0=0, rows=S, tm=1024, tn=1024, tk=2816)
    y_sample = ffn_down(act, wd1, h, row0=S, rows=NS, tm=NS, tn=1024, tk=2816)

    keep = min(BAND_PAST, S)
    kv_shape = (B_KV_HEADS, B_HEAD_DIM)
    return (
        y_prompt.reshape(1, S, D_MODEL),
        y_sample.reshape(DEC_BATCH, DEC_SEQ, D_MODEL),
        from_state_t(st_p)[None],
        conv_p[:, 8 - tail:][None],
        k32[S - keep:S].reshape(1, keep, *kv_shape),
        kv[S - keep:S, kvw:].reshape(1, keep, *kv_shape),
        mem_k_prompt,
        mem_v_prompt,
        from_state_t(st_s)[None],
        conv_s[:, 8 - tail:][None],
        k32[S:].reshape(DEC_BATCH, DEC_SEQ, *kv_shape),
        kv[S:, kvw:].reshape(DEC_BATCH, DEC_SEQ, *kv_shape),
    )
```

```python
import functools
import math

import jax
import jax.numpy as jnp
from jax import lax
from jax.experimental import pallas as pl
from jax.experimental.pallas import tpu as pltpu

F32 = jnp.float32
BF16 = jnp.bfloat16

D_MODEL = 4096
SEQ = 8192
DEC_BATCH = 8
DEC_SEQ = 32
PAST_LEN = 2048
CHUNK = 64
RMS_EPS = 1e-6

SSM_HEAD_DIM = 64
SSM_INNER = 6144
SSM_HEADS = 96
SSM_PAIRS = SSM_HEADS // 2
SSM_GROUPS = 8
SSM_PPG = SSM_PAIRS // SSM_GROUPS
SSM_STATE = 128
CONV_WIDTH = 4
CONV_DIM = 8192
XBC_BLOCK = 2048

B_HEAD_DIM = 128
B_Q_WIDTH = 6144
B_HEADS = 48
B_KV_HEADS = 8
B_GQ = 6
BAND_PAST = 512
REL_CLIP = 64
BAND_CONST_KEYS = 384
BAND_CHUNKS_PER_STEP = 8

MEM_TOKENS = 256
MEM_WIDTH = 2048
MEM_HEADS = 4
MEM_HEAD_DIM = 512

D_FF = 11008
D_FF_PAD = 11264
DT_PAD = 128

N_SAMPLE_ROWS = DEC_BATCH * DEC_SEQ
N_ROWS = SEQ + N_SAMPLE_ROWS
ROW_TILE = 768
WRES_ROW_TILE = 384
SUB_ROWS = 256
SUBS_PER_TILE = ROW_TILE // SUB_ROWS

V7X_VMEM_LIMIT_BYTES = 59 * 1024 * 1024


def _params(*sem):
    return pltpu.CompilerParams(dimension_semantics=sem, vmem_limit_bytes=V7X_VMEM_LIMIT_BYTES)


def _sigmoid(x):
    return 1.0 / (1.0 + jnp.exp(-x))


def _softplus(x):
    return jnp.maximum(x, 0.0) + jnp.log1p(jnp.exp(-jnp.abs(x)))


def _dot(a, b):
    return jnp.dot(a, b, preferred_element_type=F32)


def _dot_nt(a, b):
    return lax.dot_general(a, b, (((1,), (1,)), ((), ())), preferred_element_type=F32)


def _dot_exact(a, b):
    return jnp.dot(a, b, precision=lax.Precision.HIGHEST, preferred_element_type=F32)


def _rms_scale(x):
    return x * lax.rsqrt(jnp.mean(x * x, axis=-1, keepdims=True) + RMS_EPS)


def _rmsnorm_body(x_ref, g_ref, *o_refs):
    xn = _rms_scale(x_ref[...])
    for j, o_ref in enumerate(o_refs):
        o_ref[...] = (xn * g_ref[j:j + 1, :]).astype(o_ref.dtype)


def rmsnorm(x, gains, tm=256):
    m, d = x.shape
    n = gains.shape[0]
    return pl.pallas_call(
        _rmsnorm_body,
        grid=(m // tm,),
        in_specs=[pl.BlockSpec((tm, d), lambda i: (i, 0)),
                  pl.BlockSpec((n, d), lambda i: (0, 0))],
        out_specs=[pl.BlockSpec((tm, d), lambda i: (i, 0))] * n,
        out_shape=[jax.ShapeDtypeStruct((m, d), BF16)] * n,
        compiler_params=_params("parallel"),
        name="rmsnorm",
    )(x, gains)


def _rmsnorm2_body(xp_ref, xs_ref, g_ref, o_ref, *, n_p):
    i = pl.program_id(0)

    @pl.when(i < n_p)
    def _():
        o_ref[...] = (_rms_scale(xp_ref[...]) * g_ref[...]).astype(o_ref.dtype)

    @pl.when(i >= n_p)
    def _():
        o_ref[...] = (_rms_scale(xs_ref[...]) * g_ref[...]).astype(o_ref.dtype)


def rmsnorm_rows2(xp, xs, gain):
    tm = SUB_ROWS
    d = xp.shape[1]
    n_p, n_s = xp.shape[0] // tm, xs.shape[0] // tm
    return pl.pallas_call(
        functools.partial(_rmsnorm2_body, n_p=n_p),
        grid=(n_p + n_s,),
        in_specs=[pl.BlockSpec((tm, d), lambda i: (jnp.minimum(i, n_p - 1), 0)),
                  pl.BlockSpec((tm, d), lambda i: (jnp.maximum(i - n_p, 0), 0)),
                  pl.BlockSpec((1, d), lambda i: (0, 0))],
        out_specs=pl.BlockSpec((tm, d), lambda i: (i, 0)),
        out_shape=jax.ShapeDtypeStruct((xp.shape[0] + xs.shape[0], d), BF16),
        compiler_params=_params("parallel"),
        name="rmsnorm_rows2",
    )(xp, xs, gain.reshape(1, d))


def _wres_body(x_ref, *refs, n_w, layer, col0, n_valid, tn, transposed, epilogue, side, normed):
    w_hbm, refs = refs[:n_w], refs[n_w:]
    side_w = ssq_ref = gain_ref = side_o = None
    if side:
        side_w, refs = refs[0], refs[1:]
    if normed:
        ssq_ref, gain_ref, refs = refs[0], refs[1], refs[2:]
    o_ref, refs = refs[0], refs[1:]
    if side:
        side_o, refs = refs[0], refs[1:]
    wbf, wf32, sem = refs[:n_w], refs[n_w:2 * n_w], refs[2 * n_w]
    j, i = pl.program_id(0), pl.program_id(1)
    nj = pl.num_programs(0)
    rem = n_valid % tn

    if side:
        rows_valid, rb, n_blocks = side
        blk = jnp.minimum(j * pl.num_programs(1) + i, n_blocks - 1)
        row = blk * rb + lax.broadcasted_iota(jnp.int32, side_w.shape, 0)
        side_o[...] = jnp.where(row < rows_valid, side_w[...], 0.0).astype(side_o.dtype)

    def for_each_copy(jj, slot, fn):
        def go(width):
            for w in range(n_w):
                if transposed:
                    src = w_hbm[w].at[layer, pl.ds(col0 + jj * tn, width), :]
                    dst = wf32[w].at[slot, pl.ds(0, width), :]
                else:
                    src = w_hbm[w].at[layer, :, pl.ds(col0 + jj * tn, width)]
                    dst = wf32[w].at[slot, :, pl.ds(0, width)]
                fn(pltpu.make_async_copy(src, dst, sem.at[w, slot]))
        if rem:
            pl.when(jj < nj - 1)(lambda: go(tn))
            pl.when(jj == nj - 1)(lambda: go(rem))
        else:
            go(tn)

    @pl.when(i == 0)
    def _():
        slot = j % 2

        start = lambda cp: cp.start(priority=1)

        @pl.when(j == 0)
        def _():
            for_each_copy(0, 0, start)

        for_each_copy(j, slot, lambda cp: cp.wait())

        @pl.when(j + 1 < nj)
        def _():
            for_each_copy(j + 1, 1 - slot, start)

        def convert(ragged):
            for w in range(n_w):
                wt = wf32[w][slot]
                if ragged:
                    col = lax.broadcasted_iota(jnp.int32, wt.shape, 0 if transposed else 1)
                    wt = jnp.where(col < rem, wt, 0.0)
                if normed:
                    wt = wt * jnp.concatenate([gain_ref[...]] * (tn // 128), axis=1)
                wbf[w][...] = (wt.T if transposed else wt).astype(BF16)

        if rem:
            pl.when(j < nj - 1)(lambda: convert(False))
            pl.when(j == nj - 1)(lambda: convert(True))
        else:
            convert(False)

    x = x_ref[...]
    accs = [_dot(x, wb[...]) for wb in wbf]
    if normed:
        inv = lax.rsqrt(jnp.sum(ssq_ref[...], axis=-1, keepdims=True) / x.shape[1] + RMS_EPS)
        accs = [acc * inv for acc in accs]
    o_ref[...] = epilogue(*accs).astype(o_ref.dtype)


def matmul_wres(x, ws, layer, n_out, out_dtype, tm, tn, epilogue=lambda acc: acc, transposed=False, col0=0,
                side_cast=None, norm=None):
    m, k = x.shape
    n_w = len(ws)
    n_valid = min(n_out, ws[0].shape[1 if transposed else 2] - col0)
    nj, ni = n_out // tn, m // tm
    in_specs = [pl.BlockSpec((tm, k), lambda j, i: (i, 0))] + [pl.BlockSpec(memory_space=pl.ANY)] * n_w
    out_specs = [pl.BlockSpec((tm, tn), lambda j, i: (i, j))]
    out_shape = [jax.ShapeDtypeStruct((m, n_out), out_dtype)]
    args = [x, *ws]
    side = None
    if side_cast:
        w2, layer2, rows_out, rb = side_cast
        rows_valid, n2 = w2.shape[1], w2.shape[2]
        n_blocks, n_src_blocks = rows_out // rb, rows_valid // rb
        assert rows_out % rb == 0 and rows_valid % rb == 0 and n_blocks <= nj * ni
        side = (rows_valid, rb, n_blocks)
        in_specs.append(pl.BlockSpec((None, rb, n2), lambda j, i: (layer2, jnp.minimum(j * ni + i, n_src_blocks - 1), 0)))
        out_specs.append(pl.BlockSpec((rb, n2), lambda j, i: (jnp.minimum(j * ni + i, n_blocks - 1), 0)))
        out_shape.append(jax.ShapeDtypeStruct((rows_out, n2), BF16))
        args.append(w2)
    if norm:
        assert not transposed
        ssq, gain = norm
        in_specs += [pl.BlockSpec((tm, ssq.shape[1]), lambda j, i: (i, 0)),
                     pl.BlockSpec((k, 128), lambda j, i: (0, 0), pipeline_mode=pl.Buffered(1))]
        args += [ssq, jnp.broadcast_to(gain.astype(F32)[:, None], (k, 128))]
    outs = pl.pallas_call(
        functools.partial(_wres_body, n_w=n_w, layer=layer, col0=col0, n_valid=n_valid, tn=tn,
                          transposed=transposed, epilogue=epilogue, side=side, normed=bool(norm)),
        grid=(nj, ni),
        in_specs=in_specs,
        out_specs=out_specs,
        out_shape=out_shape,
        scratch_shapes=[pltpu.VMEM((k, tn), BF16)] * n_w
                       + [pltpu.VMEM((2, tn, k) if transposed else (2, k, tn), F32)] * n_w
                       + [pltpu.SemaphoreType.DMA((n_w, 2))],
        compiler_params=_params("arbitrary", "arbitrary"),
        name="matmul_wres",
    )(*args)
    return outs if side_cast else outs[0]


def _lane_group_sumsq(h):
    return functools.reduce(lambda a, b: a + b, [h[:, c:c + 128] * h[:, c:c + 128] for c in range(0, h.shape[1], 128)])


def _mm_out_body(xap_ref, xas_ref, xbp_ref, xbs_ref, w_ref, rp_ref, rs_ref, o_ref, o16_ref, ssq_ref, *, ka):
    last_tile = pl.program_id(0) == pl.num_programs(0) - 1
    head = (SUBS_PER_TILE - 1) * SUB_ROWS

    def emit(rows, xa, xb, res):
        h = res + _dot(xa, w_ref[0:ka, :]) + _dot(xb, w_ref[ka:, :])
        o_ref[rows, :] = h
        o16_ref[rows, :] = h.astype(BF16)
        ssq_ref[rows, :] = _lane_group_sumsq(h)

    @pl.when(jnp.logical_not(last_tile))
    def _():
        emit(slice(None), xap_ref[...].reshape(ROW_TILE, -1), xbp_ref[...].reshape(ROW_TILE, -1),
             rp_ref[...].reshape(ROW_TILE, -1))

    @pl.when(last_tile)
    def _():
        n_head = SUBS_PER_TILE - 1
        emit(slice(0, head), xap_ref[0:n_head].reshape(head, -1), xbp_ref[0:n_head].reshape(head, -1),
             rp_ref[0:n_head].reshape(head, -1))
        emit(slice(head, ROW_TILE), xas_ref[0], xbs_ref[0], rs_ref[0])


def matmul_out(xa_p, xa_s, xb_p, xb_s, w, r_p, r_s, r_s_block, tn):
    sub3 = lambda a: a.reshape(a.shape[0] // SUB_ROWS, SUB_ROWS, a.shape[1])
    ka, kb, n = xa_p.shape[1], xb_p.shape[1], w.shape[1]
    nt = N_ROWS // ROW_TILE
    spt = SUBS_PER_TILE
    once = dict(pipeline_mode=pl.Buffered(1))
    return pl.pallas_call(
        functools.partial(_mm_out_body, ka=ka),
        grid=(nt, n // tn),
        in_specs=[pl.BlockSpec((spt, SUB_ROWS, ka), lambda i, j: (i, 0, 0)),
                  pl.BlockSpec((1, SUB_ROWS, ka), lambda i, j: (0, 0, 0), **once),
                  pl.BlockSpec((spt, SUB_ROWS, kb), lambda i, j: (i, 0, 0)),
                  pl.BlockSpec((1, SUB_ROWS, kb), lambda i, j: (0, 0, 0), **once),
                  pl.BlockSpec((ka + kb, tn), lambda i, j: (0, j)),
                  pl.BlockSpec((spt, SUB_ROWS, tn), lambda i, j: (i, 0, j)),
                  pl.BlockSpec((1, SUB_ROWS, tn), lambda i, j: (r_s_block, 0, j))],
        out_specs=[pl.BlockSpec((ROW_TILE, tn), lambda i, j: (i, j)),
                   pl.BlockSpec((ROW_TILE, tn), lambda i, j: (i, j)),
                   pl.BlockSpec((ROW_TILE, 128), lambda i, j: (i, j))],
        out_shape=[jax.ShapeDtypeStruct((N_ROWS, n), F32),
                   jax.ShapeDtypeStruct((N_ROWS, n), BF16),
                   jax.ShapeDtypeStruct((N_ROWS, 128 * (n // tn)), F32)],
        compiler_params=_params("parallel", "arbitrary"),
        name="matmul_out",
    )(sub3(xa_p), sub3(xa_s), sub3(xb_p), sub3(xb_s), w, sub3(r_p), sub3(r_s))


def _silu_gate(g, u):
    return g * _sigmoid(g) * u


def _ffn_down_body(x_ref, w_ref, r_ref, o_ref, *rest):
    acc_ref, stat_refs = rest[-1], rest[:-1]
    k = pl.program_id(2)

    @pl.when(k == 0)
    def _():
        acc_ref[...] = r_ref[...]

    acc_ref[...] += _dot(x_ref[...], w_ref[...])

    @pl.when(k == pl.num_programs(2) - 1)
    def _():
        h = acc_ref[...]
        o_ref[...] = h
        if stat_refs:
            o16_ref, ssq_ref = stat_refs
            o16_ref[...] = h.astype(BF16)
            ssq_ref[...] = _lane_group_sumsq(h)


def ffn_down(x, w, res, *, row0, rows, tm, tn, tk, norm_stats=False):
    kk = x.shape[1]
    n = w.shape[1]
    r0 = row0 // tm
    out_specs = [pl.BlockSpec((tm, tn), lambda i, j, k: (i, j))]
    out_shape = [jax.ShapeDtypeStruct((rows, n), F32)]
    if norm_stats:
        out_specs += [pl.BlockSpec((tm, tn), lambda i, j, k: (i, j)), pl.BlockSpec((tm, 128), lambda i, j, k: (i, j))]
        out_shape += [jax.ShapeDtypeStruct((rows, n), BF16), jax.ShapeDtypeStruct((rows, 128 * (n // tn)), F32)]
    outs = pl.pallas_call(
        _ffn_down_body,
        grid=(rows // tm, n // tn, kk // tk),
        in_specs=[pl.BlockSpec((tm, tk), lambda i, j, k: (r0 + i, k)),
                  pl.BlockSpec((tk, tn), lambda i, j, k: (k, j)),
                  pl.BlockSpec((tm, tn), lambda i, j, k: (r0 + i, j))],
        out_specs=out_specs,
        out_shape=out_shape,
        scratch_shapes=[pltpu.VMEM((tm, tn), F32)],
        compiler_params=_params("parallel", "arbitrary", "arbitrary"),
        name="ffn_down",
    )(x, w, res)
    return outs if norm_stats else outs[0]


def _kv_post_body(kv_ref, g_ref, k32_ref, k16_ref, v16_ref, *, width, head_dim, pad_blocks):
    keep = pl.program_id(0) >= pad_blocks
    for h in range(width // head_dim):
        sl = slice(h * head_dim, (h + 1) * head_dim)
        kn = _rms_scale(kv_ref[:, sl]) * g_ref[...]
        k32_ref[:, sl] = kn
        k16_ref[:, sl] = jnp.where(keep, kn, 0.0).astype(BF16)
    v16_ref[...] = jnp.where(keep, kv_ref[:, width:], 0.0).astype(BF16)


def kv_post(kv, gain, head_dim, pad_rows=0, tm=256):
    m, w2 = kv.shape
    width = w2 // 2
    pb = pad_rows // tm
    src = lambda i: (jnp.maximum(i - pb, 0), 0)
    return pl.pallas_call(
        functools.partial(_kv_post_body, width=width, head_dim=head_dim, pad_blocks=pb),
        grid=(pb + m // tm,),
        in_specs=[pl.BlockSpec((tm, w2), src),
                  pl.BlockSpec((1, head_dim), lambda i: (0, 0))],
        out_specs=[pl.BlockSpec((tm, width), src)] + [pl.BlockSpec((tm, width), lambda i: (i, 0))] * 2,
        out_shape=[jax.ShapeDtypeStruct((m, width), F32),
                   jax.ShapeDtypeStruct((pad_rows + m, width), BF16),
                   jax.ShapeDtypeStruct((pad_rows + m, width), BF16)],
        compiler_params=_params("arbitrary"),
        name="kv_post",
    )(kv, gain.reshape(1, head_dim))


def _ssd_body(z_ref, xbc0_ref, xbc1_ref, xbc2_ref, xbc3_ref, dt_ref, dtT_ref, conv0_ref, st0_ref, cw_ref, cb_ref,
              dtb_ref, dtbT_ref, alog_ref, alogT_ref, dskip_ref, gssm_ref,
              y_ref, convn_ref, stn_ref,
              xbuf, xc, st, yacc, *, T):
    c = pl.program_id(1)
    nc = pl.num_programs(1)
    P, N = SSM_HEAD_DIM, SSM_STATE
    T2, PW = 2 * T, 2 * P
    tail = CONV_WIDTH - 1

    @pl.when(c == 0)
    def _():
        xbuf[0:8, :] = conv0_ref[0]
        st[...] = st0_ref[0]

    for q, xbc_ref in enumerate((xbc0_ref, xbc1_ref, xbc2_ref, xbc3_ref)):
        xbuf[8:8 + T, q * XBC_BLOCK:(q + 1) * XBC_BLOCK] = xbc_ref[...].astype(F32)
    acc = cb_ref[...] + cw_ref[0:1, :] * xbuf[8 - tail:8 - tail + T, :]
    for tap in range(1, CONV_WIDTH):
        acc = acc + cw_ref[tap:tap + 1, :] * xbuf[8 - tail + tap:8 - tail + tap + T, :]
    xc[...] = acc * _sigmoid(acc)
    last_rows = xbuf[T:T + 8, :]
    convn_ref[0] = last_rows
    xbuf[0:8, :] = last_rows

    iota = lambda shape, d: lax.broadcasted_iota(jnp.int32, shape, d)
    tri = iota((T, T), 1) <= iota((T, T), 0)
    dt = _softplus(dt_ref[:, 0:SSM_HEADS] + dtb_ref[...])
    cum = _dot_exact(tri.astype(F32), dt * (-jnp.exp(alog_ref[...])))

    half = lambda x, size: jnp.where(x >= size, 1, 0)
    ra, rb = iota((T2, T2), 0), iota((T2, T2), 1)
    same = half(ra, T) == half(rb, T)
    dtT = _softplus(dtT_ref[0] + dtbT_ref[...])
    dtaT = dtT * (-jnp.exp(alogT_ref[...]))
    cumT = _dot_exact(dtaT, jnp.where(same & (ra <= rb), 1.0, 0.0))
    lastT = _dot_exact(dtaT, jnp.where(same, 1.0, 0.0))
    same_p = half(iota((T2, PW), 0), T) == half(iota((T2, PW), 1), P)
    edec = jnp.exp(_dot_exact(dtaT, jnp.where(same_p, 1.0, 0.0)))
    wT = dtT * jnp.exp(lastT - cumT)

    left = iota((T, PW), 1) < P
    left2 = iota((T, T2), 1) < T
    tri2 = (iota((T, T2), 1) - T * half(iota((T, T2), 1), T)) <= iota((T, T2), 0)
    bc_off = SSM_INNER
    cc_off = SSM_INNER + SSM_GROUPS * N

    for g in range(SSM_GROUPS):
        bg = xc[:, bc_off + g * N:bc_off + (g + 1) * N]
        cg16 = xc[:, cc_off + g * N:cc_off + (g + 1) * N].astype(BF16)
        bg2 = jnp.concatenate([bg, bg], axis=0)
        cb2 = _dot_nt(cg16, bg2.astype(BF16))
        bgT2 = bg2.T
        gw = SSM_PPG * PW
        yoff = _dot(cg16, st[:, g * gw:(g + 1) * gw].astype(BF16))
        for k in range(SSM_PPG):
            q = g * SSM_PPG + k
            ps = slice(q * PW, (q + 1) * PW)
            b0 = jnp.broadcast_to(cum[:, 2 * q:2 * q + 1], (T, PW))
            b1 = jnp.broadcast_to(cum[:, 2 * q + 1:2 * q + 2], (T, PW))
            col_p = jnp.where(left, b0, b1)
            col_2 = col_p if T2 == PW else jnp.where(left2, b0[:, 0:T2], b1[:, 0:T2])
            dec = jnp.where(tri2, jnp.exp(col_2 - cumT[q:q + 1, :]), 0.0)
            m2 = (cb2 * dec * dtT[q:q + 1, :]).astype(BF16)
            xp = xc[:, ps]
            bd = jnp.concatenate([jnp.where(left, xp, 0.0), jnp.where(left, 0.0, xp)],
                                 axis=0).astype(BF16)
            yacc[:, ps] = _dot(m2, bd) + yoff[:, k * PW:(k + 1) * PW] * jnp.exp(col_p)
            btw = (bgT2 * wT[q:q + 1, :]).astype(BF16)
            st[:, ps] = st[:, ps] * edec[q:q + 1, :] + _dot(btw, bd)

    y = yacc[...] + xc[:, 0:SSM_INNER] * dskip_ref[...]
    z = z_ref[...].astype(F32)
    yg = y * (z * _sigmoid(z))
    y_ref[...] = (_rms_scale(yg) * gssm_ref[...]).astype(y_ref.dtype)

    @pl.when(c == nc - 1)
    def _():
        stn_ref[0] = st[...]


def ssd_mixer(u, dt_raw, conv0, st0, conv_w, conv_b, dt_bias, a_log, d_skip, g_ssm,
              *, row0, batch, nc, T):
    rows = batch * nc * T
    r0 = row0 // T
    nblk = batch * nc
    dtT = (dt_raw[row0:row0 + rows, :SSM_HEADS].reshape(nblk, T, SSM_PAIRS, 2)
           .transpose(0, 2, 3, 1).reshape(nblk, SSM_PAIRS, 2 * T))
    pair_rows = lambda v: jnp.repeat(v.astype(F32).reshape(SSM_PAIRS, 2), T, axis=1)
    row = lambda v: v.reshape(1, -1).astype(F32)
    const = lambda shape: pl.BlockSpec(shape, lambda b, c: (0,) * len(shape))
    xbc_specs = [pl.BlockSpec((T, XBC_BLOCK), functools.partial(
        lambda b, c, q: (r0 + b * nc + c, SSM_INNER // XBC_BLOCK + q), q=q)) for q in range(CONV_DIM // XBC_BLOCK)]
    return pl.pallas_call(
        functools.partial(_ssd_body, T=T),
        grid=(batch, nc),
        in_specs=[
            pl.BlockSpec((T, SSM_INNER), lambda b, c: (r0 + b * nc + c, 0)),
            *xbc_specs,
            pl.BlockSpec((T, DT_PAD), lambda b, c: (r0 + b * nc + c, 0)),
            pl.BlockSpec((1, SSM_PAIRS, 2 * T), lambda b, c: (b * nc + c, 0, 0)),
            pl.BlockSpec((1, 8, CONV_DIM), lambda b, c: (b, 0, 0)),
            pl.BlockSpec((1, SSM_STATE, SSM_INNER), lambda b, c: (b, 0, 0)),
            const((CONV_WIDTH, CONV_DIM)), const((1, CONV_DIM)),
            const((1, SSM_HEADS)), const((SSM_PAIRS, 2 * T)),
            const((1, SSM_HEADS)), const((SSM_PAIRS, 2 * T)),
            const((1, SSM_INNER)), const((1, SSM_INNER)),
        ],
        out_specs=[
            pl.BlockSpec((T, SSM_INNER), lambda b, c: (b * nc + c, 0)),
            pl.BlockSpec((1, 8, CONV_DIM), lambda b, c: (b, 0, 0)),
            pl.BlockSpec((1, SSM_STATE, SSM_INNER), lambda b, c: (b, 0, 0)),
        ],
        out_shape=[
            jax.ShapeDtypeStruct((rows, SSM_INNER), BF16),
            jax.ShapeDtypeStruct((batch, 8, CONV_DIM), F32),
            jax.ShapeDtypeStruct((batch, SSM_STATE, SSM_INNER), F32),
        ],
        scratch_shapes=[
            pltpu.VMEM((8 + T, CONV_DIM), F32),
            pltpu.VMEM((T, CONV_DIM), F32),
            pltpu.VMEM((SSM_STATE, SSM_INNER), F32),
            pltpu.VMEM((T, SSM_INNER), F32),
        ],
        compiler_params=_params("parallel", "arbitrary"),
        name="ssd_mixer",
    )(u, u, u, u, u, dt_raw, dtT, conv0, st0, conv_w.T.astype(F32), row(conv_b),
      row(dt_bias), pair_rows(dt_bias), row(a_log), pair_rows(a_log),
      row(jnp.repeat(d_skip, SSM_HEAD_DIM)), row(g_ssm))


def _band_body(q_ref, k_ref, v_ref, tab_ref, qg_ref, o_ref, bias_var, *, T, KB, pad, cps):
    c = pl.program_id(2)
    KA = BAND_CONST_KEYS
    d = B_HEAD_DIM

    @pl.when(c == 0)
    def _():
        for j in range(B_GQ):
            tj = tab_ref[0, j:j + 1, :]
            band = pltpu.roll(jnp.broadcast_to(tj, (T, 2 * d)), 2 * d - REL_CLIP, 1,
                              stride=1, stride_axis=0)
            bias_var[j * T:(j + 1) * T, :] = band[:, 0:KB - KA] - tj[:, 0:1]

    scale = 1.0 / math.sqrt(d)

    def chunk(cl, masked):
        start = pl.multiple_of((c * cps + cl) * T, T)
        k = k_ref[0, pl.ds(start, KB), :]
        v = v_ref[0, pl.ds(start, KB), :]
        q = jnp.concatenate(
            [(_rms_scale(q_ref[cl * T:(cl + 1) * T, j * d:(j + 1) * d].astype(F32)) * qg_ref[...] * scale
              ).astype(BF16) for j in range(B_GQ)], axis=0)
        s = _dot_nt(q, k)
        sa = s[:, 0:KA]
        sb = s[:, KA:KB] + bias_var[...]
        if masked:
            sa = jnp.where(start + lax.broadcasted_iota(jnp.int32, (1, KA), 1) >= pad, sa, -jnp.inf)
            sb = jnp.where(start + KA + lax.broadcasted_iota(jnp.int32, (1, KB - KA), 1) >= pad, sb, -jnp.inf)
        m = jnp.maximum(jnp.max(sa, axis=-1, keepdims=True), jnp.max(sb, axis=-1, keepdims=True))
        pa = jnp.exp((sa - m).astype(BF16))
        pb = jnp.exp((sb - m).astype(BF16))
        v1 = jnp.concatenate([v, jnp.ones_like(v)], axis=1)
        ol = _dot(pa, v1[0:KA]) + _dot(pb, v1[KA:KB])
        o = ol[:, 0:d] / ol[:, d:2 * d]
        for j in range(B_GQ):
            o_ref[cl * T:(cl + 1) * T, j * d:(j + 1) * d] = o[j * T:(j + 1) * T].astype(o_ref.dtype)

    if pad:
        @pl.when(c * (cps * T) < pad)
        def _():
            for cl in range(cps):
                chunk(cl, True)

        @pl.when(c * (cps * T) >= pad)
        def _():
            for cl in range(cps):
                chunk(cl, False)
    else:
        for cl in range(cps):
            chunk(cl, False)


def band_attention(u, k, v, table, q_gain, *, row0, batch, nc, T, KB, pad, cps):
    rows = batch * nc * T
    r0 = row0 // (cps * T)
    ns = nc // cps
    ktot = (nc - 1) * T + KB
    gw = B_GQ * B_HEAD_DIM
    tab = table[:, jnp.clip(2 * B_HEAD_DIM - jnp.arange(2 * B_HEAD_DIM), 0, 2 * REL_CLIP)].astype(F32)
    tab = tab.reshape(B_KV_HEADS, B_GQ, 2 * B_HEAD_DIM)
    return pl.pallas_call(
        functools.partial(_band_body, T=T, KB=KB, pad=pad, cps=cps),
        grid=(batch, B_KV_HEADS, ns),
        in_specs=[
            pl.BlockSpec((cps * T, gw), lambda b, g, c: (r0 + b * ns + c, g)),
            pl.BlockSpec((1, ktot, B_HEAD_DIM), lambda b, g, c: (b, 0, g)),
            pl.BlockSpec((1, ktot, B_HEAD_DIM), lambda b, g, c: (b, 0, g)),
            pl.BlockSpec((1, B_GQ, 2 * B_HEAD_DIM), lambda b, g, c: (g, 0, 0)),
            pl.BlockSpec((1, B_HEAD_DIM), lambda b, g, c: (0, 0)),
        ],
        out_specs=pl.BlockSpec((cps * T, gw), lambda b, g, c: (b * ns + c, g)),
        out_shape=jax.ShapeDtypeStruct((rows, B_Q_WIDTH), BF16),
        scratch_shapes=[pltpu.VMEM((B_GQ * T, KB - BAND_CONST_KEYS), F32)],
        compiler_params=_params("parallel", "parallel", "arbitrary"),
        name="band_attention",
    )(u, k, v, tab, q_gain.reshape(1, B_HEAD_DIM).astype(F32))


def _mem_body(q_ref, k_ref, v_ref, qg_ref, o_ref, kb, vb):
    @pl.when(pl.program_id(1) == 0)
    def _():
        for h in range(MEM_HEADS):
            kb[h] = k_ref[:, h, :].astype(BF16)
            vb[h] = v_ref[:, h, :].astype(BF16)

    scale = 1.0 / math.sqrt(MEM_HEAD_DIM)
    for h in range(MEM_HEADS):
        sl = slice(h * MEM_HEAD_DIM, (h + 1) * MEM_HEAD_DIM)
        qn = (_rms_scale(q_ref[:, sl].astype(F32)) * qg_ref[...] * scale).astype(BF16)
        s = _dot_nt(qn, kb[h])
        m = jnp.max(s, axis=-1, keepdims=True)
        p = jnp.exp(s - m)
        l = jnp.sum(p, axis=-1, keepdims=True)
        o_ref[:, sl] = (_dot(p.astype(BF16), vb[h]) / l).astype(o_ref.dtype)


def memory_attention(u, k, v, q_gain, *, q_col_block, layer, row0, batch, nt, tq):
    rows = batch * nt * tq
    r0 = row0 // tq
    return pl.pallas_call(
        _mem_body,
        grid=(batch, nt),
        in_specs=[
            pl.BlockSpec((tq, MEM_WIDTH), lambda b, i: (r0 + b * nt + i, q_col_block)),
            pl.BlockSpec((None, None, MEM_TOKENS, MEM_HEADS, MEM_HEAD_DIM), lambda b, i: (layer, b, 0, 0, 0)),
            pl.BlockSpec((None, None, MEM_TOKENS, MEM_HEADS, MEM_HEAD_DIM), lambda b, i: (layer, b, 0, 0, 0)),
            pl.BlockSpec((1, MEM_HEAD_DIM), lambda b, i: (0, 0)),
        ],
        out_specs=pl.BlockSpec((tq, MEM_WIDTH), lambda b, i: (b * nt + i, 0)),
        out_shape=jax.ShapeDtypeStruct((rows, MEM_WIDTH), BF16),
        scratch_shapes=[pltpu.VMEM((MEM_HEADS, MEM_TOKENS, MEM_HEAD_DIM), BF16)] * 2,
        compiler_params=_params("parallel", "arbitrary"),
        name="memory_attention",
    )(u, k, v, q_gain.reshape(1, MEM_HEAD_DIM).astype(F32))


def _ffn_act(h16, ssq, g_ffn, w_gate, w_up, w_down, layer):
    return matmul_wres(h16, [w_gate, w_up], layer, D_FF_PAD, BF16, WRES_ROW_TILE, 512, _silu_gate,
                       side_cast=(w_down, layer, D_FF_PAD, 32), norm=(ssq, g_ffn))


def kernel(x_prompt, x_sample, mem_prompt, state_ssm, state_conv, cache_kv_k, cache_kv_v, cache_mem_k, cache_mem_v, g_mix, w_in_a, conv_w, conv_b, dt_bias, a_log, d_skip, g_ssm, w_in_b, rel_bias, q_norm_b, g_kv, w_kv, k_norm_kv, g_mem, w_mem_kv, q_norm_mem, k_norm_mem, w_out, g_ffn, w_ffn_gate, w_ffn_up, w_ffn_down):
    S, NS = SEQ, N_SAMPLE_ROWS
    bf = lambda w: w.astype(BF16)

    main_w = SSM_INNER + CONV_DIM
    w_in_a_t = jnp.swapaxes(w_in_a, 1, 2)
    in_a = functools.partial(matmul_wres, ws=[w_in_a_t], layer=0, tm=WRES_ROW_TILE, transposed=True)

    xp = x_prompt.reshape(S, D_MODEL)
    xs = x_sample.reshape(NS, D_MODEL)

    mem_k32, mem_v32 = [], []
    mem_x = mem_prompt.reshape(MEM_TOKENS, D_MODEL)
    mem_shape = (1, MEM_TOKENS, MEM_HEADS, MEM_HEAD_DIM)
    for l in range(2):
        (mn,) = rmsnorm(mem_x, g_mem[l].reshape(1, -1))
        kv = matmul_wres(mn, [w_mem_kv], l, 2 * MEM_WIDTH, F32, MEM_TOKENS, 1024)
        k32, _, _ = kv_post(kv, k_norm_mem[l], MEM_HEAD_DIM)
        mem_k32.append(k32.reshape(mem_shape))
        mem_v32.append(kv[:, MEM_WIDTH:].reshape(mem_shape))

    mem_k_prompt, mem_v_prompt = jnp.stack(mem_k32), jnp.stack(mem_v32)

    def mem_attn(u, q_col_block, l):
        attn = functools.partial(memory_attention, u, q_gain=q_norm_mem[l], q_col_block=q_col_block, layer=l)
        yp = attn(k=mem_k_prompt, v=mem_v_prompt, row0=0, batch=1, nt=S // 512, tq=512)
        ys = attn(k=cache_mem_k, v=cache_mem_v, row0=S, batch=DEC_BATCH, nt=1, tq=DEC_SEQ)
        return yp, ys

    hn = rmsnorm_rows2(xp, xs, g_mix[0])
    u, wout0 = in_a(hn, n_out=main_w, out_dtype=BF16, tn=1024,
                    side_cast=(w_out, 0, w_out.shape[1], 32))
    dt_raw = in_a(hn, n_out=DT_PAD, out_dtype=F32, tn=DT_PAD, col0=main_w)
    q_mem = in_a(hn, n_out=MEM_WIDTH, out_dtype=BF16, tn=1024, col0=main_w + SSM_HEADS)

    def to_state_t(s):
        b = s.shape[0]
        return s.reshape(b, SSM_INNER, SSM_STATE).transpose(0, 2, 1)

    def from_state_t(s):
        b = s.shape[0]
        return s.transpose(0, 2, 1).reshape(b, SSM_HEADS, SSM_HEAD_DIM, SSM_STATE)

    ssd = functools.partial(ssd_mixer, u, dt_raw, conv_w=conv_w[0], conv_b=conv_b[0], dt_bias=dt_bias[0],
                            a_log=a_log[0], d_skip=d_skip[0], g_ssm=g_ssm[0])
    tail = CONV_WIDTH - 1
    conv0_p = jnp.zeros((1, 8, CONV_DIM), F32)
    st0_p = jnp.zeros((1, SSM_STATE, SSM_INNER), F32)
    y_p, conv_p, st_p = ssd(conv0=conv0_p, st0=st0_p, row0=0, batch=1, nc=S // CHUNK, T=CHUNK)
    conv0_s = jnp.pad(state_conv[0], ((0, 0), (8 - tail, 0), (0, 0)))
    y_s, conv_s, st_s = ssd(conv0=conv0_s, st0=to_state_t(state_ssm[0]), row0=S, batch=DEC_BATCH, nc=1, T=DEC_SEQ)
    ym_p, ym_s = mem_attn(q_mem, 0, 0)
    h, h16, ssq = matmul_out(y_p, y_s, ym_p, ym_s, wout0, xp, xs, 0, 512)
    act, wd0 = _ffn_act(h16, ssq, g_ffn[0], w_ffn_gate, w_ffn_up, w_ffn_down, 0)
    h, h16, ssq = ffn_down(act, wd0, h, row0=0, rows=N_ROWS, tm=ROW_TILE, tn=512, tk=5632, norm_stats=True)

    kvw = B_KV_HEADS * B_HEAD_DIM
    kv = matmul_wres(h16, [w_kv[None]], 0, 2 * kvw, F32, WRES_ROW_TILE, 1024, norm=(ssq, g_kv))
    k32, k16, v16 = kv_post(kv, k_norm_kv, B_HEAD_DIM, pad_rows=BAND_PAST)
    u, wout1 = matmul_wres(h16, [w_in_b], 0, B_Q_WIDTH + MEM_WIDTH, BF16, WRES_ROW_TILE, 1024,
                           side_cast=(w_out, 1, w_out.shape[1], 64), norm=(ssq, g_mix[1]))

    yb_p = band_attention(u, k16[None], v16[None], rel_bias[0], q_norm_b[0], row0=0, batch=1, nc=S // CHUNK,
                          T=CHUNK, KB=BAND_PAST + CHUNK, pad=BAND_PAST, cps=BAND_CHUNKS_PER_STEP)

    wc = cache_kv_k.shape[1]
    new_rows = lambda a: a[BAND_PAST + S:].reshape(DEC_BATCH, DEC_SEQ, kvw)
    ks = jnp.concatenate([bf(cache_kv_k).reshape(DEC_BATCH, wc, kvw), new_rows(k16)], axis=1)
    vs = jnp.concatenate([bf(cache_kv_v).reshape(DEC_BATCH, wc, kvw), new_rows(v16)], axis=1)
    yb_s = band_attention(u, ks, vs, rel_bias[0], q_norm_b[0],
                          row0=S, batch=DEC_BATCH, nc=1, T=DEC_SEQ, KB=wc + DEC_SEQ, pad=0, cps=1)
    ym_p, ym_s = mem_attn(u, B_Q_WIDTH // MEM_WIDTH, 1)
    h, h16, ssq = matmul_out(yb_p, yb_s, ym_p, ym_s, wout1, h, h, N_ROWS // SUB_ROWS - 1, 512)
    act, wd1 = _ffn_act(h16, ssq, g_ffn[1], w_ffn_gate, w_ffn_up, w_ffn_down, 1)
    y_prompt = ffn_down(act, wd1, h, row0=0, rows=S, tm=1024, tn=512, tk=5632)
    y_sample = ffn_down(act, wd1, h, row0=S, rows=NS, tm=NS, tn=1024, tk=2816)

    keep = min(BAND_PAST, S)
    kv_shape = (B_KV_HEADS, B_HEAD_DIM)
    return (
        y_prompt.reshape(1, S, D_MODEL),
        y_sample.reshape(DEC_BATCH, DEC_SEQ, D_MODEL),
        from_state_t(st_p)[None],
        conv_p[:, 8 - tail:][None],
        k32[S - keep:S].reshape(1, keep, *kv_shape),
        kv[S - keep:S, kvw:].reshape(1, keep, *kv_shape),
        mem_k_prompt,
        mem_v_prompt,
        from_state_t(st_s)[None],
        conv_s[:, 8 - tail:][None],
        k32[S:].reshape(DEC_BATCH, DEC_SEQ, *kv_shape),
        kv[S:, kvw:].reshape(DEC_BATCH, DEC_SEQ, *kv_shape),
    )
```

```python
import functools
import math

import jax
import jax.numpy as jnp
from jax import lax
from jax.experimental import pallas as pl
from jax.experimental.pallas import tpu as pltpu

F32 = jnp.float32
BF16 = jnp.bfloat16

D_MODEL = 4096
SEQ = 8192
DEC_BATCH = 8
DEC_SEQ = 32
PAST_LEN = 2048
CHUNK = 64
RMS_EPS = 1e-6

SSM_HEAD_DIM = 64
SSM_INNER = 6144
SSM_HEADS = 96
SSM_PAIRS = SSM_HEADS // 2
SSM_GROUPS = 8
SSM_PPG = SSM_PAIRS // SSM_GROUPS
SSM_STATE = 128
CONV_WIDTH = 4
CONV_DIM = 8192
XBC_BLOCK = 2048

B_HEAD_DIM = 128
B_Q_WIDTH = 6144
B_HEADS = 48
B_KV_HEADS = 8
B_GQ = 6
BAND_PAST = 512
REL_CLIP = 64
BAND_CONST_KEYS = 384
BAND_CHUNKS_PER_STEP = 8
SSD_CHUNKS_PER_STEP = 2

MEM_TOKENS = 256
MEM_WIDTH = 2048
MEM_HEADS = 4
MEM_HEAD_DIM = 512

D_FF = 11008
D_FF_PAD = 11264
DT_PAD = 128

N_SAMPLE_ROWS = DEC_BATCH * DEC_SEQ
N_ROWS = SEQ + N_SAMPLE_ROWS
ROW_TILE = 768
WRES_ROW_TILE = 384
SUB_ROWS = 256
SUBS_PER_TILE = ROW_TILE // SUB_ROWS

V7X_VMEM_LIMIT_BYTES = 59 * 1024 * 1024


def _params(*sem):
    return pltpu.CompilerParams(dimension_semantics=sem, vmem_limit_bytes=V7X_VMEM_LIMIT_BYTES)


def _sigmoid(x):
    return 1.0 / (1.0 + jnp.exp(-x))


def _softplus(x):
    return jnp.maximum(x, 0.0) + jnp.log1p(jnp.exp(-jnp.abs(x)))


def _dot(a, b):
    return jnp.dot(a, b, preferred_element_type=F32)


def _dot_nt(a, b):
    return lax.dot_general(a, b, (((1,), (1,)), ((), ())), preferred_element_type=F32)


def _dot_exact(a, b):
    return jnp.dot(a, b, precision=lax.Precision.HIGHEST, preferred_element_type=F32)


def _rms_scale(x):
    return x * lax.rsqrt(jnp.mean(x * x, axis=-1, keepdims=True) + RMS_EPS)


def _rmsnorm_body(x_ref, g_ref, *o_refs):
    xn = _rms_scale(x_ref[...])
    for j, o_ref in enumerate(o_refs):
        o_ref[...] = (xn * g_ref[j:j + 1, :]).astype(o_ref.dtype)


def rmsnorm(x, gains, tm=256):
    m, d = x.shape
    n = gains.shape[0]
    return pl.pallas_call(
        _rmsnorm_body,
        grid=(m // tm,),
        in_specs=[pl.BlockSpec((tm, d), lambda i: (i, 0)),
                  pl.BlockSpec((n, d), lambda i: (0, 0))],
        out_specs=[pl.BlockSpec((tm, d), lambda i: (i, 0))] * n,
        out_shape=[jax.ShapeDtypeStruct((m, d), BF16)] * n,
        compiler_params=_params("parallel"),
        name="rmsnorm",
    )(x, gains)


def _rmsnorm2_body(xp_ref, xs_ref, g_ref, o_ref, *, n_p):
    i = pl.program_id(0)

    @pl.when(i < n_p)
    def _():
        o_ref[...] = (_rms_scale(xp_ref[...]) * g_ref[...]).astype(o_ref.dtype)

    @pl.when(i >= n_p)
    def _():
        o_ref[...] = (_rms_scale(xs_ref[...]) * g_ref[...]).astype(o_ref.dtype)


def rmsnorm_rows2(xp, xs, gain):
    tm = SUB_ROWS
    d = xp.shape[1]
    n_p, n_s = xp.shape[0] // tm, xs.shape[0] // tm
    return pl.pallas_call(
        functools.partial(_rmsnorm2_body, n_p=n_p),
        grid=(n_p + n_s,),
        in_specs=[pl.BlockSpec((tm, d), lambda i: (jnp.minimum(i, n_p - 1), 0)),
                  pl.BlockSpec((tm, d), lambda i: (jnp.maximum(i - n_p, 0), 0)),
                  pl.BlockSpec((1, d), lambda i: (0, 0))],
        out_specs=pl.BlockSpec((tm, d), lambda i: (i, 0)),
        out_shape=jax.ShapeDtypeStruct((xp.shape[0] + xs.shape[0], d), BF16),
        compiler_params=_params("parallel"),
        name="rmsnorm_rows2",
    )(xp, xs, gain.reshape(1, d))


def _wres_body(x_ref, *refs, n_w, layer, col0, n_valid, tn, transposed, epilogue, side, normed):
    w_hbm, refs = refs[:n_w], refs[n_w:]
    side_w = ssq_ref = gain_ref = side_o = None
    if side:
        side_w, refs = refs[0], refs[1:]
    if normed:
        ssq_ref, gain_ref, refs = refs[0], refs[1], refs[2:]
    o_ref, refs = refs[0], refs[1:]
    if side:
        side_o, refs = refs[0], refs[1:]
    wbf, wf32, sem = refs[:n_w], refs[n_w:2 * n_w], refs[2 * n_w]
    j, i = pl.program_id(0), pl.program_id(1)
    nj = pl.num_programs(0)
    rem = n_valid % tn

    if side:
        rows_valid, rb, n_blocks = side
        blk = jnp.minimum(j * pl.num_programs(1) + i, n_blocks - 1)
        row = blk * rb + lax.broadcasted_iota(jnp.int32, side_w.shape, 0)
        side_o[...] = jnp.where(row < rows_valid, side_w[...], 0.0).astype(side_o.dtype)

    def for_each_copy(jj, slot, fn):
        def go(width):
            for w in range(n_w):
                if transposed:
                    src = w_hbm[w].at[layer, pl.ds(col0 + jj * tn, width), :]
                    dst = wf32[w].at[slot, pl.ds(0, width), :]
                else:
                    src = w_hbm[w].at[layer, :, pl.ds(col0 + jj * tn, width)]
                    dst = wf32[w].at[slot, :, pl.ds(0, width)]
                fn(pltpu.make_async_copy(src, dst, sem.at[w, slot]))
        if rem:
            pl.when(jj < nj - 1)(lambda: go(tn))
            pl.when(jj == nj - 1)(lambda: go(rem))
        else:
            go(tn)

    @pl.when(i == 0)
    def _():
        slot = j % 2

        start = lambda cp: cp.start(priority=1)

        @pl.when(j == 0)
        def _():
            for_each_copy(0, 0, start)

        for_each_copy(j, slot, lambda cp: cp.wait())

        @pl.when(j + 1 < nj)
        def _():
            for_each_copy(j + 1, 1 - slot, start)

        def convert(ragged):
            for w in range(n_w):
                wt = wf32[w][slot]
                if ragged:
                    col = lax.broadcasted_iota(jnp.int32, wt.shape, 0 if transposed else 1)
                    wt = jnp.where(col < rem, wt, 0.0)
                if normed:
                    wt = wt * jnp.concatenate([gain_ref[...]] * (tn // 128), axis=1)
                wbf[w][...] = (wt.T if transposed else wt).astype(BF16)

        if rem:
            pl.when(j < nj - 1)(lambda: convert(False))
            pl.when(j == nj - 1)(lambda: convert(True))
        else:
            convert(False)

    x = x_ref[...]
    accs = [_dot(x, wb[...]) for wb in wbf]
    if normed:
        inv = lax.rsqrt(jnp.sum(ssq_ref[...], axis=-1, keepdims=True) / x.shape[1] + RMS_EPS)
        accs = [acc * inv for acc in accs]
    o_ref[...] = epilogue(*accs).astype(o_ref.dtype)


def matmul_wres(x, ws, layer, n_out, out_dtype, tm, tn, epilogue=lambda acc: acc, transposed=False, col0=0,
                side_cast=None, norm=None):
    m, k = x.shape
    n_w = len(ws)
    n_valid = min(n_out, ws[0].shape[1 if transposed else 2] - col0)
    nj, ni = n_out // tn, m // tm
    in_specs = [pl.BlockSpec((tm, k), lambda j, i: (i, 0))] + [pl.BlockSpec(memory_space=pl.ANY)] * n_w
    out_specs = [pl.BlockSpec((tm, tn), lambda j, i: (i, j))]
    out_shape = [jax.ShapeDtypeStruct((m, n_out), out_dtype)]
    args = [x, *ws]
    side = None
    if side_cast:
        w2, layer2, rows_out, rb = side_cast
        rows_valid, n2 = w2.shape[1], w2.shape[2]
        n_blocks, n_src_blocks = rows_out // rb, rows_valid // rb
        assert rows_out % rb == 0 and rows_valid % rb == 0 and n_blocks <= nj * ni
        side = (rows_valid, rb, n_blocks)
        in_specs.append(pl.BlockSpec((None, rb, n2), lambda j, i: (layer2, jnp.minimum(j * ni + i, n_src_blocks - 1), 0)))
        out_specs.append(pl.BlockSpec((rb, n2), lambda j, i: (jnp.minimum(j * ni + i, n_blocks - 1), 0)))
        out_shape.append(jax.ShapeDtypeStruct((rows_out, n2), BF16))
        args.append(w2)
    if norm:
        assert not transposed
        ssq, gain = norm
        in_specs += [pl.BlockSpec((tm, ssq.shape[1]), lambda j, i: (i, 0)),
                     pl.BlockSpec((k, 128), lambda j, i: (0, 0), pipeline_mode=pl.Buffered(1))]
        args += [ssq, jnp.broadcast_to(gain.astype(F32)[:, None], (k, 128))]
    outs = pl.pallas_call(
        functools.partial(_wres_body, n_w=n_w, layer=layer, col0=col0, n_valid=n_valid, tn=tn,
                          transposed=transposed, epilogue=epilogue, side=side, normed=bool(norm)),
        grid=(nj, ni),
        in_specs=in_specs,
        out_specs=out_specs,
        out_shape=out_shape,
        scratch_shapes=[pltpu.VMEM((k, tn), BF16)] * n_w
                       + [pltpu.VMEM((2, tn, k) if transposed else (2, k, tn), F32)] * n_w
                       + [pltpu.SemaphoreType.DMA((n_w, 2))],
        compiler_params=_params("arbitrary", "arbitrary"),
        name="matmul_wres",
    )(*args)
    return outs if side_cast else outs[0]


def _lane_group_sumsq(h):
    return functools.reduce(lambda a, b: a + b, [h[:, c:c + 128] * h[:, c:c + 128] for c in range(0, h.shape[1], 128)])


def _mm_out_body(xap_ref, xas_ref, xbp_ref, xbs_ref, w_ref, rp_ref, rs_ref, o_ref, o16_ref, ssq_ref, *, ka):
    last_tile = pl.program_id(0) == pl.num_programs(0) - 1
    head = (SUBS_PER_TILE - 1) * SUB_ROWS

    def emit(rows, xa, xb, res):
        h = res + _dot(xa, w_ref[0:ka, :]) + _dot(xb, w_ref[ka:, :])
        o_ref[rows, :] = h
        o16_ref[rows, :] = h.astype(BF16)
        ssq_ref[rows, :] = _lane_group_sumsq(h)

    @pl.when(jnp.logical_not(last_tile))
    def _():
        emit(slice(None), xap_ref[...].reshape(ROW_TILE, -1), xbp_ref[...].reshape(ROW_TILE, -1),
             rp_ref[...].reshape(ROW_TILE, -1))

    @pl.when(last_tile)
    def _():
        n_head = SUBS_PER_TILE - 1
        emit(slice(0, head), xap_ref[0:n_head].reshape(head, -1), xbp_ref[0:n_head].reshape(head, -1),
             rp_ref[0:n_head].reshape(head, -1))
        emit(slice(head, ROW_TILE), xas_ref[0], xbs_ref[0], rs_ref[0])


def matmul_out(xa_p, xa_s, xb_p, xb_s, w, r_p, r_s, r_s_block, tn):
    sub3 = lambda a: a.reshape(a.shape[0] // SUB_ROWS, SUB_ROWS, a.shape[1])
    ka, kb, n = xa_p.shape[1], xb_p.shape[1], w.shape[1]
    nt = N_ROWS // ROW_TILE
    spt = SUBS_PER_TILE
    once = dict(pipeline_mode=pl.Buffered(1))
    return pl.pallas_call(
        functools.partial(_mm_out_body, ka=ka),
        grid=(nt, n // tn),
        in_specs=[pl.BlockSpec((spt, SUB_ROWS, ka), lambda i, j: (i, 0, 0)),
                  pl.BlockSpec((1, SUB_ROWS, ka), lambda i, j: (0, 0, 0), **once),
                  pl.BlockSpec((spt, SUB_ROWS, kb), lambda i, j: (i, 0, 0)),
                  pl.BlockSpec((1, SUB_ROWS, kb), lambda i, j: (0, 0, 0), **once),
                  pl.BlockSpec((ka + kb, tn), lambda i, j: (0, j)),
                  pl.BlockSpec((spt, SUB_ROWS, tn), lambda i, j: (i, 0, j)),
                  pl.BlockSpec((1, SUB_ROWS, tn), lambda i, j: (r_s_block, 0, j))],
        out_specs=[pl.BlockSpec((ROW_TILE, tn), lambda i, j: (i, j)),
                   pl.BlockSpec((ROW_TILE, tn), lambda i, j: (i, j)),
                   pl.BlockSpec((ROW_TILE, 128), lambda i, j: (i, j))],
        out_shape=[jax.ShapeDtypeStruct((N_ROWS, n), F32),
                   jax.ShapeDtypeStruct((N_ROWS, n), BF16),
                   jax.ShapeDtypeStruct((N_ROWS, 128 * (n // tn)), F32)],
        compiler_params=_params("parallel", "arbitrary"),
        name="matmul_out",
    )(sub3(xa_p), sub3(xa_s), sub3(xb_p), sub3(xb_s), w, sub3(r_p), sub3(r_s))


def _silu_gate(g, u):
    return g * _sigmoid(g) * u


def _ffn_down_body(x_ref, w_ref, r_ref, o_ref, *rest):
    acc_ref, stat_refs = rest[-1], rest[:-1]
    k = pl.program_id(2)

    @pl.when(k == 0)
    def _():
        acc_ref[...] = r_ref[...]

    acc_ref[...] += _dot(x_ref[...], w_ref[...])

    @pl.when(k == pl.num_programs(2) - 1)
    def _():
        h = acc_ref[...]
        o_ref[...] = h
        if stat_refs:
            o16_ref, ssq_ref = stat_refs
            o16_ref[...] = h.astype(BF16)
            ssq_ref[...] = _lane_group_sumsq(h)


def ffn_down(x, w, res, *, row0, rows, tm, tn, tk, norm_stats=False):
    kk = x.shape[1]
    n = w.shape[1]
    r0 = row0 // tm
    out_specs = [pl.BlockSpec((tm, tn), lambda i, j, k: (i, j))]
    out_shape = [jax.ShapeDtypeStruct((rows, n), F32)]
    if norm_stats:
        out_specs += [pl.BlockSpec((tm, tn), lambda i, j, k: (i, j)), pl.BlockSpec((tm, 128), lambda i, j, k: (i, j))]
        out_shape += [jax.ShapeDtypeStruct((rows, n), BF16), jax.ShapeDtypeStruct((rows, 128 * (n // tn)), F32)]
    outs = pl.pallas_call(
        _ffn_down_body,
        grid=(rows // tm, n // tn, kk // tk),
        in_specs=[pl.BlockSpec((tm, tk), lambda i, j, k: (r0 + i, k)),
                  pl.BlockSpec((tk, tn), lambda i, j, k: (k, j)),
                  pl.BlockSpec((tm, tn), lambda i, j, k: (r0 + i, j))],
        out_specs=out_specs,
        out_shape=out_shape,
        scratch_shapes=[pltpu.VMEM((tm, tn), F32)],
        compiler_params=_params("parallel", "arbitrary", "arbitrary"),
        name="ffn_down",
    )(x, w, res)
    return outs if norm_stats else outs[0]


def _kv_post_body(kv_ref, g_ref, k32_ref, k16_ref, v16_ref, *, width, head_dim, pad_blocks):
    keep = pl.program_id(0) >= pad_blocks
    for h in range(width // head_dim):
        sl = slice(h * head_dim, (h + 1) * head_dim)
        kn = _rms_scale(kv_ref[:, sl]) * g_ref[...]
        k32_ref[:, sl] = kn
        k16_ref[:, sl] = jnp.where(keep, kn, 0.0).astype(BF16)
    v16_ref[...] = jnp.where(keep, kv_ref[:, width:], 0.0).astype(BF16)


def kv_post(kv, gain, head_dim, pad_rows=0, tm=256):
    m, w2 = kv.shape
    width = w2 // 2
    pb = pad_rows // tm
    src = lambda i: (jnp.maximum(i - pb, 0), 0)
    return pl.pallas_call(
        functools.partial(_kv_post_body, width=width, head_dim=head_dim, pad_blocks=pb),
        grid=(pb + m // tm,),
        in_specs=[pl.BlockSpec((tm, w2), src),
                  pl.BlockSpec((1, head_dim), lambda i: (0, 0))],
        out_specs=[pl.BlockSpec((tm, width), src)] + [pl.BlockSpec((tm, width), lambda i: (i, 0))] * 2,
        out_shape=[jax.ShapeDtypeStruct((m, width), F32),
                   jax.ShapeDtypeStruct((pad_rows + m, width), BF16),
                   jax.ShapeDtypeStruct((pad_rows + m, width), BF16)],
        compiler_params=_params("arbitrary"),
        name="kv_post",
    )(kv, gain.reshape(1, head_dim))


def _ssd_body(z_ref, xbc0_ref, xbc1_ref, xbc2_ref, xbc3_ref, dt_ref, dtT_ref, conv0_ref, st0_ref, cw_ref, cb_ref,
              dtb_ref, dtbT_ref, alog_ref, alogT_ref, dskip_ref, gssm_ref,
              y_ref, convn_ref, stn_ref,
              xbuf, xc_all, st, yacc_all, *, T, cps):
    c = pl.program_id(1)
    nc = pl.num_programs(1)

    @pl.when(c == 0)
    def _():
        xbuf[0:8, :] = conv0_ref[0]
        st[...] = st0_ref[0]

    refs = (z_ref, (xbc0_ref, xbc1_ref, xbc2_ref, xbc3_ref), dt_ref, dtT_ref, cw_ref, cb_ref,
            dtb_ref, dtbT_ref, alog_ref, alogT_ref, dskip_ref, gssm_ref, y_ref, convn_ref, xbuf, st)
    for cl in range(cps):
        _ssd_chunk(cl, *refs, xc_all.at[cl], yacc_all.at[cl], T=T)

    @pl.when(c == nc - 1)
    def _():
        stn_ref[0] = st[...]


def _ssd_chunk(cl, z_ref, xbc_refs, dt_ref, dtT_ref, cw_ref, cb_ref, dtb_ref, dtbT_ref, alog_ref, alogT_ref,
               dskip_ref, gssm_ref, y_ref, convn_ref, xbuf, st, xc, yacc, *, T):
    rows = slice(cl * T, (cl + 1) * T)
    P, N = SSM_HEAD_DIM, SSM_STATE
    T2, PW = 2 * T, 2 * P
    tail = CONV_WIDTH - 1

    for q, xbc_ref in enumerate(xbc_refs):
        xbuf[8:8 + T, q * XBC_BLOCK:(q + 1) * XBC_BLOCK] = xbc_ref[rows, :].astype(F32)
    acc = cb_ref[...] + cw_ref[0:1, :] * xbuf[8 - tail:8 - tail + T, :]
    for tap in range(1, CONV_WIDTH):
        acc = acc + cw_ref[tap:tap + 1, :] * xbuf[8 - tail + tap:8 - tail + tap + T, :]
    xc[...] = acc * _sigmoid(acc)
    last_rows = xbuf[T:T + 8, :]
    convn_ref[0] = last_rows
    xbuf[0:8, :] = last_rows

    iota = lambda shape, d: lax.broadcasted_iota(jnp.int32, shape, d)
    tri = iota((T, T), 1) <= iota((T, T), 0)
    dt = _softplus(dt_ref[rows, 0:SSM_HEADS] + dtb_ref[...])
    cum = _dot_exact(tri.astype(F32), dt * (-jnp.exp(alog_ref[...])))

    half = lambda x, size: jnp.where(x >= size, 1, 0)
    ra, rb = iota((T2, T2), 0), iota((T2, T2), 1)
    same = half(ra, T) == half(rb, T)
    dtT = _softplus(dtT_ref[cl] + dtbT_ref[...])
    dtaT = dtT * (-jnp.exp(alogT_ref[...]))
    cumT = _dot_exact(dtaT, jnp.where(same & (ra <= rb), 1.0, 0.0))
    lastT = _dot_exact(dtaT, jnp.where(same, 1.0, 0.0))
    same_p = half(iota((T2, PW), 0), T) == half(iota((T2, PW), 1), P)
    edec = jnp.exp(_dot_exact(dtaT, jnp.where(same_p, 1.0, 0.0)))
    wT = dtT * jnp.exp(lastT - cumT)

    left = iota((T, PW), 1) < P
    left2 = iota((T, T2), 1) < T
    tri2 = (iota((T, T2), 1) - T * half(iota((T, T2), 1), T)) <= iota((T, T2), 0)
    bc_off = SSM_INNER
    cc_off = SSM_INNER + SSM_GROUPS * N

    for g in range(SSM_GROUPS):
        bg = xc[:, bc_off + g * N:bc_off + (g + 1) * N]
        cg16 = xc[:, cc_off + g * N:cc_off + (g + 1) * N].astype(BF16)
        bg2 = jnp.concatenate([bg, bg], axis=0)
        cb2 = _dot_nt(cg16, bg2.astype(BF16))
        bgT2 = bg2.T
        gw = SSM_PPG * PW
        yoff = _dot(cg16, st[:, g * gw:(g + 1) * gw].astype(BF16))
        for k in range(SSM_PPG):
            q = g * SSM_PPG + k
            ps = slice(q * PW, (q + 1) * PW)
            b0 = jnp.broadcast_to(cum[:, 2 * q:2 * q + 1], (T, PW))
            b1 = jnp.broadcast_to(cum[:, 2 * q + 1:2 * q + 2], (T, PW))
            col_p = jnp.where(left, b0, b1)
            col_2 = col_p if T2 == PW else jnp.where(left2, b0[:, 0:T2], b1[:, 0:T2])
            dec = jnp.where(tri2, jnp.exp(col_2 - cumT[q:q + 1, :]), 0.0)
            m2 = (cb2 * dec * dtT[q:q + 1, :]).astype(BF16)
            xp = xc[:, ps]
            bd = jnp.concatenate([jnp.where(left, xp, 0.0), jnp.where(left, 0.0, xp)],
                                 axis=0).astype(BF16)
            yacc[:, ps] = _dot(m2, bd) + yoff[:, k * PW:(k + 1) * PW] * jnp.exp(col_p)
            btw = (bgT2 * wT[q:q + 1, :]).astype(BF16)
            st[:, ps] = st[:, ps] * edec[q:q + 1, :] + _dot(btw, bd)

    y = yacc[...] + xc[:, 0:SSM_INNER] * dskip_ref[...]
    z = z_ref[rows, :].astype(F32)
    yg = y * (z * _sigmoid(z))
    y_ref[rows, :] = (_rms_scale(yg) * gssm_ref[...]).astype(y_ref.dtype)


def ssd_mixer(u, dt_raw, conv0, st0, conv_w, conv_b, dt_bias, a_log, d_skip, g_ssm,
              *, row0, batch, nc, T, cps):
    rows = batch * nc * T
    nblk = batch * nc
    TS = cps * T
    r0 = row0 // TS
    nc = nc // cps
    dtT = (dt_raw[row0:row0 + rows, :SSM_HEADS].reshape(nblk, T, SSM_PAIRS, 2)
           .transpose(0, 2, 3, 1).reshape(nblk, SSM_PAIRS, 2 * T))
    pair_rows = lambda v: jnp.repeat(v.astype(F32).reshape(SSM_PAIRS, 2), T, axis=1)
    row = lambda v: v.reshape(1, -1).astype(F32)
    const = lambda shape: pl.BlockSpec(shape, lambda b, c: (0,) * len(shape))
    xbc_specs = [pl.BlockSpec((TS, XBC_BLOCK), functools.partial(
        lambda b, c, q: (r0 + b * nc + c, SSM_INNER // XBC_BLOCK + q), q=q)) for q in range(CONV_DIM // XBC_BLOCK)]
    return pl.pallas_call(
        functools.partial(_ssd_body, T=T, cps=cps),
        grid=(batch, nc),
        in_specs=[
            pl.BlockSpec((TS, SSM_INNER), lambda b, c: (r0 + b * nc + c, 0)),
            *xbc_specs,
            pl.BlockSpec((TS, DT_PAD), lambda b, c: (r0 + b * nc + c, 0)),
            pl.BlockSpec((cps, SSM_PAIRS, 2 * T), lambda b, c: (b * nc + c, 0, 0)),
            pl.BlockSpec((1, 8, CONV_DIM), lambda b, c: (b, 0, 0)),
            pl.BlockSpec((1, SSM_STATE, SSM_INNER), lambda b, c: (b, 0, 0)),
            const((CONV_WIDTH, CONV_DIM)), const((1, CONV_DIM)),
            const((1, SSM_HEADS)), const((SSM_PAIRS, 2 * T)),
            const((1, SSM_HEADS)), const((SSM_PAIRS, 2 * T)),
            const((1, SSM_INNER)), const((1, SSM_INNER)),
        ],
        out_specs=[
            pl.BlockSpec((TS, SSM_INNER), lambda b, c: (b * nc + c, 0)),
            pl.BlockSpec((1, 8, CONV_DIM), lambda b, c: (b, 0, 0)),
            pl.BlockSpec((1, SSM_STATE, SSM_INNER), lambda b, c: (b, 0, 0)),
        ],
        out_shape=[
            jax.ShapeDtypeStruct((rows, SSM_INNER), BF16),
            jax.ShapeDtypeStruct((batch, 8, CONV_DIM), F32),
            jax.ShapeDtypeStruct((batch, SSM_STATE, SSM_INNER), F32),
        ],
        scratch_shapes=[
            pltpu.VMEM((8 + T, CONV_DIM), F32),
            pltpu.VMEM((cps, T, CONV_DIM), F32),
            pltpu.VMEM((SSM_STATE, SSM_INNER), F32),
            pltpu.VMEM((cps, T, SSM_INNER), F32),
        ],
        compiler_params=_params("parallel", "arbitrary"),
        name="ssd_mixer",
    )(u, u, u, u, u, dt_raw, dtT, conv0, st0, conv_w.T.astype(F32), row(conv_b),
      row(dt_bias), pair_rows(dt_bias), row(a_log), pair_rows(a_log),
      row(jnp.repeat(d_skip, SSM_HEAD_DIM)), row(g_ssm))


def _band_body(q_ref, k_ref, v_ref, tab_ref, qg_ref, o_ref, bias_var, *, T, KB, pad, cps):
    c = pl.program_id(2)
    KA = BAND_CONST_KEYS
    d = B_HEAD_DIM

    @pl.when(c == 0)
    def _():
        for j in range(B_GQ):
            tj = tab_ref[0, j:j + 1, :]
            band = pltpu.roll(jnp.broadcast_to(tj, (T, 2 * d)), 2 * d - REL_CLIP, 1,
                              stride=1, stride_axis=0)
            bias_var[j * T:(j + 1) * T, :] = band[:, 0:KB - KA] - tj[:, 0:1]

    scale = 1.0 / math.sqrt(d)

    def chunk(cl, masked):
        start = pl.multiple_of((c * cps + cl) * T, T)
        k = k_ref[0, pl.ds(start, KB), :]
        v = v_ref[0, pl.ds(start, KB), :]
        q = jnp.concatenate(
            [(_rms_scale(q_ref[cl * T:(cl + 1) * T, j * d:(j + 1) * d].astype(F32)) * qg_ref[...] * scale
              ).astype(BF16) for j in range(B_GQ)], axis=0)
        s = _dot_nt(q, k)
        sa = s[:, 0:KA]
        sb = s[:, KA:KB] + bias_var[...]
        if masked:
            sa = jnp.where(start + lax.broadcasted_iota(jnp.int32, (1, KA), 1) >= pad, sa, -jnp.inf)
            sb = jnp.where(start + KA + lax.broadcasted_iota(jnp.int32, (1, KB - KA), 1) >= pad, sb, -jnp.inf)
        m = jnp.maximum(jnp.max(sa, axis=-1, keepdims=True), jnp.max(sb, axis=-1, keepdims=True))
        pa = jnp.exp((sa - m).astype(BF16))
        pb = jnp.exp((sb - m).astype(BF16))
        v1 = jnp.concatenate([v, jnp.ones_like(v)], axis=1)
        ol = _dot(pa, v1[0:KA]) + _dot(pb, v1[KA:KB])
        o = ol[:, 0:d] / ol[:, d:2 * d]
        for j in range(B_GQ):
            o_ref[cl * T:(cl + 1) * T, j * d:(j + 1) * d] = o[j * T:(j + 1) * T].astype(o_ref.dtype)

    if pad:
        @pl.when(c * (cps * T) < pad)
        def _():
            for cl in range(cps):
                chunk(cl, True)

        @pl.when(c * (cps * T) >= pad)
        def _():
            for cl in range(cps):
                chunk(cl, False)
    else:
        for cl in range(cps):
            chunk(cl, False)


def band_attention(u, k, v, table, q_gain, *, row0, batch, nc, T, KB, pad, cps):
    rows = batch * nc * T
    r0 = row0 // (cps * T)
    ns = nc // cps
    ktot = (nc - 1) * T + KB
    gw = B_GQ * B_HEAD_DIM
    tab = table[:, jnp.clip(2 * B_HEAD_DIM - jnp.arange(2 * B_HEAD_DIM), 0, 2 * REL_CLIP)].astype(F32)
    tab = tab.reshape(B_KV_HEADS, B_GQ, 2 * B_HEAD_DIM)
    return pl.pallas_call(
        functools.partial(_band_body, T=T, KB=KB, pad=pad, cps=cps),
        grid=(batch, B_KV_HEADS, ns),
        in_specs=[
            pl.BlockSpec((cps * T, gw), lambda b, g, c: (r0 + b * ns + c, g)),
            pl.BlockSpec((1, ktot, B_HEAD_DIM), lambda b, g, c: (b, 0, g)),
            pl.BlockSpec((1, ktot, B_HEAD_DIM), lambda b, g, c: (b, 0, g)),
            pl.BlockSpec((1, B_GQ, 2 * B_HEAD_DIM), lambda b, g, c: (g, 0, 0)),
            pl.BlockSpec((1, B_HEAD_DIM), lambda b, g, c: (0, 0)),
        ],
        out_specs=pl.BlockSpec((cps * T, gw), lambda b, g, c: (b * ns + c, g)),
        out_shape=jax.ShapeDtypeStruct((rows, B_Q_WIDTH), BF16),
        scratch_shapes=[pltpu.VMEM((B_GQ * T, KB - BAND_CONST_KEYS), F32)],
        compiler_params=_params("parallel", "parallel", "arbitrary"),
        name="band_attention",
    )(u, k, v, tab, q_gain.reshape(1, B_HEAD_DIM).astype(F32))


def _mem_body(q_ref, k_ref, v_ref, qg_ref, o_ref, kb, vb):
    @pl.when(pl.program_id(1) == 0)
    def _():
        for h in range(MEM_HEADS):
            kb[h] = k_ref[:, h, :].astype(BF16)
            vb[h] = v_ref[:, h, :].astype(BF16)

    scale = 1.0 / math.sqrt(MEM_HEAD_DIM)
    for h in range(MEM_HEADS):
        sl = slice(h * MEM_HEAD_DIM, (h + 1) * MEM_HEAD_DIM)
        qn = (_rms_scale(q_ref[:, sl].astype(F32)) * qg_ref[...] * scale).astype(BF16)
        s = _dot_nt(qn, kb[h])
        m = jnp.max(s, axis=-1, keepdims=True)
        p = jnp.exp(s - m)
        l = jnp.sum(p, axis=-1, keepdims=True)
        o_ref[:, sl] = (_dot(p.astype(BF16), vb[h]) / l).astype(o_ref.dtype)


def memory_attention(u, k, v, q_gain, *, q_col_block, layer, row0, batch, nt, tq):
    rows = batch * nt * tq
    r0 = row0 // tq
    return pl.pallas_call(
        _mem_body,
        grid=(batch, nt),
        in_specs=[
            pl.BlockSpec((tq, MEM_WIDTH), lambda b, i: (r0 + b * nt + i, q_col_block)),
            pl.BlockSpec((None, None, MEM_TOKENS, MEM_HEADS, MEM_HEAD_DIM), lambda b, i: (layer, b, 0, 0, 0)),
            pl.BlockSpec((None, None, MEM_TOKENS, MEM_HEADS, MEM_HEAD_DIM), lambda b, i: (layer, b, 0, 0, 0)),
            pl.BlockSpec((1, MEM_HEAD_DIM), lambda b, i: (0, 0)),
        ],
        out_specs=pl.BlockSpec((tq, MEM_WIDTH), lambda b, i: (b * nt + i, 0)),
        out_shape=jax.ShapeDtypeStruct((rows, MEM_WIDTH), BF16),
        scratch_shapes=[pltpu.VMEM((MEM_HEADS, MEM_TOKENS, MEM_HEAD_DIM), BF16)] * 2,
        compiler_params=_params("parallel", "arbitrary"),
        name="memory_attention",
    )(u, k, v, q_gain.reshape(1, MEM_HEAD_DIM).astype(F32))


def _ffn_act(h16, ssq, g_ffn, w_gate, w_up, w_down, layer):
    return matmul_wres(h16, [w_gate, w_up], layer, D_FF_PAD, BF16, WRES_ROW_TILE, 512, _silu_gate,
                       side_cast=(w_down, layer, D_FF_PAD, 32), norm=(ssq, g_ffn))


def kernel(x_prompt, x_sample, mem_prompt, state_ssm, state_conv, cache_kv_k, cache_kv_v, cache_mem_k, cache_mem_v, g_mix, w_in_a, conv_w, conv_b, dt_bias, a_log, d_skip, g_ssm, w_in_b, rel_bias, q_norm_b, g_kv, w_kv, k_norm_kv, g_mem, w_mem_kv, q_norm_mem, k_norm_mem, w_out, g_ffn, w_ffn_gate, w_ffn_up, w_ffn_down):
    S, NS = SEQ, N_SAMPLE_ROWS
    bf = lambda w: w.astype(BF16)

    main_w = SSM_INNER + CONV_DIM
    w_in_a_t = jnp.swapaxes(w_in_a, 1, 2)
    in_a = functools.partial(matmul_wres, ws=[w_in_a_t], layer=0, tm=WRES_ROW_TILE, transposed=True)

    xp = x_prompt.reshape(S, D_MODEL)
    xs = x_sample.reshape(NS, D_MODEL)

    mem_k32, mem_v32 = [], []
    mem_x = mem_prompt.reshape(MEM_TOKENS, D_MODEL)
    mem_shape = (1, MEM_TOKENS, MEM_HEADS, MEM_HEAD_DIM)
    for l in range(2):
        (mn,) = rmsnorm(mem_x, g_mem[l].reshape(1, -1))
        kv = matmul_wres(mn, [w_mem_kv], l, 2 * MEM_WIDTH, F32, MEM_TOKENS, 1024)
        k32, _, _ = kv_post(kv, k_norm_mem[l], MEM_HEAD_DIM)
        mem_k32.append(k32.reshape(mem_shape))
        mem_v32.append(kv[:, MEM_WIDTH:].reshape(mem_shape))

    mem_k_prompt, mem_v_prompt = jnp.stack(mem_k32), jnp.stack(mem_v32)

    def mem_attn(u, q_col_block, l):
        attn = functools.partial(memory_attention, u, q_gain=q_norm_mem[l], q_col_block=q_col_block, layer=l)
        yp = attn(k=mem_k_prompt, v=mem_v_prompt, row0=0, batch=1, nt=S // 512, tq=512)
        ys = attn(k=cache_mem_k, v=cache_mem_v, row0=S, batch=DEC_BATCH, nt=1, tq=DEC_SEQ)
        return yp, ys

    hn = rmsnorm_rows2(xp, xs, g_mix[0])
    u, wout0 = in_a(hn, n_out=main_w, out_dtype=BF16, tn=1024,
                    side_cast=(w_out, 0, w_out.shape[1], 32))
    dt_raw = in_a(hn, n_out=DT_PAD, out_dtype=F32, tn=DT_PAD, col0=main_w)
    q_mem = in_a(hn, n_out=MEM_WIDTH, out_dtype=BF16, tn=1024, col0=main_w + SSM_HEADS)

    def to_state_t(s):
        b = s.shape[0]
        return s.reshape(b, SSM_INNER, SSM_STATE).transpose(0, 2, 1)

    def from_state_t(s):
        b = s.shape[0]
        return s.transpose(0, 2, 1).reshape(b, SSM_HEADS, SSM_HEAD_DIM, SSM_STATE)

    ssd = functools.partial(ssd_mixer, u, dt_raw, conv_w=conv_w[0], conv_b=conv_b[0], dt_bias=dt_bias[0],
                            a_log=a_log[0], d_skip=d_skip[0], g_ssm=g_ssm[0])
    tail = CONV_WIDTH - 1
    conv0_p = jnp.zeros((1, 8, CONV_DIM), F32)
    st0_p = jnp.zeros((1, SSM_STATE, SSM_INNER), F32)
    y_p, conv_p, st_p = ssd(conv0=conv0_p, st0=st0_p, row0=0, batch=1, nc=S // CHUNK, T=CHUNK,
                            cps=SSD_CHUNKS_PER_STEP)
    conv0_s = jnp.pad(state_conv[0], ((0, 0), (8 - tail, 0), (0, 0)))
    y_s, conv_s, st_s = ssd(conv0=conv0_s, st0=to_state_t(state_ssm[0]), row0=S, batch=DEC_BATCH, nc=1, T=DEC_SEQ,
                            cps=1)
    ym_p, ym_s = mem_attn(q_mem, 0, 0)
    h, h16, ssq = matmul_out(y_p, y_s, ym_p, ym_s, wout0, xp, xs, 0, 512)
    act, wd0 = _ffn_act(h16, ssq, g_ffn[0], w_ffn_gate, w_ffn_up, w_ffn_down, 0)
    h, h16, ssq = ffn_down(act, wd0, h, row0=0, rows=N_ROWS, tm=ROW_TILE, tn=1024, tk=2816, norm_stats=True)

    kvw = B_KV_HEADS * B_HEAD_DIM
    kv = matmul_wres(h16, [w_kv[None]], 0, 2 * kvw, F32, WRES_ROW_TILE, 1024, norm=(ssq, g_kv))
    k32, k16, v16 = kv_post(kv, k_norm_kv, B_HEAD_DIM, pad_rows=BAND_PAST)
    u, wout1 = matmul_wres(h16, [w_in_b], 0, B_Q_WIDTH + MEM_WIDTH, BF16, WRES_ROW_TILE, 1024,
                           side_cast=(w_out, 1, w_out.shape[1], 64), norm=(ssq, g_mix[1]))

    yb_p = band_attention(u, k16[None], v16[None], rel_bias[0], q_norm_b[0], row0=0, batch=1, nc=S // CHUNK,
                          T=CHUNK, KB=BAND_PAST + CHUNK, pad=BAND_PAST, cps=BAND_CHUNKS_PER_STEP)

    wc = cache_kv_k.shape[1]
    new_rows = lambda a: a[BAND_PAST + S:].reshape(DEC_BATCH, DEC_SEQ, kvw)
    ks = jnp.concatenate([bf(cache_kv_k).reshape(DEC_BATCH, wc, kvw), new_rows(k16)], axis=1)
    vs = jnp.concatenate([bf(cache_kv_v).reshape(DEC_BATCH, wc, kvw), new_rows(v16)], axis=1)
    yb_s = band_attention(u, ks, vs, rel_bias[0], q_norm_b[0],
                          row0=S, batch=DEC_BATCH, nc=1, T=DEC_SEQ, KB=wc + DEC_SEQ, pad=0, cps=1)
    ym_p, ym_s = mem_attn(u, B_Q_WIDTH // MEM_WIDTH, 1)
    h, h16, ssq = matmul_out(yb_p, yb_s, ym_p, ym_s, wout1, h, h, N_ROWS // SUB_ROWS - 1, 512)
    act, wd1 = _ffn_act(h16, ssq, g_ffn[1], w_ffn_gate, w_ffn_up, w_ffn_down, 1)
    y_prompt = ffn_down(act, wd1, h, row0=0, rows=S, tm=1024, tn=1024, tk=2816)
    y_sample = ffn_down(act, wd1, h, row0=S, rows=NS, tm=NS, tn=1024, tk=2816)

    keep = min(BAND_PAST, S)
    kv_shape = (B_KV_HEADS, B_HEAD_DIM)
    return (
        y_prompt.reshape(1, S, D_MODEL),
        y_sample.reshape(DEC_BATCH, DEC_SEQ, D_MODEL),
        from_state_t(st_p)[None],
        conv_p[:, 8 - tail:][None],
        k32[S - keep:S].reshape(1, keep, *kv_shape),
        kv[S - keep:S, kvw:].reshape(1, keep, *kv_shape),
        mem_k_prompt,
        mem_v_prompt,
        from_state_t(st_s)[None],
        conv_s[:, 8 - tail:][None],
        k32[S:].reshape(DEC_BATCH, DEC_SEQ, *kv_shape),
        kv[S:, kvw:].reshape(DEC_BATCH, DEC_SEQ, *kv_shape),
    )
```

```python
import functools
import math

import jax
import jax.numpy as jnp
from jax import lax
from jax.experimental import pallas as pl
from jax.experimental.pallas import tpu as pltpu

F32 = jnp.float32
BF16 = jnp.bfloat16

D_MODEL = 4096
SEQ = 8192
DEC_BATCH = 8
DEC_SEQ = 32
PAST_LEN = 2048
CHUNK = 64
RMS_EPS = 1e-6

SSM_HEAD_DIM = 64
SSM_INNER = 6144
SSM_HEADS = 96
SSM_PAIRS = SSM_HEADS // 2
SSM_GROUPS = 8
SSM_PPG = SSM_PAIRS // SSM_GROUPS
SSM_STATE = 128
CONV_WIDTH = 4
CONV_DIM = 8192
XBC_BLOCK = 2048

B_HEAD_DIM = 128
B_Q_WIDTH = 6144
B_HEADS = 48
B_KV_HEADS = 8
B_GQ = 6
BAND_PAST = 512
REL_CLIP = 64
BAND_CONST_KEYS = 384
BAND_CHUNKS_PER_STEP = 8
SSD_CHUNKS_PER_STEP = 2

MEM_TOKENS = 256
MEM_WIDTH = 2048
MEM_HEADS = 4
MEM_HEAD_DIM = 512

D_FF = 11008
D_FF_PAD = 11264
DT_PAD = 128

N_SAMPLE_ROWS = DEC_BATCH * DEC_SEQ
N_ROWS = SEQ + N_SAMPLE_ROWS
ROW_TILE = 768
WRES_ROW_TILE = 384
SUB_ROWS = 256
SUBS_PER_TILE = ROW_TILE // SUB_ROWS

V7X_VMEM_LIMIT_BYTES = 59 * 1024 * 1024


def _params(*sem):
    return pltpu.CompilerParams(dimension_semantics=sem, vmem_limit_bytes=V7X_VMEM_LIMIT_BYTES)


def _sigmoid(x):
    return 1.0 / (1.0 + jnp.exp(-x))


def _softplus(x):
    return jnp.maximum(x, 0.0) + jnp.log1p(jnp.exp(-jnp.abs(x)))


def _dot(a, b):
    return jnp.dot(a, b, preferred_element_type=F32)


def _dot_nt(a, b):
    return lax.dot_general(a, b, (((1,), (1,)), ((), ())), preferred_element_type=F32)


def _dot_exact(a, b):
    return jnp.dot(a, b, precision=lax.Precision.HIGHEST, preferred_element_type=F32)


def _rms_scale(x):
    return x * lax.rsqrt(jnp.mean(x * x, axis=-1, keepdims=True) + RMS_EPS)


def _rmsnorm_body(x_ref, g_ref, *o_refs):
    xn = _rms_scale(x_ref[...])
    for j, o_ref in enumerate(o_refs):
        o_ref[...] = (xn * g_ref[j:j + 1, :]).astype(o_ref.dtype)


def rmsnorm(x, gains, tm=256):
    m, d = x.shape
    n = gains.shape[0]
    return pl.pallas_call(
        _rmsnorm_body,
        grid=(m // tm,),
        in_specs=[pl.BlockSpec((tm, d), lambda i: (i, 0)),
                  pl.BlockSpec((n, d), lambda i: (0, 0))],
        out_specs=[pl.BlockSpec((tm, d), lambda i: (i, 0))] * n,
        out_shape=[jax.ShapeDtypeStruct((m, d), BF16)] * n,
        compiler_params=_params("parallel"),
        name="rmsnorm",
    )(x, gains)


def _rmsnorm2_body(xp_ref, xs_ref, g_ref, o_ref, *, n_p):
    i = pl.program_id(0)

    @pl.when(i < n_p)
    def _():
        o_ref[...] = (_rms_scale(xp_ref[...]) * g_ref[...]).astype(o_ref.dtype)

    @pl.when(i >= n_p)
    def _():
        o_ref[...] = (_rms_scale(xs_ref[...]) * g_ref[...]).astype(o_ref.dtype)


def rmsnorm_rows2(xp, xs, gain):
    tm = SUB_ROWS
    d = xp.shape[1]
    n_p, n_s = xp.shape[0] // tm, xs.shape[0] // tm
    return pl.pallas_call(
        functools.partial(_rmsnorm2_body, n_p=n_p),
        grid=(n_p + n_s,),
        in_specs=[pl.BlockSpec((tm, d), lambda i: (jnp.minimum(i, n_p - 1), 0)),
                  pl.BlockSpec((tm, d), lambda i: (jnp.maximum(i - n_p, 0), 0)),
                  pl.BlockSpec((1, d), lambda i: (0, 0))],
        out_specs=pl.BlockSpec((tm, d), lambda i: (i, 0)),
        out_shape=jax.ShapeDtypeStruct((xp.shape[0] + xs.shape[0], d), BF16),
        compiler_params=_params("parallel"),
        name="rmsnorm_rows2",
    )(xp, xs, gain.reshape(1, d))


def _side_cast_plan(jobs, n_steps, step_of):
    in_specs, args, out_specs, out_shapes, metas = [], [], [], [], []
    for w, layer, rows_out, cols_out, rb, gain in jobs:
        r, n = w.shape[1], w.shape[2]
        n_blocks, n_src = rows_out // rb, r // rb
        assert rows_out % rb == 0 and r % rb == 0 and n_blocks <= n_steps and n % 128 == 0
        src = lambda *g, layer=layer, n_src=n_src: (layer, jnp.minimum(step_of(*g), n_src - 1), 0)
        in_specs.append(pl.BlockSpec((None, rb, n), src))
        args.append(w)
        if gain is not None:
            in_specs.append(pl.BlockSpec((rb, 1), lambda *g, n_src=n_src: (jnp.minimum(step_of(*g), n_src - 1), 0)))
            args.append(gain.reshape(r, 1).astype(F32))
        out_specs.append(pl.BlockSpec((rb, cols_out), lambda *g, n_blocks=n_blocks: (jnp.minimum(step_of(*g), n_blocks - 1), 0)))
        out_shapes.append(jax.ShapeDtypeStruct((rows_out, cols_out), BF16))
        metas.append((r, rb, n_blocks, gain is not None))
    return in_specs, args, out_specs, out_shapes, metas


def _side_cast_step(metas, in_refs, out_refs, step):
    in_refs = list(in_refs)
    for (rows_valid, rb, n_blocks, has_gain), o_ref in zip(metas, out_refs):
        w_ref = in_refs.pop(0)
        x = w_ref[...]
        if has_gain:
            x = x * in_refs.pop(0)[...]
        row = jnp.minimum(step, n_blocks - 1) * rb + lax.broadcasted_iota(jnp.int32, x.shape, 0)
        x = jnp.where(row < rows_valid, x, 0.0)
        if o_ref.shape[1] > x.shape[1]:
            x = jnp.concatenate([x, jnp.zeros((rb, o_ref.shape[1] - x.shape[1]), F32)], axis=1)
        o_ref[...] = x.astype(o_ref.dtype)


def _wres_body(x_ref, *refs, n_w, layer, col0, n_valid, tn, transposed, epilogue, side, n_side_in, normed):
    w_hbm, refs = refs[:n_w], refs[n_w:]
    side_in, refs = refs[:n_side_in], refs[n_side_in:]
    ssq_ref = gain_ref = None
    if normed:
        ssq_ref, gain_ref, refs = refs[0], refs[1], refs[2:]
    o_ref, refs = refs[0], refs[1:]
    side_out, refs = refs[:len(side)], refs[len(side):]
    wbf, wf32, sem = refs[:n_w], refs[n_w:2 * n_w], refs[2 * n_w]
    j, i = pl.program_id(0), pl.program_id(1)
    nj = pl.num_programs(0)
    rem = n_valid % tn

    _side_cast_step(side, side_in, side_out, j * pl.num_programs(1) + i)

    def for_each_copy(jj, slot, fn):
        def go(width):
            for w in range(n_w):
                if transposed:
                    src = w_hbm[w].at[layer, pl.ds(col0 + jj * tn, width), :]
                    dst = wf32[w].at[slot, pl.ds(0, width), :]
                else:
                    src = w_hbm[w].at[layer, :, pl.ds(col0 + jj * tn, width)]
                    dst = wf32[w].at[slot, :, pl.ds(0, width)]
                fn(pltpu.make_async_copy(src, dst, sem.at[w, slot]))
        if rem:
            pl.when(jj < nj - 1)(lambda: go(tn))
            pl.when(jj == nj - 1)(lambda: go(rem))
        else:
            go(tn)

    @pl.when(i == 0)
    def _():
        slot = j % 2

        start = lambda cp: cp.start(priority=1)

        @pl.when(j == 0)
        def _():
            for_each_copy(0, 0, start)

        for_each_copy(j, slot, lambda cp: cp.wait())

        @pl.when(j + 1 < nj)
        def _():
            for_each_copy(j + 1, 1 - slot, start)

        def convert(ragged):
            for w in range(n_w):
                wt = wf32[w][slot]
                if ragged:
                    col = lax.broadcasted_iota(jnp.int32, wt.shape, 0 if transposed else 1)
                    wt = jnp.where(col < rem, wt, 0.0)
                if normed:
                    wt = wt * jnp.concatenate([gain_ref[...]] * (tn // 128), axis=1)
                wbf[w][...] = (wt.T if transposed else wt).astype(BF16)

        if rem:
            pl.when(j < nj - 1)(lambda: convert(False))
            pl.when(j == nj - 1)(lambda: convert(True))
        else:
            convert(False)

    x = x_ref[...]
    accs = [_dot(x, wb[...]) for wb in wbf]
    if normed:
        inv = lax.rsqrt(jnp.sum(ssq_ref[...], axis=-1, keepdims=True) / x.shape[1] + RMS_EPS)
        accs = [acc * inv for acc in accs]
    o_ref[...] = epilogue(*accs).astype(o_ref.dtype)


def matmul_wres(x, ws, layer, n_out, out_dtype, tm, tn, epilogue=lambda acc: acc, transposed=False, col0=0,
                side_casts=(), norm=None):
    m, k = x.shape
    n_w = len(ws)
    n_valid = min(n_out, ws[0].shape[1 if transposed else 2] - col0)
    nj, ni = n_out // tn, m // tm
    s_in, s_args, s_out, s_shape, side = _side_cast_plan(side_casts, nj * ni, lambda j, i: j * ni + i)
    in_specs = [pl.BlockSpec((tm, k), lambda j, i: (i, 0))] + [pl.BlockSpec(memory_space=pl.ANY)] * n_w + s_in
    args = [x, *ws, *s_args]
    if norm:
        assert not transposed
        ssq, gain = norm
        in_specs += [pl.BlockSpec((tm, ssq.shape[1]), lambda j, i: (i, 0)),
                     pl.BlockSpec((k, 128), lambda j, i: (0, 0), pipeline_mode=pl.Buffered(1))]
        args += [ssq, jnp.broadcast_to(gain.astype(F32)[:, None], (k, 128))]
    outs = pl.pallas_call(
        functools.partial(_wres_body, n_w=n_w, layer=layer, col0=col0, n_valid=n_valid, tn=tn,
                          transposed=transposed, epilogue=epilogue, side=side, n_side_in=len(s_in),
                          normed=bool(norm)),
        grid=(nj, ni),
        in_specs=in_specs,
        out_specs=[pl.BlockSpec((tm, tn), lambda j, i: (i, j))] + s_out,
        out_shape=[jax.ShapeDtypeStruct((m, n_out), out_dtype)] + s_shape,
        scratch_shapes=[pltpu.VMEM((k, tn), BF16)] * n_w
                       + [pltpu.VMEM((2, tn, k) if transposed else (2, k, tn), F32)] * n_w
                       + [pltpu.SemaphoreType.DMA((n_w, 2))],
        compiler_params=_params("arbitrary", "arbitrary"),
        name="matmul_wres",
    )(*args)
    return outs if side_casts else outs[0]


def _ffn_up_body(x_ref, wg_ref, wu_ref, ssq_ref, *refs, side, n_side_in):
    side_in, o_ref, side_out = refs[:n_side_in], refs[n_side_in], refs[n_side_in + 1:]
    _side_cast_step(side, side_in, side_out, pl.program_id(0) * pl.num_programs(1) + pl.program_id(1))
    x = x_ref[...]
    inv = lax.rsqrt(jnp.sum(ssq_ref[...], axis=-1, keepdims=True) / x.shape[1] + RMS_EPS)
    o_ref[...] = _silu_gate(_dot(x, wg_ref[...]) * inv, _dot(x, wu_ref[...]) * inv).astype(o_ref.dtype)


def ffn_up(x, ssq, wg, wu, tm, tn, side_casts=()):
    m, k = x.shape
    n = wg.shape[1]
    ni, nj = m // tm, n // tn
    s_in, s_args, s_out, s_shape, side = _side_cast_plan(side_casts, ni * nj, lambda i, j: i * nj + j)
    outs = pl.pallas_call(
        functools.partial(_ffn_up_body, side=side, n_side_in=len(s_in)),
        grid=(ni, nj),
        in_specs=[pl.BlockSpec((tm, k), lambda i, j: (i, 0)),
                  pl.BlockSpec((k, tn), lambda i, j: (0, j)),
                  pl.BlockSpec((k, tn), lambda i, j: (0, j)),
                  pl.BlockSpec((tm, ssq.shape[1]), lambda i, j: (i, 0))] + s_in,
        out_specs=[pl.BlockSpec((tm, tn), lambda i, j: (i, j))] + s_out,
        out_shape=[jax.ShapeDtypeStruct((m, n), BF16)] + s_shape,
        compiler_params=_params("arbitrary", "arbitrary"),
        name="ffn_up",
    )(x, wg, wu, ssq, *s_args)
    return outs if side_casts else outs[0]


def _lane_group_sumsq(h):
    return functools.reduce(lambda a, b: a + b, [h[:, c:c + 128] * h[:, c:c + 128] for c in range(0, h.shape[1], 128)])


def _mm_out_body(xap_ref, xas_ref, xbp_ref, xbs_ref, w_ref, rp_ref, rs_ref, o_ref, o16_ref, ssq_ref, *, ka):
    last_tile = pl.program_id(0) == pl.num_programs(0) - 1
    head = (SUBS_PER_TILE - 1) * SUB_ROWS

    def emit(rows, xa, xb, res):
        h = res + _dot(xa, w_ref[0:ka, :]) + _dot(xb, w_ref[ka:, :])
        o_ref[rows, :] = h
        o16_ref[rows, :] = h.astype(BF16)
        ssq_ref[rows, :] = _lane_group_sumsq(h)

    @pl.when(jnp.logical_not(last_tile))
    def _():
        emit(slice(None), xap_ref[...].reshape(ROW_TILE, -1), xbp_ref[...].reshape(ROW_TILE, -1),
             rp_ref[...].reshape(ROW_TILE, -1))

    @pl.when(last_tile)
    def _():
        n_head = SUBS_PER_TILE - 1
        emit(slice(0, head), xap_ref[0:n_head].reshape(head, -1), xbp_ref[0:n_head].reshape(head, -1),
             rp_ref[0:n_head].reshape(head, -1))
        emit(slice(head, ROW_TILE), xas_ref[0], xbs_ref[0], rs_ref[0])


def matmul_out(xa_p, xa_s, xb_p, xb_s, w, r_p, r_s, r_s_block, tn):
    sub3 = lambda a: a.reshape(a.shape[0] // SUB_ROWS, SUB_ROWS, a.shape[1])
    ka, kb, n = xa_p.shape[1], xb_p.shape[1], w.shape[1]
    nt = N_ROWS // ROW_TILE
    spt = SUBS_PER_TILE
    once = dict(pipeline_mode=pl.Buffered(1))
    return pl.pallas_call(
        functools.partial(_mm_out_body, ka=ka),
        grid=(nt, n // tn),
        in_specs=[pl.BlockSpec((spt, SUB_ROWS, ka), lambda i, j: (i, 0, 0)),
                  pl.BlockSpec((1, SUB_ROWS, ka), lambda i, j: (0, 0, 0), **once),
                  pl.BlockSpec((spt, SUB_ROWS, kb), lambda i, j: (i, 0, 0)),
                  pl.BlockSpec((1, SUB_ROWS, kb), lambda i, j: (0, 0, 0), **once),
                  pl.BlockSpec((ka + kb, tn), lambda i, j: (0, j)),
                  pl.BlockSpec((spt, SUB_ROWS, tn), lambda i, j: (i, 0, j)),
                  pl.BlockSpec((1, SUB_ROWS, tn), lambda i, j: (r_s_block, 0, j))],
        out_specs=[pl.BlockSpec((ROW_TILE, tn), lambda i, j: (i, j)),
                   pl.BlockSpec((ROW_TILE, tn), lambda i, j: (i, j)),
                   pl.BlockSpec((ROW_TILE, 128), lambda i, j: (i, j))],
        out_shape=[jax.ShapeDtypeStruct((N_ROWS, n), F32),
                   jax.ShapeDtypeStruct((N_ROWS, n), BF16),
                   jax.ShapeDtypeStruct((N_ROWS, 128 * (n // tn)), F32)],
        compiler_params=_params("parallel", "arbitrary"),
        name="matmul_out",
    )(sub3(xa_p), sub3(xa_s), sub3(xb_p), sub3(xb_s), w, sub3(r_p), sub3(r_s))


def _silu_gate(g, u):
    return g * _sigmoid(g) * u


def _ffn_down_body(x_ref, w_ref, r_ref, o_ref, *rest):
    acc_ref, stat_refs = rest[-1], rest[:-1]
    k = pl.program_id(2)

    @pl.when(k == 0)
    def _():
        acc_ref[...] = r_ref[...]

    acc_ref[...] += _dot(x_ref[...], w_ref[...])

    @pl.when(k == pl.num_programs(2) - 1)
    def _():
        h = acc_ref[...]
        o_ref[...] = h
        if stat_refs:
            o16_ref, ssq_ref = stat_refs
            o16_ref[...] = h.astype(BF16)
            ssq_ref[...] = _lane_group_sumsq(h)


def ffn_down(x, w, res, *, row0, rows, tm, tn, tk, norm_stats=False):
    kk = x.shape[1]
    n = w.shape[1]
    r0 = row0 // tm
    out_specs = [pl.BlockSpec((tm, tn), lambda i, j, k: (i, j))]
    out_shape = [jax.ShapeDtypeStruct((rows, n), F32)]
    if norm_stats:
        out_specs += [pl.BlockSpec((tm, tn), lambda i, j, k: (i, j)), pl.BlockSpec((tm, 128), lambda i, j, k: (i, j))]
        out_shape += [jax.ShapeDtypeStruct((rows, n), BF16), jax.ShapeDtypeStruct((rows, 128 * (n // tn)), F32)]
    outs = pl.pallas_call(
        _ffn_down_body,
        grid=(rows // tm, n // tn, kk // tk),
        in_specs=[pl.BlockSpec((tm, tk), lambda i, j, k: (r0 + i, k)),
                  pl.BlockSpec((tk, tn), lambda i, j, k: (k, j)),
                  pl.BlockSpec((tm, tn), lambda i, j, k: (r0 + i, j))],
        out_specs=out_specs,
        out_shape=out_shape,
        scratch_shapes=[pltpu.VMEM((tm, tn), F32)],
        compiler_params=_params("parallel", "arbitrary", "arbitrary"),
        name="ffn_down",
    )(x, w, res)
    return outs if norm_stats else outs[0]


def _kv_post_body(kv_ref, g_ref, k32_ref, k16_ref, v16_ref, *, width, head_dim, pad_blocks):
    keep = pl.program_id(0) >= pad_blocks
    for h in range(width // head_dim):
        sl = slice(h * head_dim, (h + 1) * head_dim)
        kn = _rms_scale(kv_ref[:, sl]) * g_ref[...]
        k32_ref[:, sl] = kn
        k16_ref[:, sl] = jnp.where(keep, kn, 0.0).astype(BF16)
    v16_ref[...] = jnp.where(keep, kv_ref[:, width:], 0.0).astype(BF16)


def kv_post(kv, gain, head_dim, pad_rows=0, tm=256):
    m, w2 = kv.shape
    width = w2 // 2
    pb = pad_rows // tm
    src = lambda i: (jnp.maximum(i - pb, 0), 0)
    return pl.pallas_call(
        functools.partial(_kv_post_body, width=width, head_dim=head_dim, pad_blocks=pb),
        grid=(pb + m // tm,),
        in_specs=[pl.BlockSpec((tm, w2), src),
                  pl.BlockSpec((1, head_dim), lambda i: (0, 0))],
        out_specs=[pl.BlockSpec((tm, width), src)] + [pl.BlockSpec((tm, width), lambda i: (i, 0))] * 2,
        out_shape=[jax.ShapeDtypeStruct((m, width), F32),
                   jax.ShapeDtypeStruct((pad_rows + m, width), BF16),
                   jax.ShapeDtypeStruct((pad_rows + m, width), BF16)],
        compiler_params=_params("arbitrary"),
        name="kv_post",
    )(kv, gain.reshape(1, head_dim))


def _ssd_body(z_ref, xbc0_ref, xbc1_ref, xbc2_ref, xbc3_ref, dt_ref, dtT_ref, conv0_ref, st0_ref, cw_ref, cb_ref,
              dtb_ref, dtbT_ref, alog_ref, alogT_ref, dskip_ref, gssm_ref,
              y_ref, convn_ref, stn_ref,
              xbuf, xc_all, st, yacc_all, *, T, cps):
    c = pl.program_id(1)
    nc = pl.num_programs(1)

    @pl.when(c == 0)
    def _():
        xbuf[0:8, :] = conv0_ref[0]
        st[...] = st0_ref[0]

    refs = (z_ref, (xbc0_ref, xbc1_ref, xbc2_ref, xbc3_ref), dt_ref, dtT_ref, cw_ref, cb_ref,
            dtb_ref, dtbT_ref, alog_ref, alogT_ref, dskip_ref, gssm_ref, y_ref, convn_ref, xbuf, st)
    for cl in range(cps):
        _ssd_chunk(cl, *refs, xc_all.at[cl], yacc_all.at[cl], T=T)

    @pl.when(c == nc - 1)
    def _():
        stn_ref[0] = st[...]


def _ssd_chunk(cl, z_ref, xbc_refs, dt_ref, dtT_ref, cw_ref, cb_ref, dtb_ref, dtbT_ref, alog_ref, alogT_ref,
               dskip_ref, gssm_ref, y_ref, convn_ref, xbuf, st, xc, yacc, *, T):
    rows = slice(cl * T, (cl + 1) * T)
    P, N = SSM_HEAD_DIM, SSM_STATE
    T2, PW = 2 * T, 2 * P
    tail = CONV_WIDTH - 1

    for q, xbc_ref in enumerate(xbc_refs):
        xbuf[8:8 + T, q * XBC_BLOCK:(q + 1) * XBC_BLOCK] = xbc_ref[rows, :].astype(F32)
    acc = cb_ref[...] + cw_ref[0:1, :] * xbuf[8 - tail:8 - tail + T, :]
    for tap in range(1, CONV_WIDTH):
        acc = acc + cw_ref[tap:tap + 1, :] * xbuf[8 - tail + tap:8 - tail + tap + T, :]
    xc[...] = acc * _sigmoid(acc)
    last_rows = xbuf[T:T + 8, :]
    convn_ref[0] = last_rows
    xbuf[0:8, :] = last_rows

    iota = lambda shape, d: lax.broadcasted_iota(jnp.int32, shape, d)
    tri = iota((T, T), 1) <= iota((T, T), 0)
    dt = _softplus(dt_ref[rows, 0:SSM_HEADS] + dtb_ref[...])
    cum = _dot_exact(tri.astype(F32), dt * (-jnp.exp(alog_ref[...])))

    half = lambda x, size: jnp.where(x >= size, 1, 0)
    ra, rb = iota((T2, T2), 0), iota((T2, T2), 1)
    same = half(ra, T) == half(rb, T)
    dtT = _softplus(dtT_ref[cl] + dtbT_ref[...])
    dtaT = dtT * (-jnp.exp(alogT_ref[...]))
    cumT = _dot_exact(dtaT, jnp.where(same & (ra <= rb), 1.0, 0.0))
    lastT = _dot_exact(dtaT, jnp.where(same, 1.0, 0.0))
    same_p = half(iota((T2, PW), 0), T) == half(iota((T2, PW), 1), P)
    edec = jnp.exp(_dot_exact(dtaT, jnp.where(same_p, 1.0, 0.0)))
    wT = dtT * jnp.exp(lastT - cumT)

    left = iota((T, PW), 1) < P
    left2 = iota((T, T2), 1) < T
    tri2 = (iota((T, T2), 1) - T * half(iota((T, T2), 1), T)) <= iota((T, T2), 0)
    bc_off = SSM_INNER
    cc_off = SSM_INNER + SSM_GROUPS * N

    for g in range(SSM_GROUPS):
        bg = xc[:, bc_off + g * N:bc_off + (g + 1) * N]
        cg16 = xc[:, cc_off + g * N:cc_off + (g + 1) * N].astype(BF16)
        bg2 = jnp.concatenate([bg, bg], axis=0)
        cb2 = _dot_nt(cg16, bg2.astype(BF16))
        bgT2 = bg2.T
        gw = SSM_PPG * PW
        yoff = _dot(cg16, st[:, g * gw:(g + 1) * gw].astype(BF16))
        for k in range(SSM_PPG):
            q = g * SSM_PPG + k
            ps = slice(q * PW, (q + 1) * PW)
            b0 = jnp.broadcast_to(cum[:, 2 * q:2 * q + 1], (T, PW))
            b1 = jnp.broadcast_to(cum[:, 2 * q + 1:2 * q + 2], (T, PW))
            col_p = jnp.where(left, b0, b1)
            col_2 = col_p if T2 == PW else jnp.where(left2, b0[:, 0:T2], b1[:, 0:T2])
            dec = jnp.where(tri2, jnp.exp(col_2 - cumT[q:q + 1, :]), 0.0)
            m2 = (cb2 * dec * dtT[q:q + 1, :]).astype(BF16)
            xp = xc[:, ps]
            bd = jnp.concatenate([jnp.where(left, xp, 0.0), jnp.where(left, 0.0, xp)],
                                 axis=0).astype(BF16)
            yacc[:, ps] = _dot(m2, bd) + yoff[:, k * PW:(k + 1) * PW] * jnp.exp(col_p)
            btw = (bgT2 * wT[q:q + 1, :]).astype(BF16)
            st[:, ps] = st[:, ps] * edec[q:q + 1, :] + _dot(btw, bd)

    y = yacc[...] + xc[:, 0:SSM_INNER] * dskip_ref[...]
    z = z_ref[rows, :].astype(F32)
    yg = y * (z * _sigmoid(z))
    y_ref[rows, :] = (_rms_scale(yg) * gssm_ref[...]).astype(y_ref.dtype)


def ssd_mixer(u, dt_raw, conv0, st0, conv_w, conv_b, dt_bias, a_log, d_skip, g_ssm,
              *, row0, batch, nc, T, cps):
    rows = batch * nc * T
    nblk = batch * nc
    TS = cps * T
    r0 = row0 // TS
    nc = nc // cps
    dtT = (dt_raw[row0:row0 + rows, :SSM_HEADS].reshape(nblk, T, SSM_PAIRS, 2)
           .transpose(0, 2, 3, 1).reshape(nblk, SSM_PAIRS, 2 * T))
    pair_rows = lambda v: jnp.repeat(v.astype(F32).reshape(SSM_PAIRS, 2), T, axis=1)
    row = lambda v: v.reshape(1, -1).astype(F32)
    const = lambda shape: pl.BlockSpec(shape, lambda b, c: (0,) * len(shape))
    xbc_specs = [pl.BlockSpec((TS, XBC_BLOCK), functools.partial(
        lambda b, c, q: (r0 + b * nc + c, SSM_INNER // XBC_BLOCK + q), q=q)) for q in range(CONV_DIM // XBC_BLOCK)]
    return pl.pallas_call(
        functools.partial(_ssd_body, T=T, cps=cps),
        grid=(batch, nc),
        in_specs=[
            pl.BlockSpec((TS, SSM_INNER), lambda b, c: (r0 + b * nc + c, 0)),
            *xbc_specs,
            pl.BlockSpec((TS, DT_PAD), lambda b, c: (r0 + b * nc + c, 0)),
            pl.BlockSpec((cps, SSM_PAIRS, 2 * T), lambda b, c: (b * nc + c, 0, 0)),
            pl.BlockSpec((1, 8, CONV_DIM), lambda b, c: (b, 0, 0)),
            pl.BlockSpec((1, SSM_STATE, SSM_INNER), lambda b, c: (b, 0, 0)),
            const((CONV_WIDTH, CONV_DIM)), const((1, CONV_DIM)),
            const((1, SSM_HEADS)), const((SSM_PAIRS, 2 * T)),
            const((1, SSM_HEADS)), const((SSM_PAIRS, 2 * T)),
            const((1, SSM_INNER)), const((1, SSM_INNER)),
        ],
        out_specs=[
            pl.BlockSpec((TS, SSM_INNER), lambda b, c: (b * nc + c, 0)),
            pl.BlockSpec((1, 8, CONV_DIM), lambda b, c: (b, 0, 0)),
            pl.BlockSpec((1, SSM_STATE, SSM_INNER), lambda b, c: (b, 0, 0)),
        ],
        out_shape=[
            jax.ShapeDtypeStruct((rows, SSM_INNER), BF16),
            jax.ShapeDtypeStruct((batch, 8, CONV_DIM), F32),
            jax.ShapeDtypeStruct((batch, SSM_STATE, SSM_INNER), F32),
        ],
        scratch_shapes=[
            pltpu.VMEM((8 + T, CONV_DIM), F32),
            pltpu.VMEM((cps, T, CONV_DIM), F32),
            pltpu.VMEM((SSM_STATE, SSM_INNER), F32),
            pltpu.VMEM((cps, T, SSM_INNER), F32),
        ],
        compiler_params=_params("parallel", "arbitrary"),
        name="ssd_mixer",
    )(u, u, u, u, u, dt_raw, dtT, conv0, st0, conv_w.T.astype(F32), row(conv_b),
      row(dt_bias), pair_rows(dt_bias), row(a_log), pair_rows(a_log),
      row(jnp.repeat(d_skip, SSM_HEAD_DIM)), row(g_ssm))


def _band_body(q_ref, k_ref, v_ref, tab_ref, qg_ref, o_ref, bias_var, *, T, KB, pad, cps):
    c = pl.program_id(2)
    KA = BAND_CONST_KEYS
    d = B_HEAD_DIM

    @pl.when(c == 0)
    def _():
        for j in range(B_GQ):
            tj = tab_ref[0, j:j + 1, :]
            band = pltpu.roll(jnp.broadcast_to(tj, (T, 2 * d)), 2 * d - REL_CLIP, 1,
                              stride=1, stride_axis=0)
            bias_var[j * T:(j + 1) * T, :] = band[:, 0:KB - KA] - tj[:, 0:1]

    scale = 1.0 / math.sqrt(d)

    def chunk(cl, masked):
        start = pl.multiple_of((c * cps + cl) * T, T)
        k = k_ref[0, pl.ds(start, KB), :]
        v = v_ref[0, pl.ds(start, KB), :]
        q = jnp.concatenate(
            [(_rms_scale(q_ref[cl * T:(cl + 1) * T, j * d:(j + 1) * d].astype(F32)) * qg_ref[...] * scale
              ).astype(BF16) for j in range(B_GQ)], axis=0)
        s = _dot_nt(q, k)
        sa = s[:, 0:KA]
        sb = s[:, KA:KB] + bias_var[...]
        if masked:
            sa = jnp.where(start + lax.broadcasted_iota(jnp.int32, (1, KA), 1) >= pad, sa, -jnp.inf)
            sb = jnp.where(start + KA + lax.broadcasted_iota(jnp.int32, (1, KB - KA), 1) >= pad, sb, -jnp.inf)
        m = jnp.maximum(jnp.max(sa, axis=-1, keepdims=True), jnp.max(sb, axis=-1, keepdims=True))
        pa = jnp.exp((sa - m).astype(BF16))
        pb = jnp.exp((sb - m).astype(BF16))
        v1 = jnp.concatenate([v, jnp.ones_like(v)], axis=1)
        ol = _dot(pa, v1[0:KA]) + _dot(pb, v1[KA:KB])
        o = ol[:, 0:d] / ol[:, d:2 * d]
        for j in range(B_GQ):
            o_ref[cl * T:(cl + 1) * T, j * d:(j + 1) * d] = o[j * T:(j + 1) * T].astype(o_ref.dtype)

    if pad:
        @pl.when(c * (cps * T) < pad)
        def _():
            for cl in range(cps):
                chunk(cl, True)

        @pl.when(c * (cps * T) >= pad)
        def _():
            for cl in range(cps):
                chunk(cl, False)
    else:
        for cl in range(cps):
            chunk(cl, False)


def band_attention(u, k, v, table, q_gain, *, row0, batch, nc, T, KB, pad, cps):
    rows = batch * nc * T
    r0 = row0 // (cps * T)
    ns = nc // cps
    ktot = (nc - 1) * T + KB
    gw = B_GQ * B_HEAD_DIM
    tab = table[:, jnp.clip(2 * B_HEAD_DIM - jnp.arange(2 * B_HEAD_DIM), 0, 2 * REL_CLIP)].astype(F32)
    tab = tab.reshape(B_KV_HEADS, B_GQ, 2 * B_HEAD_DIM)
    return pl.pallas_call(
        functools.partial(_band_body, T=T, KB=KB, pad=pad, cps=cps),
        grid=(batch, B_KV_HEADS, ns),
        in_specs=[
            pl.BlockSpec((cps * T, gw), lambda b, g, c: (r0 + b * ns + c, g)),
            pl.BlockSpec((1, ktot, B_HEAD_DIM), lambda b, g, c: (b, 0, g)),
            pl.BlockSpec((1, ktot, B_HEAD_DIM), lambda b, g, c: (b, 0, g)),
            pl.BlockSpec((1, B_GQ, 2 * B_HEAD_DIM), lambda b, g, c: (g, 0, 0)),
            pl.BlockSpec((1, B_HEAD_DIM), lambda b, g, c: (0, 0)),
        ],
        out_specs=pl.BlockSpec((cps * T, gw), lambda b, g, c: (b * ns + c, g)),
        out_shape=jax.ShapeDtypeStruct((rows, B_Q_WIDTH), BF16),
        scratch_shapes=[pltpu.VMEM((B_GQ * T, KB - BAND_CONST_KEYS), F32)],
        compiler_params=_params("parallel", "parallel", "arbitrary"),
        name="band_attention",
    )(u, k, v, tab, q_gain.reshape(1, B_HEAD_DIM).astype(F32))


def _mem_body(q_ref, k_ref, v_ref, qg_ref, o_ref, kb, vb):
    @pl.when(pl.program_id(1) == 0)
    def _():
        for h in range(MEM_HEADS):
            kb[h] = k_ref[:, h, :].astype(BF16)
            vb[h] = v_ref[:, h, :].astype(BF16)

    scale = 1.0 / math.sqrt(MEM_HEAD_DIM)
    for h in range(MEM_HEADS):
        sl = slice(h * MEM_HEAD_DIM, (h + 1) * MEM_HEAD_DIM)
        qn = (_rms_scale(q_ref[:, sl].astype(F32)) * qg_ref[...] * scale).astype(BF16)
        s = _dot_nt(qn, kb[h])
        m = jnp.max(s, axis=-1, keepdims=True)
        p = jnp.exp(s - m)
        l = jnp.sum(p, axis=-1, keepdims=True)
        o_ref[:, sl] = (_dot(p.astype(BF16), vb[h]) / l).astype(o_ref.dtype)


def memory_attention(u, k, v, q_gain, *, q_col_block, layer, row0, batch, nt, tq):
    rows = batch * nt * tq
    r0 = row0 // tq
    return pl.pallas_call(
        _mem_body,
        grid=(batch, nt),
        in_specs=[
            pl.BlockSpec((tq, MEM_WIDTH), lambda b, i: (r0 + b * nt + i, q_col_block)),
            pl.BlockSpec((None, None, MEM_TOKENS, MEM_HEADS, MEM_HEAD_DIM), lambda b, i: (layer, b, 0, 0, 0)),
            pl.BlockSpec((None, None, MEM_TOKENS, MEM_HEADS, MEM_HEAD_DIM), lambda b, i: (layer, b, 0, 0, 0)),
            pl.BlockSpec((1, MEM_HEAD_DIM), lambda b, i: (0, 0)),
        ],
        out_specs=pl.BlockSpec((tq, MEM_WIDTH), lambda b, i: (b * nt + i, 0)),
        out_shape=jax.ShapeDtypeStruct((rows, MEM_WIDTH), BF16),
        scratch_shapes=[pltpu.VMEM((MEM_HEADS, MEM_TOKENS, MEM_HEAD_DIM), BF16)] * 2,
        compiler_params=_params("parallel", "arbitrary"),
        name="memory_attention",
    )(u, k, v, q_gain.reshape(1, MEM_HEAD_DIM).astype(F32))


def kernel(x_prompt, x_sample, mem_prompt, state_ssm, state_conv, cache_kv_k, cache_kv_v, cache_mem_k, cache_mem_v, g_mix, w_in_a, conv_w, conv_b, dt_bias, a_log, d_skip, g_ssm, w_in_b, rel_bias, q_norm_b, g_kv, w_kv, k_norm_kv, g_mem, w_mem_kv, q_norm_mem, k_norm_mem, w_out, g_ffn, w_ffn_gate, w_ffn_up, w_ffn_down):
    S, NS = SEQ, N_SAMPLE_ROWS
    bf = lambda w: w.astype(BF16)

    main_w = SSM_INNER + CONV_DIM
    w_in_a_t = jnp.swapaxes(w_in_a, 1, 2)
    in_a = functools.partial(matmul_wres, ws=[w_in_a_t], layer=0, tm=WRES_ROW_TILE, transposed=True)

    xp = x_prompt.reshape(S, D_MODEL)
    xs = x_sample.reshape(NS, D_MODEL)

    mem_k32, mem_v32 = [], []
    mem_x = mem_prompt.reshape(MEM_TOKENS, D_MODEL)
    mem_shape = (1, MEM_TOKENS, MEM_HEADS, MEM_HEAD_DIM)
    for l in range(2):
        (mn,) = rmsnorm(mem_x, g_mem[l].reshape(1, -1))
        kv = matmul_wres(mn, [w_mem_kv], l, 2 * MEM_WIDTH, F32, MEM_TOKENS, 1024)
        k32, _, _ = kv_post(kv, k_norm_mem[l], MEM_HEAD_DIM)
        mem_k32.append(k32.reshape(mem_shape))
        mem_v32.append(kv[:, MEM_WIDTH:].reshape(mem_shape))

    mem_k_prompt, mem_v_prompt = jnp.stack(mem_k32), jnp.stack(mem_v32)

    def mem_attn(u, q_col_block, l):
        attn = functools.partial(memory_attention, u, q_gain=q_norm_mem[l], q_col_block=q_col_block, layer=l)
        yp = attn(k=mem_k_prompt, v=mem_v_prompt, row0=0, batch=1, nt=S // 512, tq=512)
        ys = attn(k=cache_mem_k, v=cache_mem_v, row0=S, batch=DEC_BATCH, nt=1, tq=DEC_SEQ)
        return yp, ys

    hn = rmsnorm_rows2(xp, xs, g_mix[0])
    d_out = w_out.shape[1]
    ffn_in = lambda w, l, rb: (w, l, D_MODEL, D_FF_PAD, rb, g_ffn[l])
    u, wout0, wg0, wu0 = in_a(hn, n_out=main_w, out_dtype=BF16, tn=1024,
                              side_casts=[(w_out, 0, d_out, D_MODEL, 32, None),
                                          ffn_in(w_ffn_gate, 0, 16), ffn_in(w_ffn_up, 0, 16)])
    dt_raw = in_a(hn, n_out=DT_PAD, out_dtype=F32, tn=DT_PAD, col0=main_w)
    q_mem = in_a(hn, n_out=MEM_WIDTH, out_dtype=BF16, tn=1024, col0=main_w + SSM_HEADS)

    def to_state_t(s):
        b = s.shape[0]
        return s.reshape(b, SSM_INNER, SSM_STATE).transpose(0, 2, 1)

    def from_state_t(s):
        b = s.shape[0]
        return s.transpose(0, 2, 1).reshape(b, SSM_HEADS, SSM_HEAD_DIM, SSM_STATE)

    ssd = functools.partial(ssd_mixer, u, dt_raw, conv_w=conv_w[0], conv_b=conv_b[0], dt_bias=dt_bias[0],
                            a_log=a_log[0], d_skip=d_skip[0], g_ssm=g_ssm[0])
    tail = CONV_WIDTH - 1
    conv0_p = jnp.zeros((1, 8, CONV_DIM), F32)
    st0_p = jnp.zeros((1, SSM_STATE, SSM_INNER), F32)
    y_p, conv_p, st_p = ssd(conv0=conv0_p, st0=st0_p, row0=0, batch=1, nc=S // CHUNK, T=CHUNK,
                            cps=SSD_CHUNKS_PER_STEP)
    conv0_s = jnp.pad(state_conv[0], ((0, 0), (8 - tail, 0), (0, 0)))
    y_s, conv_s, st_s = ssd(conv0=conv0_s, st0=to_state_t(state_ssm[0]), row0=S, batch=DEC_BATCH, nc=1, T=DEC_SEQ,
                            cps=1)
    ym_p, ym_s = mem_attn(q_mem, 0, 0)
    h, h16, ssq = matmul_out(y_p, y_s, ym_p, ym_s, wout0, xp, xs, 0, 512)
    ffn_out = lambda l: (w_ffn_down, l, D_FF_PAD, D_MODEL, 64, None)
    act, wd0, wg1, wu1 = ffn_up(h16, ssq, wg0, wu0, ROW_TILE, 512,
                                side_casts=[ffn_out(0), ffn_in(w_ffn_gate, 1, 32), ffn_in(w_ffn_up, 1, 32)])
    h, h16, ssq = ffn_down(act, wd0, h, row0=0, rows=N_ROWS, tm=ROW_TILE, tn=1024, tk=2816, norm_stats=True)

    kvw = B_KV_HEADS * B_HEAD_DIM
    kv = matmul_wres(h16, [w_kv[None]], 0, 2 * kvw, F32, WRES_ROW_TILE, 1024, norm=(ssq, g_kv))
    k32, k16, v16 = kv_post(kv, k_norm_kv, B_HEAD_DIM, pad_rows=BAND_PAST)
    u, wout1 = matmul_wres(h16, [w_in_b], 0, B_Q_WIDTH + MEM_WIDTH, BF16, WRES_ROW_TILE, 1024,
                           side_casts=[(w_out, 1, d_out, D_MODEL, 64, None)], norm=(ssq, g_mix[1]))

    yb_p = band_attention(u, k16[None], v16[None], rel_bias[0], q_norm_b[0], row0=0, batch=1, nc=S // CHUNK,
                          T=CHUNK, KB=BAND_PAST + CHUNK, pad=BAND_PAST, cps=BAND_CHUNKS_PER_STEP)

    wc = cache_kv_k.shape[1]
    new_rows = lambda a: a[BAND_PAST + S:].reshape(DEC_BATCH, DEC_SEQ, kvw)
    ks = jnp.concatenate([bf(cache_kv_k).reshape(DEC_BATCH, wc, kvw), new_rows(k16)], axis=1)
    vs = jnp.concatenate([bf(cache_kv_v).reshape(DEC_BATCH, wc, kvw), new_rows(v16)], axis=1)
    yb_s = band_attention(u, ks, vs, rel_bias[0], q_norm_b[0],
                          row0=S, batch=DEC_BATCH, nc=1, T=DEC_SEQ, KB=wc + DEC_SEQ, pad=0, cps=1)
    ym_p, ym_s = mem_attn(u, B_Q_WIDTH // MEM_WIDTH, 1)
    h, h16, ssq = matmul_out(yb_p, yb_s, ym_p, ym_s, wout1, h, h, N_ROWS // SUB_ROWS - 1, 512)
    act, wd1 = ffn_up(h16, ssq, wg1, wu1, ROW_TILE, 512, side_casts=[ffn_out(1)])
    y_prompt = ffn_down(act, wd1, h, row0=0, rows=S, tm=1024, tn=1024, tk=2816)
    y_sample = ffn_down(act, wd1, h, row0=S, rows=NS, tm=NS, tn=1024, tk=2816)

    keep = min(BAND_PAST, S)
    kv_shape = (B_KV_HEADS, B_HEAD_DIM)
    return (
        y_prompt.reshape(1, S, D_MODEL),
        y_sample.reshape(DEC_BATCH, DEC_SEQ, D_MODEL),
        from_state_t(st_p)[None],
        conv_p[:, 8 - tail:][None],
        k32[S - keep:S].reshape(1, keep, *kv_shape),
        kv[S - keep:S, kvw:].reshape(1, keep, *kv_shape),
        mem_k_prompt,
        mem_v_prompt,
        from_state_t(st_s)[None],
        conv_s[:, 8 - tail:][None],
        k32[S:].reshape(DEC_BATCH, DEC_SEQ, *kv_shape),
        kv[S:, kvw:].reshape(DEC_BATCH, DEC_SEQ, *kv_shape),
    )
```

```python
import functools
import math

import jax
import jax.numpy as jnp
from jax import lax
from jax.experimental import pallas as pl
from jax.experimental.pallas import tpu as pltpu

F32 = jnp.float32
BF16 = jnp.bfloat16

D_MODEL = 4096
SEQ = 8192
DEC_BATCH = 8
DEC_SEQ = 32
PAST_LEN = 2048
CHUNK = 64
RMS_EPS = 1e-6

SSM_HEAD_DIM = 64
SSM_INNER = 6144
SSM_HEADS = 96
SSM_PAIRS = SSM_HEADS // 2
SSM_GROUPS = 8
SSM_PPG = SSM_PAIRS // SSM_GROUPS
SSM_STATE = 128
CONV_WIDTH = 4
CONV_DIM = 8192
XBC_BLOCK = 2048

B_HEAD_DIM = 128
B_Q_WIDTH = 6144
B_HEADS = 48
B_KV_HEADS = 8
B_GQ = 6
BAND_PAST = 512
REL_CLIP = 64
BAND_CONST_KEYS = 384
BAND_CHUNKS_PER_STEP = 8
SSD_CHUNKS_PER_STEP = 2

MEM_TOKENS = 256
MEM_WIDTH = 2048
MEM_HEADS = 4
MEM_HEAD_DIM = 512

D_FF = 11008
D_FF_PAD = 11264
DT_PAD = 128

N_SAMPLE_ROWS = DEC_BATCH * DEC_SEQ
N_ROWS = SEQ + N_SAMPLE_ROWS
ROW_TILE = 768
WRES_ROW_TILE = 384
SUB_ROWS = 256
SUBS_PER_TILE = ROW_TILE // SUB_ROWS

V7X_VMEM_LIMIT_BYTES = 59 * 1024 * 1024


def _params(*sem):
    return pltpu.CompilerParams(dimension_semantics=sem, vmem_limit_bytes=V7X_VMEM_LIMIT_BYTES)


def _sigmoid(x):
    return 1.0 / (1.0 + jnp.exp(-x))


def _softplus(x):
    return jnp.maximum(x, 0.0) + jnp.log1p(jnp.exp(-jnp.abs(x)))


def _dot(a, b):
    return jnp.dot(a, b, preferred_element_type=F32)


def _dot_nt(a, b):
    return lax.dot_general(a, b, (((1,), (1,)), ((), ())), preferred_element_type=F32)


def _dot_exact(a, b):
    return jnp.dot(a, b, precision=lax.Precision.HIGHEST, preferred_element_type=F32)


def _rms_scale(x):
    return x * lax.rsqrt(jnp.mean(x * x, axis=-1, keepdims=True) + RMS_EPS)


def _rmsnorm_body(x_ref, g_ref, *o_refs):
    xn = _rms_scale(x_ref[...])
    for j, o_ref in enumerate(o_refs):
        o_ref[...] = (xn * g_ref[j:j + 1, :]).astype(o_ref.dtype)


def rmsnorm(x, gains, tm=256):
    m, d = x.shape
    n = gains.shape[0]
    return pl.pallas_call(
        _rmsnorm_body,
        grid=(m // tm,),
        in_specs=[pl.BlockSpec((tm, d), lambda i: (i, 0)),
                  pl.BlockSpec((n, d), lambda i: (0, 0))],
        out_specs=[pl.BlockSpec((tm, d), lambda i: (i, 0))] * n,
        out_shape=[jax.ShapeDtypeStruct((m, d), BF16)] * n,
        compiler_params=_params("parallel"),
        name="rmsnorm",
    )(x, gains)


def _rmsnorm2_body(xp_ref, xs_ref, g_ref, o_ref, *, n_p):
    i = pl.program_id(0)

    @pl.when(i < n_p)
    def _():
        o_ref[...] = (_rms_scale(xp_ref[...]) * g_ref[...]).astype(o_ref.dtype)

    @pl.when(i >= n_p)
    def _():
        o_ref[...] = (_rms_scale(xs_ref[...]) * g_ref[...]).astype(o_ref.dtype)


def rmsnorm_rows2(xp, xs, gain):
    tm = SUB_ROWS
    d = xp.shape[1]
    n_p, n_s = xp.shape[0] // tm, xs.shape[0] // tm
    return pl.pallas_call(
        functools.partial(_rmsnorm2_body, n_p=n_p),
        grid=(n_p + n_s,),
        in_specs=[pl.BlockSpec((tm, d), lambda i: (jnp.minimum(i, n_p - 1), 0)),
                  pl.BlockSpec((tm, d), lambda i: (jnp.maximum(i - n_p, 0), 0)),
                  pl.BlockSpec((1, d), lambda i: (0, 0))],
        out_specs=pl.BlockSpec((tm, d), lambda i: (i, 0)),
        out_shape=jax.ShapeDtypeStruct((xp.shape[0] + xs.shape[0], d), BF16),
        compiler_params=_params("parallel"),
        name="rmsnorm_rows2",
    )(xp, xs, gain.reshape(1, d))


def _side_cast_plan(jobs, n_steps, step_of):
    in_specs, args, out_specs, out_shapes, metas = [], [], [], [], []
    for w, layer, rows_out, cols_out, rb, gain in jobs:
        r, n = w.shape[1], w.shape[2]
        n_blocks, n_src = rows_out // rb, r // rb
        assert rows_out % rb == 0 and r % rb == 0 and n_blocks <= n_steps and n % 128 == 0
        src = lambda *g, layer=layer, n_src=n_src: (layer, jnp.minimum(step_of(*g), n_src - 1), 0)
        in_specs.append(pl.BlockSpec((None, rb, n), src))
        args.append(w)
        if gain is not None:
            in_specs.append(pl.BlockSpec((rb, 1), lambda *g, n_src=n_src: (jnp.minimum(step_of(*g), n_src - 1), 0)))
            args.append(gain.reshape(r, 1).astype(F32))
        out_specs.append(pl.BlockSpec((rb, cols_out), lambda *g, n_blocks=n_blocks: (jnp.minimum(step_of(*g), n_blocks - 1), 0)))
        out_shapes.append(jax.ShapeDtypeStruct((rows_out, cols_out), BF16))
        metas.append((r, rb, n_blocks, gain is not None))
    return in_specs, args, out_specs, out_shapes, metas


def _side_cast_step(metas, in_refs, out_refs, step):
    in_refs = list(in_refs)
    for (rows_valid, rb, n_blocks, has_gain), o_ref in zip(metas, out_refs):
        w_ref = in_refs.pop(0)
        x = w_ref[...]
        if has_gain:
            x = x * in_refs.pop(0)[...]
        if n_blocks * rb > rows_valid:
            row = jnp.minimum(step, n_blocks - 1) * rb + lax.broadcasted_iota(jnp.int32, x.shape, 0)
            x = jnp.where(row < rows_valid, x, 0.0)
        if o_ref.shape[1] > x.shape[1]:
            x = jnp.concatenate([x, jnp.zeros((rb, o_ref.shape[1] - x.shape[1]), F32)], axis=1)
        o_ref[...] = x.astype(o_ref.dtype)


def _wres_body(x_ref, *refs, n_w, layer, col0, n_valid, tn, transposed, epilogue, side, n_side_in, normed):
    w_hbm, refs = refs[:n_w], refs[n_w:]
    side_in, refs = refs[:n_side_in], refs[n_side_in:]
    ssq_ref = gain_ref = None
    if normed:
        ssq_ref, gain_ref, refs = refs[0], refs[1], refs[2:]
    o_ref, refs = refs[0], refs[1:]
    side_out, refs = refs[:len(side)], refs[len(side):]
    wbf, wf32, sem = refs[:n_w], refs[n_w:2 * n_w], refs[2 * n_w]
    j, i = pl.program_id(0), pl.program_id(1)
    nj = pl.num_programs(0)
    rem = n_valid % tn

    _side_cast_step(side, side_in, side_out, j * pl.num_programs(1) + i)

    def for_each_copy(jj, slot, fn):
        def go(width):
            for w in range(n_w):
                if transposed:
                    src = w_hbm[w].at[layer, pl.ds(col0 + jj * tn, width), :]
                    dst = wf32[w].at[slot, pl.ds(0, width), :]
                else:
                    src = w_hbm[w].at[layer, :, pl.ds(col0 + jj * tn, width)]
                    dst = wf32[w].at[slot, :, pl.ds(0, width)]
                fn(pltpu.make_async_copy(src, dst, sem.at[w, slot]))
        if rem:
            pl.when(jj < nj - 1)(lambda: go(tn))
            pl.when(jj == nj - 1)(lambda: go(rem))
        else:
            go(tn)

    @pl.when(i == 0)
    def _():
        slot = j % 2

        start = lambda cp: cp.start(priority=1)

        @pl.when(j == 0)
        def _():
            for_each_copy(0, 0, start)

        for_each_copy(j, slot, lambda cp: cp.wait())

        @pl.when(j + 1 < nj)
        def _():
            for_each_copy(j + 1, 1 - slot, start)

        def convert(ragged):
            for w in range(n_w):
                wt = wf32[w][slot]
                if ragged:
                    col = lax.broadcasted_iota(jnp.int32, wt.shape, 0 if transposed else 1)
                    wt = jnp.where(col < rem, wt, 0.0)
                if normed:
                    wt = wt * jnp.concatenate([gain_ref[...]] * (tn // 128), axis=1)
                wbf[w][...] = (wt.T if transposed else wt).astype(BF16)

        if rem:
            pl.when(j < nj - 1)(lambda: convert(False))
            pl.when(j == nj - 1)(lambda: convert(True))
        else:
            convert(False)

    x = x_ref[...]
    accs = [_dot(x, wb[...]) for wb in wbf]
    if normed:
        inv = lax.rsqrt(jnp.sum(ssq_ref[...], axis=-1, keepdims=True) / x.shape[1] + RMS_EPS)
        accs = [acc * inv for acc in accs]
    o_ref[...] = epilogue(*accs).astype(o_ref.dtype)


def matmul_wres(x, ws, layer, n_out, out_dtype, tm, tn, epilogue=lambda acc: acc, transposed=False, col0=0,
                side_casts=(), norm=None):
    m, k = x.shape
    n_w = len(ws)
    n_valid = min(n_out, ws[0].shape[1 if transposed else 2] - col0)
    nj, ni = n_out // tn, m // tm
    s_in, s_args, s_out, s_shape, side = _side_cast_plan(side_casts, nj * ni, lambda j, i: j * ni + i)
    in_specs = [pl.BlockSpec((tm, k), lambda j, i: (i, 0))] + [pl.BlockSpec(memory_space=pl.ANY)] * n_w + s_in
    args = [x, *ws, *s_args]
    if norm:
        assert not transposed
        ssq, gain = norm
        in_specs += [pl.BlockSpec((tm, ssq.shape[1]), lambda j, i: (i, 0)),
                     pl.BlockSpec((k, 128), lambda j, i: (0, 0), pipeline_mode=pl.Buffered(1))]
        args += [ssq, jnp.broadcast_to(gain.astype(F32)[:, None], (k, 128))]
    outs = pl.pallas_call(
        functools.partial(_wres_body, n_w=n_w, layer=layer, col0=col0, n_valid=n_valid, tn=tn,
                          transposed=transposed, epilogue=epilogue, side=side, n_side_in=len(s_in),
                          normed=bool(norm)),
        grid=(nj, ni),
        in_specs=in_specs,
        out_specs=[pl.BlockSpec((tm, tn), lambda j, i: (i, j))] + s_out,
        out_shape=[jax.ShapeDtypeStruct((m, n_out), out_dtype)] + s_shape,
        scratch_shapes=[pltpu.VMEM((k, tn), BF16)] * n_w
                       + [pltpu.VMEM((2, tn, k) if transposed else (2, k, tn), F32)] * n_w
                       + [pltpu.SemaphoreType.DMA((n_w, 2))],
        compiler_params=_params("arbitrary", "arbitrary"),
        name="matmul_wres",
    )(*args)
    return outs if side_casts else outs[0]


def _ffn_up_body(x_ref, wg_ref, wu_ref, ssq_ref, *refs, side, n_side_in):
    side_in, o_ref, side_out = refs[:n_side_in], refs[n_side_in], refs[n_side_in + 1:]
    _side_cast_step(side, side_in, side_out, pl.program_id(0) * pl.num_programs(1) + pl.program_id(1))
    x = x_ref[...]
    inv = lax.rsqrt(jnp.sum(ssq_ref[...], axis=-1, keepdims=True) / x.shape[1] + RMS_EPS)
    o_ref[...] = _silu_gate(_dot(x, wg_ref[...]) * inv, _dot(x, wu_ref[...]) * inv).astype(o_ref.dtype)


def ffn_up(x, ssq, wg, wu, tm, tn, side_casts=()):
    m, k = x.shape
    n = wg.shape[1]
    ni, nj = m // tm, n // tn
    s_in, s_args, s_out, s_shape, side = _side_cast_plan(side_casts, ni * nj, lambda i, j: i * nj + j)
    outs = pl.pallas_call(
        functools.partial(_ffn_up_body, side=side, n_side_in=len(s_in)),
        grid=(ni, nj),
        in_specs=[pl.BlockSpec((tm, k), lambda i, j: (i, 0)),
                  pl.BlockSpec((k, tn), lambda i, j: (0, j)),
                  pl.BlockSpec((k, tn), lambda i, j: (0, j)),
                  pl.BlockSpec((tm, ssq.shape[1]), lambda i, j: (i, 0))] + s_in,
        out_specs=[pl.BlockSpec((tm, tn), lambda i, j: (i, j))] + s_out,
        out_shape=[jax.ShapeDtypeStruct((m, n), BF16)] + s_shape,
        compiler_params=_params("arbitrary", "arbitrary"),
        name="ffn_up",
    )(x, wg, wu, ssq, *s_args)
    return outs if side_casts else outs[0]


def _lane_group_sumsq(h):
    return functools.reduce(lambda a, b: a + b, [h[:, c:c + 128] * h[:, c:c + 128] for c in range(0, h.shape[1], 128)])


def _mm_out_body(xap_ref, xas_ref, xbp_ref, xbs_ref, w_ref, rp_ref, rs_ref, o_ref, o16_ref, ssq_ref, *, ka):
    last_tile = pl.program_id(0) == pl.num_programs(0) - 1
    head = (SUBS_PER_TILE - 1) * SUB_ROWS

    def emit(rows, xa, xb, res):
        h = res + _dot(xa, w_ref[0:ka, :]) + _dot(xb, w_ref[ka:, :])
        o_ref[rows, :] = h
        o16_ref[rows, :] = h.astype(BF16)
        ssq_ref[rows, :] = _lane_group_sumsq(h)

    @pl.when(jnp.logical_not(last_tile))
    def _():
        emit(slice(None), xap_ref[...].reshape(ROW_TILE, -1), xbp_ref[...].reshape(ROW_TILE, -1),
             rp_ref[...].reshape(ROW_TILE, -1))

    @pl.when(last_tile)
    def _():
        n_head = SUBS_PER_TILE - 1
        emit(slice(0, head), xap_ref[0:n_head].reshape(head, -1), xbp_ref[0:n_head].reshape(head, -1),
             rp_ref[0:n_head].reshape(head, -1))
        emit(slice(head, ROW_TILE), xas_ref[0], xbs_ref[0], rs_ref[0])


def matmul_out(xa_p, xa_s, xb_p, xb_s, w, r_p, r_s, r_s_block, tn):
    sub3 = lambda a: a.reshape(a.shape[0] // SUB_ROWS, SUB_ROWS, a.shape[1])
    ka, kb, n = xa_p.shape[1], xb_p.shape[1], w.shape[1]
    nt = N_ROWS // ROW_TILE
    spt = SUBS_PER_TILE
    once = dict(pipeline_mode=pl.Buffered(1))
    return pl.pallas_call(
        functools.partial(_mm_out_body, ka=ka),
        grid=(nt, n // tn),
        in_specs=[pl.BlockSpec((spt, SUB_ROWS, ka), lambda i, j: (i, 0, 0)),
                  pl.BlockSpec((1, SUB_ROWS, ka), lambda i, j: (0, 0, 0), **once),
                  pl.BlockSpec((spt, SUB_ROWS, kb), lambda i, j: (i, 0, 0)),
                  pl.BlockSpec((1, SUB_ROWS, kb), lambda i, j: (0, 0, 0), **once),
                  pl.BlockSpec((ka + kb, tn), lambda i, j: (0, j)),
                  pl.BlockSpec((spt, SUB_ROWS, tn), lambda i, j: (i, 0, j)),
                  pl.BlockSpec((1, SUB_ROWS, tn), lambda i, j: (r_s_block, 0, j))],
        out_specs=[pl.BlockSpec((ROW_TILE, tn), lambda i, j: (i, j)),
                   pl.BlockSpec((ROW_TILE, tn), lambda i, j: (i, j)),
                   pl.BlockSpec((ROW_TILE, 128), lambda i, j: (i, j))],
        out_shape=[jax.ShapeDtypeStruct((N_ROWS, n), F32),
                   jax.ShapeDtypeStruct((N_ROWS, n), BF16),
                   jax.ShapeDtypeStruct((N_ROWS, 128 * (n // tn)), F32)],
        compiler_params=_params("parallel", "arbitrary"),
        name="matmul_out",
    )(sub3(xa_p), sub3(xa_s), sub3(xb_p), sub3(xb_s), w, sub3(r_p), sub3(r_s))


def _silu_gate(g, u):
    return g * _sigmoid(g) * u


def _ffn_down_body(x_ref, w_ref, r_ref, o_ref, *rest):
    acc_ref, stat_refs = rest[-1], rest[:-1]
    k = pl.program_id(2)

    @pl.when(k == 0)
    def _():
        acc_ref[...] = r_ref[...]

    acc_ref[...] += _dot(x_ref[...], w_ref[...])

    @pl.when(k == pl.num_programs(2) - 1)
    def _():
        h = acc_ref[...]
        o_ref[...] = h
        if stat_refs:
            o16_ref, ssq_ref = stat_refs
            o16_ref[...] = h.astype(BF16)
            ssq_ref[...] = _lane_group_sumsq(h)


def ffn_down(x, w, res, *, row0, rows, tm, tn, tk, norm_stats=False):
    kk = x.shape[1]
    n = w.shape[1]
    r0 = row0 // tm
    out_specs = [pl.BlockSpec((tm, tn), lambda i, j, k: (i, j))]
    out_shape = [jax.ShapeDtypeStruct((rows, n), F32)]
    if norm_stats:
        out_specs += [pl.BlockSpec((tm, tn), lambda i, j, k: (i, j)), pl.BlockSpec((tm, 128), lambda i, j, k: (i, j))]
        out_shape += [jax.ShapeDtypeStruct((rows, n), BF16), jax.ShapeDtypeStruct((rows, 128 * (n // tn)), F32)]
    outs = pl.pallas_call(
        _ffn_down_body,
        grid=(rows // tm, n // tn, kk // tk),
        in_specs=[pl.BlockSpec((tm, tk), lambda i, j, k: (r0 + i, k)),
                  pl.BlockSpec((tk, tn), lambda i, j, k: (k, j)),
                  pl.BlockSpec((tm, tn), lambda i, j, k: (r0 + i, j))],
        out_specs=out_specs,
        out_shape=out_shape,
        scratch_shapes=[pltpu.VMEM((tm, tn), F32)],
        compiler_params=_params("parallel", "arbitrary", "arbitrary"),
        name="ffn_down",
    )(x, w, res)
    return outs if norm_stats else outs[0]


def _kv_post_body(kv_ref, g_ref, k32_ref, k16_ref, v16_ref, *, width, head_dim, pad_blocks):
    keep = pl.program_id(0) >= pad_blocks
    for h in range(width // head_dim):
        sl = slice(h * head_dim, (h + 1) * head_dim)
        kn = _rms_scale(kv_ref[:, sl]) * g_ref[...]
        k32_ref[:, sl] = kn
        k16_ref[:, sl] = jnp.where(keep, kn, 0.0).astype(BF16)
    v16_ref[...] = jnp.where(keep, kv_ref[:, width:], 0.0).astype(BF16)


def kv_post(kv, gain, head_dim, pad_rows=0, tm=256):
    m, w2 = kv.shape
    width = w2 // 2
    pb = pad_rows // tm
    src = lambda i: (jnp.maximum(i - pb, 0), 0)
    return pl.pallas_call(
        functools.partial(_kv_post_body, width=width, head_dim=head_dim, pad_blocks=pb),
        grid=(pb + m // tm,),
        in_specs=[pl.BlockSpec((tm, w2), src),
                  pl.BlockSpec((1, head_dim), lambda i: (0, 0))],
        out_specs=[pl.BlockSpec((tm, width), src)] + [pl.BlockSpec((tm, width), lambda i: (i, 0))] * 2,
        out_shape=[jax.ShapeDtypeStruct((m, width), F32),
                   jax.ShapeDtypeStruct((pad_rows + m, width), BF16),
                   jax.ShapeDtypeStruct((pad_rows + m, width), BF16)],
        compiler_params=_params("arbitrary"),
        name="kv_post",
    )(kv, gain.reshape(1, head_dim))


def _ssd_body(z_ref, xbc0_ref, xbc1_ref, xbc2_ref, xbc3_ref, dt_ref, dtT_ref, conv0_ref, st0_ref, cw_ref, cb_ref,
              dtb_ref, dtbT_ref, alog_ref, alogT_ref, dskip_ref, gssm_ref, *rest, T, cps, side, n_side_in):
    side_in, (y_ref, convn_ref, stn_ref), rest = rest[:n_side_in], rest[n_side_in:n_side_in + 3], rest[n_side_in + 3:]
    side_out, (xbuf, xc_all, st, yacc_all) = rest[:len(side)], rest[len(side):]
    c = pl.program_id(1)
    nc = pl.num_programs(1)
    _side_cast_step(side, side_in, side_out, pl.program_id(0) * nc + c)

    @pl.when(c == 0)
    def _():
        xbuf[0:8, :] = conv0_ref[0]
        st[...] = st0_ref[0]

    refs = (z_ref, (xbc0_ref, xbc1_ref, xbc2_ref, xbc3_ref), dt_ref, dtT_ref, cw_ref, cb_ref,
            dtb_ref, dtbT_ref, alog_ref, alogT_ref, dskip_ref, gssm_ref, y_ref, convn_ref, xbuf, st)
    for cl in range(cps):
        _ssd_chunk(cl, *refs, xc_all.at[cl], yacc_all.at[cl], T=T)

    @pl.when(c == nc - 1)
    def _():
        stn_ref[0] = st[...]


def _ssd_chunk(cl, z_ref, xbc_refs, dt_ref, dtT_ref, cw_ref, cb_ref, dtb_ref, dtbT_ref, alog_ref, alogT_ref,
               dskip_ref, gssm_ref, y_ref, convn_ref, xbuf, st, xc, yacc, *, T):
    rows = slice(cl * T, (cl + 1) * T)
    P, N = SSM_HEAD_DIM, SSM_STATE
    T2, PW = 2 * T, 2 * P
    tail = CONV_WIDTH - 1

    for q, xbc_ref in enumerate(xbc_refs):
        xbuf[8:8 + T, q * XBC_BLOCK:(q + 1) * XBC_BLOCK] = xbc_ref[rows, :].astype(F32)
    acc = cb_ref[...] + cw_ref[0:1, :] * xbuf[8 - tail:8 - tail + T, :]
    for tap in range(1, CONV_WIDTH):
        acc = acc + cw_ref[tap:tap + 1, :] * xbuf[8 - tail + tap:8 - tail + tap + T, :]
    xc[...] = acc * _sigmoid(acc)
    last_rows = xbuf[T:T + 8, :]
    convn_ref[0] = last_rows
    xbuf[0:8, :] = last_rows

    iota = lambda shape, d: lax.broadcasted_iota(jnp.int32, shape, d)
    tri = iota((T, T), 1) <= iota((T, T), 0)
    dt = _softplus(dt_ref[rows, 0:SSM_HEADS] + dtb_ref[...])
    cum = _dot_exact(tri.astype(F32), dt * (-jnp.exp(alog_ref[...])))

    half = lambda x, size: jnp.where(x >= size, 1, 0)
    ra, rb = iota((T2, T2), 0), iota((T2, T2), 1)
    same = half(ra, T) == half(rb, T)
    dtT = _softplus(dtT_ref[cl] + dtbT_ref[...])
    dtaT = dtT * (-jnp.exp(alogT_ref[...]))
    cumT = _dot_exact(dtaT, jnp.where(same & (ra <= rb), 1.0, 0.0))
    lastT = _dot_exact(dtaT, jnp.where(same, 1.0, 0.0))
    same_p = half(iota((T2, PW), 0), T) == half(iota((T2, PW), 1), P)
    edec = jnp.exp(_dot_exact(dtaT, jnp.where(same_p, 1.0, 0.0)))
    wT = dtT * jnp.exp(lastT - cumT)

    left = iota((T, PW), 1) < P
    left2 = iota((T, T2), 1) < T
    tri2 = (iota((T, T2), 1) - T * half(iota((T, T2), 1), T)) <= iota((T, T2), 0)
    bc_off = SSM_INNER
    cc_off = SSM_INNER + SSM_GROUPS * N

    for g in range(SSM_GROUPS):
        bg = xc[:, bc_off + g * N:bc_off + (g + 1) * N]
        cg16 = xc[:, cc_off + g * N:cc_off + (g + 1) * N].astype(BF16)
        bg2 = jnp.concatenate([bg, bg], axis=0)
        cb2 = _dot_nt(cg16, bg2.astype(BF16))
        bgT2 = bg2.T
        gw = SSM_PPG * PW
        yoff = _dot(cg16, st[:, g * gw:(g + 1) * gw].astype(BF16))
        for k in range(SSM_PPG):
            q = g * SSM_PPG + k
            ps = slice(q * PW, (q + 1) * PW)
            b0 = jnp.broadcast_to(cum[:, 2 * q:2 * q + 1], (T, PW))
            b1 = jnp.broadcast_to(cum[:, 2 * q + 1:2 * q + 2], (T, PW))
            col_p = jnp.where(left, b0, b1)
            col_2 = col_p if T2 == PW else jnp.where(left2, b0[:, 0:T2], b1[:, 0:T2])
            dec = jnp.where(tri2, jnp.exp(col_2 - cumT[q:q + 1, :]), 0.0)
            m2 = (cb2 * dec * dtT[q:q + 1, :]).astype(BF16)
            xp = xc[:, ps]
            bd = jnp.concatenate([jnp.where(left, xp, 0.0), jnp.where(left, 0.0, xp)],
                                 axis=0).astype(BF16)
            yacc[:, ps] = _dot(m2, bd) + yoff[:, k * PW:(k + 1) * PW] * jnp.exp(col_p)
            btw = (bgT2 * wT[q:q + 1, :]).astype(BF16)
            st[:, ps] = st[:, ps] * edec[q:q + 1, :] + _dot(btw, bd)

    y = yacc[...] + xc[:, 0:SSM_INNER] * dskip_ref[...]
    z = z_ref[rows, :].astype(F32)
    yg = y * (z * _sigmoid(z))
    y_ref[rows, :] = (_rms_scale(yg) * gssm_ref[...]).astype(y_ref.dtype)


def ssd_mixer(u, dt_raw, conv0, st0, conv_w, conv_b, dt_bias, a_log, d_skip, g_ssm,
              *, row0, batch, nc, T, cps, side_casts=()):
    rows = batch * nc * T
    nblk = batch * nc
    TS = cps * T
    r0 = row0 // TS
    nc = nc // cps
    dtT = (dt_raw[row0:row0 + rows, :SSM_HEADS].reshape(nblk, T, SSM_PAIRS, 2)
           .transpose(0, 2, 3, 1).reshape(nblk, SSM_PAIRS, 2 * T))
    pair_rows = lambda v: jnp.repeat(v.astype(F32).reshape(SSM_PAIRS, 2), T, axis=1)
    row = lambda v: v.reshape(1, -1).astype(F32)
    const = lambda shape: pl.BlockSpec(shape, lambda b, c: (0,) * len(shape))
    xbc_specs = [pl.BlockSpec((TS, XBC_BLOCK), functools.partial(
        lambda b, c, q: (r0 + b * nc + c, SSM_INNER // XBC_BLOCK + q), q=q)) for q in range(CONV_DIM // XBC_BLOCK)]
    s_in, s_args, s_out, s_shape, side = _side_cast_plan(side_casts, batch * nc, lambda b, c: b * nc + c)
    return pl.pallas_call(
        functools.partial(_ssd_body, T=T, cps=cps, side=side, n_side_in=len(s_in)),
        grid=(batch, nc),
        in_specs=[
            pl.BlockSpec((TS, SSM_INNER), lambda b, c: (r0 + b * nc + c, 0)),
            *xbc_specs,
            pl.BlockSpec((TS, DT_PAD), lambda b, c: (r0 + b * nc + c, 0)),
            pl.BlockSpec((cps, SSM_PAIRS, 2 * T), lambda b, c: (b * nc + c, 0, 0)),
            pl.BlockSpec((1, 8, CONV_DIM), lambda b, c: (b, 0, 0)),
            pl.BlockSpec((1, SSM_STATE, SSM_INNER), lambda b, c: (b, 0, 0)),
            const((CONV_WIDTH, CONV_DIM)), const((1, CONV_DIM)),
            const((1, SSM_HEADS)), const((SSM_PAIRS, 2 * T)),
            const((1, SSM_HEADS)), const((SSM_PAIRS, 2 * T)),
            const((1, SSM_INNER)), const((1, SSM_INNER)),
            *s_in,
        ],
        out_specs=[
            pl.BlockSpec((TS, SSM_INNER), lambda b, c: (b * nc + c, 0)),
            pl.BlockSpec((1, 8, CONV_DIM), lambda b, c: (b, 0, 0)),
            pl.BlockSpec((1, SSM_STATE, SSM_INNER), lambda b, c: (b, 0, 0)),
            *s_out,
        ],
        out_shape=[
            jax.ShapeDtypeStruct((rows, SSM_INNER), BF16),
            jax.ShapeDtypeStruct((batch, 8, CONV_DIM), F32),
            jax.ShapeDtypeStruct((batch, SSM_STATE, SSM_INNER), F32),
            *s_shape,
        ],
        scratch_shapes=[
            pltpu.VMEM((8 + T, CONV_DIM), F32),
            pltpu.VMEM((cps, T, CONV_DIM), F32),
            pltpu.VMEM((SSM_STATE, SSM_INNER), F32),
            pltpu.VMEM((cps, T, SSM_INNER), F32),
        ],
        compiler_params=_params("parallel", "arbitrary"),
        name="ssd_mixer",
    )(u, u, u, u, u, dt_raw, dtT, conv0, st0, conv_w.T.astype(F32), row(conv_b),
      row(dt_bias), pair_rows(dt_bias), row(a_log), pair_rows(a_log),
      row(jnp.repeat(d_skip, SSM_HEAD_DIM)), row(g_ssm), *s_args)


def _band_body(q_ref, k_ref, v_ref, tab_ref, qg_ref, *rest, T, KB, pad, cps, side, n_side_in):
    side_in, o_ref, side_out, bias_var = rest[:n_side_in], rest[n_side_in], rest[n_side_in + 1:-1], rest[-1]
    c = pl.program_id(2)
    step = (pl.program_id(0) * pl.num_programs(1) + pl.program_id(1)) * pl.num_programs(2) + c
    _side_cast_step(side, side_in, side_out, step)
    KA = BAND_CONST_KEYS
    d = B_HEAD_DIM

    @pl.when(c == 0)
    def _():
        for j in range(B_GQ):
            tj = tab_ref[0, j:j + 1, :]
            band = pltpu.roll(jnp.broadcast_to(tj, (T, 2 * d)), 2 * d - REL_CLIP, 1,
                              stride=1, stride_axis=0)
            bias_var[j * T:(j + 1) * T, :] = band[:, 0:KB - KA] - tj[:, 0:1]

    scale = 1.0 / math.sqrt(d)

    def chunk(cl, masked):
        start = pl.multiple_of((c * cps + cl) * T, T)
        k = k_ref[0, pl.ds(start, KB), :]
        v = v_ref[0, pl.ds(start, KB), :]
        q = jnp.concatenate(
            [(_rms_scale(q_ref[cl * T:(cl + 1) * T, j * d:(j + 1) * d].astype(F32)) * qg_ref[...] * scale
              ).astype(BF16) for j in range(B_GQ)], axis=0)
        s = _dot_nt(q, k)
        sa = s[:, 0:KA]
        sb = s[:, KA:KB] + bias_var[...]
        if masked:
            sa = jnp.where(start + lax.broadcasted_iota(jnp.int32, (1, KA), 1) >= pad, sa, -jnp.inf)
            sb = jnp.where(start + KA + lax.broadcasted_iota(jnp.int32, (1, KB - KA), 1) >= pad, sb, -jnp.inf)
        m = jnp.maximum(jnp.max(sa, axis=-1, keepdims=True), jnp.max(sb, axis=-1, keepdims=True))
        pa = jnp.exp((sa - m).astype(BF16))
        pb = jnp.exp((sb - m).astype(BF16))
        v1 = jnp.concatenate([v, jnp.ones_like(v)], axis=1)
        ol = _dot(pa, v1[0:KA]) + _dot(pb, v1[KA:KB])
        o = ol[:, 0:d] / ol[:, d:2 * d]
        for j in range(B_GQ):
            o_ref[cl * T:(cl + 1) * T, j * d:(j + 1) * d] = o[j * T:(j + 1) * T].astype(o_ref.dtype)

    if pad:
        @pl.when(c * (cps * T) < pad)
        def _():
            for cl in range(cps):
                chunk(cl, True)

        @pl.when(c * (cps * T) >= pad)
        def _():
            for cl in range(cps):
                chunk(cl, False)
    else:
        for cl in range(cps):
            chunk(cl, False)


def band_attention(u, k, v, table, q_gain, *, row0, batch, nc, T, KB, pad, cps, side_casts=()):
    rows = batch * nc * T
    r0 = row0 // (cps * T)
    ns = nc // cps
    ktot = (nc - 1) * T + KB
    gw = B_GQ * B_HEAD_DIM
    tab = table[:, jnp.clip(2 * B_HEAD_DIM - jnp.arange(2 * B_HEAD_DIM), 0, 2 * REL_CLIP)].astype(F32)
    tab = tab.reshape(B_KV_HEADS, B_GQ, 2 * B_HEAD_DIM)
    s_in, s_args, s_out, s_shape, side = _side_cast_plan(
        side_casts, batch * B_KV_HEADS * ns, lambda b, g, c: (b * B_KV_HEADS + g) * ns + c)
    outs = pl.pallas_call(
        functools.partial(_band_body, T=T, KB=KB, pad=pad, cps=cps, side=side, n_side_in=len(s_in)),
        grid=(batch, B_KV_HEADS, ns),
        in_specs=[
            pl.BlockSpec((cps * T, gw), lambda b, g, c: (r0 + b * ns + c, g)),
            pl.BlockSpec((1, ktot, B_HEAD_DIM), lambda b, g, c: (b, 0, g)),
            pl.BlockSpec((1, ktot, B_HEAD_DIM), lambda b, g, c: (b, 0, g)),
            pl.BlockSpec((1, B_GQ, 2 * B_HEAD_DIM), lambda b, g, c: (g, 0, 0)),
            pl.BlockSpec((1, B_HEAD_DIM), lambda b, g, c: (0, 0)),
            *s_in,
        ],
        out_specs=[pl.BlockSpec((cps * T, gw), lambda b, g, c: (b * ns + c, g)), *s_out],
        out_shape=[jax.ShapeDtypeStruct((rows, B_Q_WIDTH), BF16), *s_shape],
        scratch_shapes=[pltpu.VMEM((B_GQ * T, KB - BAND_CONST_KEYS), F32)],
        compiler_params=_params("arbitrary", "arbitrary", "arbitrary"),
        name="band_attention",
    )(u, k, v, tab, q_gain.reshape(1, B_HEAD_DIM).astype(F32), *s_args)
    return outs if side_casts else outs[0]


def _mem_body(q_ref, k_ref, v_ref, qg_ref, o_ref, kb, vb):
    @pl.when(pl.program_id(1) == 0)
    def _():
        for h in range(MEM_HEADS):
            kb[h] = k_ref[:, h, :].astype(BF16)
            vb[h] = v_ref[:, h, :].astype(BF16)

    scale = 1.0 / math.sqrt(MEM_HEAD_DIM)
    for h in range(MEM_HEADS):
        sl = slice(h * MEM_HEAD_DIM, (h + 1) * MEM_HEAD_DIM)
        qn = (_rms_scale(q_ref[:, sl].astype(F32)) * qg_ref[...] * scale).astype(BF16)
        s = _dot_nt(qn, kb[h])
        m = jnp.max(s, axis=-1, keepdims=True)
        p = jnp.exp(s - m)
        l = jnp.sum(p, axis=-1, keepdims=True)
        o_ref[:, sl] = (_dot(p.astype(BF16), vb[h]) / l).astype(o_ref.dtype)


def memory_attention(u, k, v, q_gain, *, q_col_block, layer, row0, batch, nt, tq):
    rows = batch * nt * tq
    r0 = row0 // tq
    return pl.pallas_call(
        _mem_body,
        grid=(batch, nt),
        in_specs=[
            pl.BlockSpec((tq, MEM_WIDTH), lambda b, i: (r0 + b * nt + i, q_col_block)),
            pl.BlockSpec((None, None, MEM_TOKENS, MEM_HEADS, MEM_HEAD_DIM), lambda b, i: (layer, b, 0, 0, 0)),
            pl.BlockSpec((None, None, MEM_TOKENS, MEM_HEADS, MEM_HEAD_DIM), lambda b, i: (layer, b, 0, 0, 0)),
            pl.BlockSpec((1, MEM_HEAD_DIM), lambda b, i: (0, 0)),
        ],
        out_specs=pl.BlockSpec((tq, MEM_WIDTH), lambda b, i: (b * nt + i, 0)),
        out_shape=jax.ShapeDtypeStruct((rows, MEM_WIDTH), BF16),
        scratch_shapes=[pltpu.VMEM((MEM_HEADS, MEM_TOKENS, MEM_HEAD_DIM), BF16)] * 2,
        compiler_params=_params("parallel", "arbitrary"),
        name="memory_attention",
    )(u, k, v, q_gain.reshape(1, MEM_HEAD_DIM).astype(F32))


def kernel(x_prompt, x_sample, mem_prompt, state_ssm, state_conv, cache_kv_k, cache_kv_v, cache_mem_k, cache_mem_v, g_mix, w_in_a, conv_w, conv_b, dt_bias, a_log, d_skip, g_ssm, w_in_b, rel_bias, q_norm_b, g_kv, w_kv, k_norm_kv, g_mem, w_mem_kv, q_norm_mem, k_norm_mem, w_out, g_ffn, w_ffn_gate, w_ffn_up, w_ffn_down):
    S, NS = SEQ, N_SAMPLE_ROWS
    bf = lambda w: w.astype(BF16)

    main_w = SSM_INNER + CONV_DIM
    w_in_a_t = jnp.swapaxes(w_in_a, 1, 2)
    in_a = functools.partial(matmul_wres, ws=[w_in_a_t], layer=0, tm=WRES_ROW_TILE, transposed=True)

    xp = x_prompt.reshape(S, D_MODEL)
    xs = x_sample.reshape(NS, D_MODEL)

    mem_k32, mem_v32 = [], []
    mem_x = mem_prompt.reshape(MEM_TOKENS, D_MODEL)
    mem_shape = (1, MEM_TOKENS, MEM_HEADS, MEM_HEAD_DIM)
    for l in range(2):
        (mn,) = rmsnorm(mem_x, g_mem[l].reshape(1, -1))
        kv = matmul_wres(mn, [w_mem_kv], l, 2 * MEM_WIDTH, F32, MEM_TOKENS, 1024)
        k32, _, _ = kv_post(kv, k_norm_mem[l], MEM_HEAD_DIM)
        mem_k32.append(k32.reshape(mem_shape))
        mem_v32.append(kv[:, MEM_WIDTH:].reshape(mem_shape))

    mem_k_prompt, mem_v_prompt = jnp.stack(mem_k32), jnp.stack(mem_v32)

    def mem_attn(u, q_col_block, l):
        attn = functools.partial(memory_attention, u, q_gain=q_norm_mem[l], q_col_block=q_col_block, layer=l)
        yp = attn(k=mem_k_prompt, v=mem_v_prompt, row0=0, batch=1, nt=S // 512, tq=512)
        ys = attn(k=cache_mem_k, v=cache_mem_v, row0=S, batch=DEC_BATCH, nt=1, tq=DEC_SEQ)
        return yp, ys

    hn = rmsnorm_rows2(xp, xs, g_mix[0])
    d_out = w_out.shape[1]
    ffn_in = lambda w, l, rb: (w, l, D_MODEL, D_FF_PAD, rb, g_ffn[l])
    u, wout0 = in_a(hn, n_out=main_w, out_dtype=BF16, tn=1024,
                    side_casts=[(w_out, 0, d_out, D_MODEL, 32, None)])
    dt_raw = in_a(hn, n_out=DT_PAD, out_dtype=F32, tn=DT_PAD, col0=main_w)
    q_mem = in_a(hn, n_out=MEM_WIDTH, out_dtype=BF16, tn=1024, col0=main_w + SSM_HEADS)

    def to_state_t(s):
        b = s.shape[0]
        return s.reshape(b, SSM_INNER, SSM_STATE).transpose(0, 2, 1)

    def from_state_t(s):
        b = s.shape[0]
        return s.transpose(0, 2, 1).reshape(b, SSM_HEADS, SSM_HEAD_DIM, SSM_STATE)

    ssd = functools.partial(ssd_mixer, u, dt_raw, conv_w=conv_w[0], conv_b=conv_b[0], dt_bias=dt_bias[0],
                            a_log=a_log[0], d_skip=d_skip[0], g_ssm=g_ssm[0])
    tail = CONV_WIDTH - 1
    conv0_p = jnp.zeros((1, 8, CONV_DIM), F32)
    st0_p = jnp.zeros((1, SSM_STATE, SSM_INNER), F32)
    ssd_steps = S // CHUNK // SSD_CHUNKS_PER_STEP
    y_p, conv_p, st_p, wg0, wu0 = ssd(
        conv0=conv0_p, st0=st0_p, row0=0, batch=1, nc=S // CHUNK, T=CHUNK, cps=SSD_CHUNKS_PER_STEP,
        side_casts=[ffn_in(w_ffn_gate, 0, D_MODEL // ssd_steps), ffn_in(w_ffn_up, 0, D_MODEL // ssd_steps)])
    conv0_s = jnp.pad(state_conv[0], ((0, 0), (8 - tail, 0), (0, 0)))
    y_s, conv_s, st_s = ssd(conv0=conv0_s, st0=to_state_t(state_ssm[0]), row0=S, batch=DEC_BATCH, nc=1, T=DEC_SEQ,
                            cps=1)
    ym_p, ym_s = mem_attn(q_mem, 0, 0)
    h, h16, ssq = matmul_out(y_p, y_s, ym_p, ym_s, wout0, xp, xs, 0, 512)
    ffn_out = lambda l: (w_ffn_down, l, D_FF_PAD, D_MODEL, 64, None)
    act, wd0 = ffn_up(h16, ssq, wg0, wu0, ROW_TILE, 512, side_casts=[ffn_out(0)])
    h, h16, ssq = ffn_down(act, wd0, h, row0=0, rows=N_ROWS, tm=ROW_TILE, tn=1024, tk=2816, norm_stats=True)

    kvw = B_KV_HEADS * B_HEAD_DIM
    kv = matmul_wres(h16, [w_kv[None]], 0, 2 * kvw, F32, WRES_ROW_TILE, 1024, norm=(ssq, g_kv))
    k32, k16, v16 = kv_post(kv, k_norm_kv, B_HEAD_DIM, pad_rows=BAND_PAST)
    u, wout1 = matmul_wres(h16, [w_in_b], 0, B_Q_WIDTH + MEM_WIDTH, BF16, WRES_ROW_TILE, 1024,
                           side_casts=[(w_out, 1, d_out, D_MODEL, 64, None)], norm=(ssq, g_mix[1]))

    band_steps = B_KV_HEADS * (S // CHUNK // BAND_CHUNKS_PER_STEP)
    yb_p, wg1, wu1 = band_attention(
        u, k16[None], v16[None], rel_bias[0], q_norm_b[0], row0=0, batch=1, nc=S // CHUNK,
        T=CHUNK, KB=BAND_PAST + CHUNK, pad=BAND_PAST, cps=BAND_CHUNKS_PER_STEP,
        side_casts=[ffn_in(w_ffn_gate, 1, D_MODEL // band_steps), ffn_in(w_ffn_up, 1, D_MODEL // band_steps)])

    wc = cache_kv_k.shape[1]
    new_rows = lambda a: a[BAND_PAST + S:].reshape(DEC_BATCH, DEC_SEQ, kvw)
    ks = jnp.concatenate([bf(cache_kv_k).reshape(DEC_BATCH, wc, kvw), new_rows(k16)], axis=1)
    vs = jnp.concatenate([bf(cache_kv_v).reshape(DEC_BATCH, wc, kvw), new_rows(v16)], axis=1)
    yb_s = band_attention(u, ks, vs, rel_bias[0], q_norm_b[0],
                          row0=S, batch=DEC_BATCH, nc=1, T=DEC_SEQ, KB=wc + DEC_SEQ, pad=0, cps=1)
    ym_p, ym_s = mem_attn(u, B_Q_WIDTH // MEM_WIDTH, 1)
    h, h16, ssq = matmul_out(yb_p, yb_s, ym_p, ym_s, wout1, h, h, N_ROWS // SUB_ROWS - 1, 512)
    act, wd1 = ffn_up(h16, ssq, wg1, wu1, ROW_TILE, 512, side_casts=[ffn_out(1)])
    y_prompt = ffn_down(act, wd1, h, row0=0, rows=S, tm=1024, tn=1024, tk=2816)
    y_sample = ffn_down(act, wd1, h, row0=S, rows=NS, tm=NS, tn=1024, tk=2816)

    keep = min(BAND_PAST, S)
    kv_shape = (B_KV_HEADS, B_HEAD_DIM)
    return (
        y_prompt.reshape(1, S, D_MODEL),
        y_sample.reshape(DEC_BATCH, DEC_SEQ, D_MODEL),
        from_state_t(st_p)[None],
        conv_p[:, 8 - tail:][None],
        k32[S - keep:S].reshape(1, keep, *kv_shape),
        kv[S - keep:S, kvw:].reshape(1, keep, *kv_shape),
        mem_k_prompt,
        mem_v_prompt,
        from_state_t(st_s)[None],
        conv_s[:, 8 - tail:][None],
        k32[S:].reshape(DEC_BATCH, DEC_SEQ, *kv_shape),
        kv[S:, kvw:].reshape(DEC_BATCH, DEC_SEQ, *kv_shape),
    )
```
